```python
import math
import jax, jax.numpy as jnp
from jax import lax
import numpy as np

D_MODEL = 1024
BATCH = 2
SEQ = 8192
DEPTH = 1

CHUNK = 64
N_META = 16
ATT_HEADS = 8
HEAD_DIM = 64
ATT_WIDTH = ATT_HEADS * HEAD_DIM
Q_BLOCK = 128
SSM_WIDTH = D_MODEL // 2
SSM_GROUP = 16
SSM_GROUPS = SSM_WIDTH // SSM_GROUP
SSM_STATE = 64
N_EXPERTS = 32
TOP_K = 4
D_EXPERT = D_MODEL
SWIGLU_LIMIT = 7.0
SWIGLU_ALPHA = 1.702
LN_EPS = 1e-5
DEEPNORM_ALPHA = (2.0 * DEPTH) ** 0.25
DEEPNORM_BETA = (8.0 * DEPTH) ** -0.25

Q_OFF = 0
K_OFF = Q_OFF + ATT_WIDTH
V_OFF = K_OFF + ATT_WIDTH
F_OFF = V_OFF + ATT_WIDTH
U_OFF = F_OFF + ATT_HEADS
GA_OFF = U_OFF + SSM_WIDTH
GB_OFF = GA_OFF + D_MODEL
IN_COLS = GB_OFF + D_MODEL

kernel_name = "hybrid_fox_s5_moe_deepnorm"


def layer_norm(x, g, b):
    xf = x.astype(jnp.float32)
    mu = jnp.mean(xf, axis=-1, keepdims=True)
    var = jnp.mean(jnp.square(xf - mu), axis=-1, keepdims=True)
    return ((xf - mu) * lax.rsqrt(var + LN_EPS) * g + b).astype(x.dtype)


def forgetting_attention(q, k, v, log_f):
    bsz, length, n_h, dh = q.shape
    n_blk = -(-length // Q_BLOCK)
    pad = n_blk * Q_BLOCK - length
    to_bhld = lambda a: jnp.pad(a.transpose(0, 2, 1, 3), ((0, 0), (0, 0), (0, pad), (0, 0)))
    q, k, v = to_bhld(q), to_bhld(k), to_bhld(v)
    cum_f = jnp.cumsum(log_f, axis=1).transpose(0, 2, 1)
    cum_f = jnp.pad(cum_f, ((0, 0), (0, 0), (0, pad)))
    k_pos = jnp.arange(n_blk * Q_BLOCK)
    scale = 1.0 / math.sqrt(dh)

    def one_block(i):
        start = i * Q_BLOCK
        q_blk = lax.dynamic_slice_in_dim(q, start, Q_BLOCK, axis=2)
        f_q = lax.dynamic_slice_in_dim(cum_f, start, Q_BLOCK, axis=2)
        s = jnp.einsum('bhqd,bhkd->bhqk', q_blk, k).astype(jnp.float32) * scale
        s = s + f_q[..., None] - cum_f[:, :, None, :]
        q_pos = start + jnp.arange(Q_BLOCK)
        s = jnp.where(k_pos[None, :] <= q_pos[:, None], s, -jnp.inf)
        p = jax.nn.softmax(s, axis=-1).astype(v.dtype)
        return jnp.einsum('bhqk,bhkd->bhqd', p, v)

    out = lax.map(one_block, jnp.arange(n_blk))
    out = out.transpose(1, 0, 3, 2, 4).reshape(bsz, n_blk * Q_BLOCK, n_h * dh)
    return out[:, :length]


def s5_ssm(u, a_re, a_im, log_dt, b_re, b_im, c_re, c_im, d_skip):
    bsz, length, n_g, n_c = u.shape
    f32 = jnp.float32
    a_re = a_re.astype(f32); a_im = a_im.astype(f32)
    dt = jnp.exp(log_dt.astype(f32))[:, None]
    mag = jnp.exp(a_re * dt)
    ang = a_im * dt
    lb_re, lb_im = mag * jnp.cos(ang), mag * jnp.sin(ang)
    den = a_re * a_re + a_im * a_im
    z_re, z_im = lb_re - 1.0, lb_im
    coef_re = (z_re * a_re + z_im * a_im) / den
    coef_im = (z_im * a_re - z_re * a_im) / den
    b_re = b_re.astype(f32); b_im = b_im.astype(f32)
    bb_re = coef_re[..., None] * b_re - coef_im[..., None] * b_im
    bb_im = coef_re[..., None] * b_im + coef_im[..., None] * b_re
    uf = u.astype(f32)
    bu_re = jnp.einsum('blgc,gpc->lbgp', uf, bb_re)
    bu_im = jnp.einsum('blgc,gpc->lbgp', uf, bb_im)
    a_r = jnp.broadcast_to(lb_re, (length, 1, n_g, lb_re.shape[-1]))
    a_i = jnp.broadcast_to(lb_im, (length, 1, n_g, lb_im.shape[-1]))

    def combine(e1, e2):
        ar1, ai1, br1, bi1 = e1
        ar2, ai2, br2, bi2 = e2
        return (ar1 * ar2 - ai1 * ai2,
                ar1 * ai2 + ai1 * ar2,
                ar2 * br1 - ai2 * bi1 + br2,
                ar2 * bi1 + ai2 * br1 + bi2)

    _, _, x_re, x_im = lax.associative_scan(combine, (a_r, a_i, bu_re, bu_im), axis=0)
    y = (jnp.einsum('lbgp,gcp->blgc', x_re, c_re.astype(f32))
         - jnp.einsum('lbgp,gcp->blgc', x_im, c_im.astype(f32))
         + d_skip.astype(f32) * uf)
    return y.reshape(bsz, length, n_g * n_c)


def hybrid_mixer(h, w_in, b_f, w_up_a, w_up_b, w_o, a_re, a_im, log_dt, b_re, b_im,
                 c_re, c_im, d_skip, w_glu, b_glu):
    bsz, length, _ = h.shape
    proj = h @ w_in
    q = proj[..., Q_OFF:K_OFF].reshape(bsz, length, ATT_HEADS, HEAD_DIM)
    k = proj[..., K_OFF:V_OFF].reshape(bsz, length, ATT_HEADS, HEAD_DIM)
    v = proj[..., V_OFF:F_OFF].reshape(bsz, length, ATT_HEADS, HEAD_DIM)
    log_f = jax.nn.log_sigmoid((proj[..., F_OFF:U_OFF] + b_f).astype(jnp.float32))
    y_a = forgetting_attention(q, k, v, log_f)

    u = proj[..., U_OFF:GA_OFF].reshape(bsz, length, SSM_GROUPS, SSM_GROUP)
    y_b = jax.nn.gelu(s5_ssm(u, a_re, a_im, log_dt, b_re, b_im, c_re, c_im, d_skip)).astype(h.dtype)
    y_b = y_b * jax.nn.sigmoid(y_b @ w_glu + b_glu)

    g_a = jax.nn.sigmoid(proj[..., GA_OFF:GB_OFF])
    g_b = jax.nn.sigmoid(proj[..., GB_OFF:IN_COLS])
    merged = g_a * (y_a @ w_up_a) + g_b * (y_b @ w_up_b)
    return merged @ w_o


def moe_ffn(h, w_router, b_router, w_gate, b_gate, w_up, b_up, w_down, b_down):
    bsz, length, dm = h.shape
    t = h.reshape(-1, dm)
    logits = (t @ w_router + b_router).astype(jnp.float32)
    top_v, top_i = lax.top_k(logits, TOP_K)
    top_w = jax.nn.softmax(top_v, axis=-1)
    comb = jnp.einsum('tk,tke->te', top_w, jax.nn.one_hot(top_i, N_EXPERTS, dtype=jnp.float32))
    out = jnp.zeros(t.shape, jnp.float32)
    for e in range(N_EXPERTS):
        gate = jnp.minimum(t @ w_gate[e] + b_gate[e], SWIGLU_LIMIT)
        up = jnp.clip(t @ w_up[e] + b_up[e], -SWIGLU_LIMIT, SWIGLU_LIMIT)
        act = (up + 1.0) * gate * jax.nn.sigmoid(SWIGLU_ALPHA * gate)
        out = out + comb[:, e:e + 1] * (act @ w_down[e] + b_down[e])
    return out.reshape(bsz, length, dm).astype(h.dtype)


def setup_inputs(seed: int = 0) -> dict:
    key = jax.random.key(seed)
    ks = iter(jax.random.split(key, 40))
    nrm = lambda shape, s: jax.random.normal(next(ks), shape, jnp.float32) * s
    dm, L_ = D_MODEL, DEPTH
    x = nrm((BATCH, SEQ, dm), 1.0)
    meta = nrm((N_META, dm), 1.0)
    ln_in_g = 1.0 + nrm((dm,), 0.02)
    ln_in_b = nrm((dm,), 0.02)
    w_in = nrm((L_, dm, IN_COLS), dm ** -0.5)
    w_in = w_in.at[:, :, V_OFF:F_OFF].multiply(DEEPNORM_BETA)
    b_f = jax.random.uniform(next(ks), (L_, ATT_HEADS), jnp.float32, 1.0, 4.0)
    w_up_a = nrm((L_, ATT_WIDTH, dm), DEEPNORM_BETA * ATT_WIDTH ** -0.5)
    w_up_b = nrm((L_, SSM_WIDTH, dm), DEEPNORM_BETA * SSM_WIDTH ** -0.5)
    w_o = nrm((L_, dm, dm), DEEPNORM_BETA * dm ** -0.5)
    ssm_a_re = -0.5 + nrm((L_, SSM_GROUPS, SSM_STATE), 0.01)
    ssm_a_im = jnp.pi * jnp.arange(SSM_STATE, dtype=jnp.float32) + nrm((L_, SSM_GROUPS, SSM_STATE), 0.01)
    ssm_log_dt = jax.random.uniform(next(ks), (L_, SSM_GROUPS), jnp.float32,
                                    math.log(0.001), math.log(0.1))
    ssm_b_re = nrm((L_, SSM_GROUPS, SSM_STATE, SSM_GROUP), (2 * SSM_GROUP) ** -0.5)
    ssm_b_im = nrm((L_, SSM_GROUPS, SSM_STATE, SSM_GROUP), (2 * SSM_GROUP) ** -0.5)
    ssm_c_re = nrm((L_, SSM_GROUPS, SSM_GROUP, SSM_STATE), (2 * SSM_STATE) ** -0.5)
    ssm_c_im = nrm((L_, SSM_GROUPS, SSM_GROUP, SSM_STATE), (2 * SSM_STATE) ** -0.5)
    ssm_d = nrm((L_, SSM_GROUPS, SSM_GROUP), 1.0)
    w_glu = nrm((L_, SSM_WIDTH, SSM_WIDTH), SSM_WIDTH ** -0.5)
    b_glu = nrm((L_, SSM_WIDTH), 0.02)
    ln1_g = 1.0 + nrm((L_, dm), 0.02)
    ln1_b = nrm((L_, dm), 0.02)
    w_router = nrm((L_, dm, N_EXPERTS), dm ** -0.5)
    b_router = nrm((L_, N_EXPERTS), 0.01)
    w_gate = nrm((L_, N_EXPERTS, dm, D_EXPERT), dm ** -0.5)
    b_gate = nrm((L_, N_EXPERTS, D_EXPERT), 0.02)
    w_up = nrm((L_, N_EXPERTS, dm, D_EXPERT), dm ** -0.5)
    b_up = nrm((L_, N_EXPERTS, D_EXPERT), 0.02)
    w_down = nrm((L_, N_EXPERTS, D_EXPERT, dm), DEEPNORM_BETA * D_EXPERT ** -0.5)
    b_down = nrm((L_, N_EXPERTS, dm), 0.02)
    ln2_g = 1.0 + nrm((L_, dm), 0.02)
    ln2_b = nrm((L_, dm), 0.02)
    return {"x": x, "meta": meta, "ln_in_g": ln_in_g, "ln_in_b": ln_in_b,
            "w_in": w_in, "b_f": b_f, "w_up_a": w_up_a, "w_up_b": w_up_b, "w_o": w_o,
            "ssm_a_re": ssm_a_re, "ssm_a_im": ssm_a_im, "ssm_log_dt": ssm_log_dt,
            "ssm_b_re": ssm_b_re, "ssm_b_im": ssm_b_im, "ssm_c_re": ssm_c_re, "ssm_c_im": ssm_c_im,
            "ssm_d": ssm_d, "w_glu": w_glu, "b_glu": b_glu, "ln1_g": ln1_g, "ln1_b": ln1_b,
            "w_router": w_router, "b_router": b_router, "w_gate": w_gate, "b_gate": b_gate,
            "w_up": w_up, "b_up": b_up, "w_down": w_down, "b_down": b_down,
            "ln2_g": ln2_g, "ln2_b": ln2_b}


def reference(x, meta, ln_in_g, ln_in_b, w_in, b_f, w_up_a, w_up_b, w_o,
              ssm_a_re, ssm_a_im, ssm_log_dt, ssm_b_re, ssm_b_im, ssm_c_re, ssm_c_im,
              ssm_d, w_glu, b_glu, ln1_g, ln1_b, w_router, b_router, w_gate, b_gate,
              w_up, b_up, w_down, b_down, ln2_g, ln2_b):
    bsz = x.shape[0]
    meta_b = jnp.broadcast_to(meta[None].astype(x.dtype), (bsz, N_META, D_MODEL))
    h = jnp.concatenate([meta_b, x], axis=1)
    h = layer_norm(h, ln_in_g, ln_in_b)
    for l in range(DEPTH):
        mix = hybrid_mixer(h, w_in[l], b_f[l], w_up_a[l], w_up_b[l], w_o[l],
                           ssm_a_re[l], ssm_a_im[l], ssm_log_dt[l], ssm_b_re[l], ssm_b_im[l],
                           ssm_c_re[l], ssm_c_im[l], ssm_d[l], w_glu[l], b_glu[l])
        h = layer_norm(DEEPNORM_ALPHA * h + mix, ln1_g[l], ln1_b[l])
        ffn = moe_ffn(h, w_router[l], b_router[l], w_gate[l], b_gate[l],
                      w_up[l], b_up[l], w_down[l], b_down[l])
        h = layer_norm(DEEPNORM_ALPHA * h + ffn, ln2_g[l], ln2_b[l])
    return h[:, N_META:]
```

```python
import functools
import math

import jax
import jax.numpy as jnp
from jax import lax
from jax.experimental import pallas as pl
from jax.experimental.pallas import tpu as pltpu

F32 = jnp.float32
BF16 = jnp.bfloat16

N_META = 16
ATT_HEADS = 8
HEAD_DIM = 64
ATT_WIDTH = ATT_HEADS * HEAD_DIM
SSM_GROUP = 16
SSM_STATE = 64
TOP_K = 4
SWIGLU_LIMIT = 7.0
SWIGLU_ALPHA = 1.702
LN_EPS = 1e-5

LANES = 128
SSM_CHUNK = 64
META_KEYS_PAD = LANES
BIG = 1e30


def _layer_norm(x, g, b):
    mu = jnp.mean(x, axis=-1, keepdims=True)
    xc = x - mu
    var = jnp.mean(xc * xc, axis=-1, keepdims=True)
    return xc * lax.rsqrt(var + LN_EPS) * g + b


def _log_sigmoid(z):
    return jnp.minimum(z, 0.0) - jnp.log1p(jnp.exp(-jnp.abs(z)))


def _sigmoid(z):
    return 1.0 / (1.0 + jnp.exp(-z))


def _split3(x):
    hi = x.astype(BF16)
    r1 = x - hi.astype(F32)
    mid = r1.astype(BF16)
    lo = (r1 - mid.astype(F32)).astype(BF16)
    return hi, mid, lo


def _inproj_kernel(x_ref, g_ref, b_ref, w_ref, wf_ref, bf_ref,
                   q_ref, k_ref, v_ref, u_ref, ga_ref, gb_ref, fc_ref, carry_ref,
                   *, tiles_per_seq):
    i = pl.program_id(0)
    tm = x_ref.shape[0]
    h = _layer_norm(x_ref[...], g_ref[...], b_ref[...]).astype(BF16)

    def proj(lo, hi):
        return jnp.dot(h, w_ref[:, lo:hi], preferred_element_type=F32)

    aw = ATT_WIDTH
    q_ref[...] = (proj(0, aw) * (1.0 / math.sqrt(HEAD_DIM))).astype(BF16)
    k_ref[...] = proj(aw, 2 * aw).astype(BF16)
    v_ref[...] = proj(2 * aw, 3 * aw).astype(BF16)
    u_ref[...] = proj(3 * aw, 4 * aw).astype(BF16)
    dm = ga_ref.shape[1]
    ga_ref[...] = _sigmoid(proj(4 * aw, 4 * aw + dm)).astype(BF16)
    gb_ref[...] = _sigmoid(proj(4 * aw + dm, 4 * aw + 2 * dm)).astype(BF16)

    zf = jnp.dot(h, wf_ref[...], preferred_element_type=F32) + bf_ref[...]
    lf = _log_sigmoid(zf)

    @pl.when(i % tiles_per_seq == 0)
    def _():
        carry_ref[...] = jnp.zeros_like(carry_ref)

    row = lax.broadcasted_iota(jnp.int32, (tm, tm), 0)
    col = lax.broadcasted_iota(jnp.int32, (tm, tm), 1)
    tri = (col <= row).astype(BF16)
    hi, mid, lo = _split3(lf)
    cs = (jnp.dot(tri, hi, preferred_element_type=F32)
          + jnp.dot(tri, mid, preferred_element_type=F32)
          + jnp.dot(tri, lo, preferred_element_type=F32)) + carry_ref[...]
    fc_ref[...] = cs
    carry_ref[...] = cs[tm - 1:tm, :]


def _inproj(x2, ln_g, ln_b, w_main, w_f, b_f, *, tm, tiles_per_seq):
    t, dm = x2.shape
    aw = ATT_WIDTH
    n_main = w_main.shape[1]
    const = lambda i: (0, 0)
    rows = lambda i: (i, 0)
    out_shape = [jax.ShapeDtypeStruct((t, aw), BF16)] * 4 + \
                [jax.ShapeDtypeStruct((t, dm), BF16)] * 2 + \
                [jax.ShapeDtypeStruct((t, LANES), F32)]
    out_specs = [pl.BlockSpec((tm, aw), rows)] * 4 + [pl.BlockSpec((tm, dm), rows)] * 2 + \
                [pl.BlockSpec((tm, LANES), rows)]
    return pl.pallas_call(
        functools.partial(_inproj_kernel, tiles_per_seq=tiles_per_seq),
        grid=(t // tm,),
        in_specs=[pl.BlockSpec((tm, dm), rows),
                  pl.BlockSpec((1, dm), const), pl.BlockSpec((1, dm), const),
                  pl.BlockSpec((dm, n_main), const),
                  pl.BlockSpec((dm, LANES), const), pl.BlockSpec((1, LANES), const)],
        out_specs=out_specs,
        out_shape=out_shape,
        scratch_shapes=[pltpu.VMEM((1, LANES), F32)],
        compiler_params=pltpu.CompilerParams(
            dimension_semantics=("arbitrary",), vmem_limit_bytes=56 * 1024 * 1024),
        name="inproj",
    )(x2, ln_g, ln_b, w_main, w_f, b_f)


def _attn_kernel(qi_tab, ki_tab, q_ref, k_ref, v_ref, fq_ref, fk_ref, km_ref, vm_ref, fkm_ref,
                 o_ref, m_ref, l_ref, acc_ref):
    p = pl.program_id(1)
    qi = qi_tab[p]
    ki = ki_tab[p]
    tq = q_ref.shape[0]
    lane = lax.broadcasted_iota(jnp.int32, (1, LANES), 1)
    first_head = lane < HEAD_DIM

    def process(k_blk, v_blk, fk_rows, mask):
        for hp in range(ATT_HEADS // 2):
            sl = slice(LANES * hp, LANES * (hp + 1))
            q2 = q_ref[:, sl]
            k2 = k_blk[:, sl]
            v2 = v_blk[:, sl]
            alphas = []
            pv = None
            for half in range(2):
                h = 2 * hp + half
                sel = first_head if half == 0 else jnp.logical_not(first_head)
                qm = jnp.where(sel, q2, jnp.zeros_like(q2))
                s = lax.dot_general(qm, k2, (((1,), (1,)), ((), ())), preferred_element_type=F32)
                s = s + fq_ref[:, h:h + 1] - fk_rows[h:h + 1, :]
                if mask is not None:
                    s = jnp.where(mask, s, -jnp.inf)
                m_old = m_ref[h]
                m_new = jnp.maximum(m_old, jnp.max(s, axis=-1, keepdims=True))
                alpha = jnp.exp(m_old - m_new)
                pm = jnp.exp(s - m_new)
                l_ref[h] = alpha * l_ref[h] + jnp.sum(pm, axis=-1, keepdims=True)
                m_ref[h] = m_new
                vmask = jnp.where(sel, v2, jnp.zeros_like(v2))
                d = jnp.dot(pm.astype(BF16), vmask, preferred_element_type=F32)
                pv = d if pv is None else pv + d
                alphas.append(alpha)
            alpha2 = jnp.where(first_head, alphas[0], alphas[1])
            acc_ref[:, sl] = alpha2 * acc_ref[:, sl] + pv

    @pl.when(ki == 0)
    def _():
        m_ref[...] = jnp.full_like(m_ref, -jnp.inf)
        l_ref[...] = jnp.zeros_like(l_ref)
        acc_ref[...] = jnp.zeros_like(acc_ref)
        process(km_ref[...], vm_ref[...], fkm_ref[...], None)

    @pl.when(ki < qi)
    def _():
        process(k_ref[...], v_ref[...], fk_ref[0], None)

    @pl.when(ki == qi)
    def _():
        tk = k_ref.shape[0]
        row = lax.broadcasted_iota(jnp.int32, (tq, tk), 0)
        col = lax.broadcasted_iota(jnp.int32, (tq, tk), 1)
        process(k_ref[...], v_ref[...], fk_ref[0], col <= row)
        for hp in range(ATT_HEADS // 2):
            sl = slice(LANES * hp, LANES * (hp + 1))
            inv = jnp.where(first_head, 1.0 / l_ref[2 * hp], 1.0 / l_ref[2 * hp + 1])
            o_ref[:, sl] = (acc_ref[:, sl] * inv).astype(o_ref.dtype)


def _attn(q, k, v, fq, fk_rows, k_meta, v_meta, fk_meta, *, bsz, seq, tq):
    nq = seq // tq
    pairs = [(a, b) for a in range(nq) for b in range(a + 1)]
    qi_tab = jnp.asarray([a for a, _ in pairs], jnp.int32)
    ki_tab = jnp.asarray([b for _, b in pairs], jnp.int32)
    w = q.shape[1]
    qmap = lambda b, p, qt, kt: (b * nq + qt[p], 0)
    kmap = lambda b, p, qt, kt: (b * nq + kt[p], 0)
    const = lambda b, p, qt, kt: (0, 0)
    grid_spec = pltpu.PrefetchScalarGridSpec(
        num_scalar_prefetch=2,
        grid=(bsz, len(pairs)),
        in_specs=[pl.BlockSpec((tq, w), qmap),
                  pl.BlockSpec((tq, w), kmap),
                  pl.BlockSpec((tq, w), kmap),
                  pl.BlockSpec((tq, LANES), qmap),
                  pl.BlockSpec((1, ATT_HEADS, tq), lambda b, p, qt, kt: (b, 0, kt[p])),
                  pl.BlockSpec((META_KEYS_PAD, w), const),
                  pl.BlockSpec((META_KEYS_PAD, w), const),
                  pl.BlockSpec((ATT_HEADS, META_KEYS_PAD), const)],
        out_specs=pl.BlockSpec((tq, w), qmap),
        scratch_shapes=[pltpu.VMEM((ATT_HEADS, tq, 1), F32),
                        pltpu.VMEM((ATT_HEADS, tq, 1), F32),
                        pltpu.VMEM((tq, w), F32)],
    )
    return pl.pallas_call(
        _attn_kernel,
        grid_spec=grid_spec,
        out_shape=jax.ShapeDtypeStruct((bsz * seq, w), BF16),
        compiler_params=pltpu.CompilerParams(
            dimension_semantics=("arbitrary", "arbitrary"), vmem_limit_bytes=48 * 1024 * 1024),
        name="fox_attention",
    )(qi_tab, ki_tab, q, k, v, fq, fk_rows, k_meta, v_meta, fk_meta)


def _ssm_kernel(u_ref, toep_ref, wst_ref, wout_ref, um_ref, wm_ref, apow_ref, d_ref, y_ref,
                *, chunks_per_seq, n_steps):
    u = u_ref[0]
    rows = u.shape[0]
    p2 = 2 * SSM_STATE
    y = jnp.dot(u, toep_ref[0], preferred_element_type=F32)
    s = jnp.dot(u, wst_ref[0], preferred_element_type=F32)
    x0 = jnp.dot(um_ref[0], wm_ref[0], preferred_element_type=F32)[0:1, :]

    lane = lax.broadcasted_iota(jnp.int32, (1, p2), 1)
    re_half = lane < SSM_STATE
    j = lax.broadcasted_iota(jnp.int32, (rows, 1), 0) % chunks_per_seq

    def cmul(step, z):
        ar = apow_ref[0, step:step + 1, :]
        ai = apow_ref[0, n_steps + step:n_steps + step + 1, :]
        return ar * z + ai * pltpu.roll(z, SSM_STATE, axis=1)

    s = s + jnp.where(j == 0, cmul(0, jnp.broadcast_to(x0, s.shape)), 0.0)
    for step in range(n_steps):
        sh = 1 << step
        prev = pltpu.roll(s, sh, axis=0)
        s = s + jnp.where(j >= sh, cmul(step, prev), 0.0)
    x_in = jnp.where(j == 0, x0, pltpu.roll(s, 1, axis=0))
    del re_half
    y = y + jnp.dot(x_in.astype(BF16), wout_ref[0], preferred_element_type=F32)
    y = y + d_ref[0] * u.astype(F32)
    y_ref[0] = jax.nn.gelu(y).astype(y_ref.dtype)


def _ssm(u_g, toep, wst, wout, u_meta, wmeta, apow, d_tiled, *, chunks_per_seq):
    g, rows, tc = u_g.shape
    n_steps = apow.shape[1] // 2
    grp = lambda i: (i, 0, 0)
    return pl.pallas_call(
        functools.partial(_ssm_kernel, chunks_per_seq=chunks_per_seq, n_steps=n_steps),
        grid=(g,),
        in_specs=[pl.BlockSpec((1, rows, tc), grp),
                  pl.BlockSpec((1, tc, tc), grp),
                  pl.BlockSpec((1, tc, 2 * SSM_STATE), grp),
                  pl.BlockSpec((1, 2 * SSM_STATE, tc), grp),
                  pl.BlockSpec((1,) + u_meta.shape[1:], grp),
                  pl.BlockSpec((1,) + wmeta.shape[1:], grp),
                  pl.BlockSpec((1,) + apow.shape[1:], grp),
                  pl.BlockSpec((1, 1, tc), grp)],
        out_specs=pl.BlockSpec((1, rows, tc), grp),
        out_shape=jax.ShapeDtypeStruct((g, rows, tc), BF16),
        compiler_params=pltpu.CompilerParams(
            dimension_semantics=("arbitrary",), vmem_limit_bytes=48 * 1024 * 1024),
        name="s5_ssm",
    )(u_g, toep, wst, wout, u_meta, wmeta, apow, d_tiled)


def _ssm_tables(a_re, a_im, log_dt, b_re, b_im, c_re, c_im, d_skip, *, chunk, n_steps):
    g, p = a_re.shape
    c = b_re.shape[-1]
    dt = jnp.exp(log_dt)[:, None]
    mag = jnp.exp(a_re * dt)
    ang = a_im * dt
    lb_re, lb_im = mag * jnp.cos(ang), mag * jnp.sin(ang)
    den = a_re * a_re + a_im * a_im
    z_re, z_im = lb_re - 1.0, lb_im
    coef_re = (z_re * a_re + z_im * a_im) / den
    coef_im = (z_im * a_re - z_re * a_im) / den
    bb_re = coef_re[..., None] * b_re - coef_im[..., None] * b_im
    bb_im = coef_re[..., None] * b_im + coef_im[..., None] * b_re

    def step(carry, _):
        cr, ci = carry
        return (cr * lb_re - ci * lb_im, cr * lb_im + ci * lb_re), (cr, ci)
    _, (pw_re, pw_im) = lax.scan(step, (jnp.ones_like(lb_re), jnp.zeros_like(lb_re)), None,
                                 length=chunk + 1)

    e_re = c_re[None] * pw_re[:, :, None, :] - c_im[None] * pw_im[:, :, None, :]
    e_im = c_re[None] * pw_im[:, :, None, :] + c_im[None] * pw_re[:, :, None, :]
    kern = (jnp.einsum('tgcp,gpd->tgcd', e_re[:chunk], bb_re)
            - jnp.einsum('tgcp,gpd->tgcd', e_im[:chunk], bb_im))
    kt = jnp.transpose(kern, (1, 3, 0, 2))
    kt = jnp.concatenate([jnp.zeros_like(kt), kt], axis=2)
    rows = [kt[:, :, chunk - tp:2 * chunk - tp, :] for tp in range(chunk)]
    toep = jnp.stack(rows, axis=1).reshape(g, chunk * c, chunk * c)

    def in_to_state(n):
        wr = pw_re[n - 1::-1][:n, :, :, None] * bb_re[None] - pw_im[n - 1::-1][:n, :, :, None] * bb_im[None]
        wi = pw_re[n - 1::-1][:n, :, :, None] * bb_im[None] + pw_im[n - 1::-1][:n, :, :, None] * bb_re[None]
        w = jnp.concatenate([wr, wi], axis=2)
        return jnp.transpose(w, (1, 0, 3, 2)).reshape(g, n * c, 2 * p)
    wst = in_to_state(chunk)
    wmeta = in_to_state(N_META)

    wout = jnp.concatenate([e_re[1:chunk + 1], -e_im[1:chunk + 1]], axis=3)
    wout = jnp.transpose(wout, (1, 3, 0, 2)).reshape(g, 2 * p, chunk * c)

    ar, ai = pw_re[chunk], pw_im[chunk]
    rows_r, rows_i = [], []
    for _ in range(n_steps):
        rows_r.append(jnp.concatenate([ar, ar], axis=-1))
        rows_i.append(jnp.concatenate([-ai, ai], axis=-1))
        ar, ai = ar * ar - ai * ai, 2.0 * ar * ai
    apow = jnp.stack(rows_r + rows_i, axis=1)
    d_tiled = jnp.tile(d_skip, (1, chunk))[:, None, :]
    return toep.astype(BF16), wst.astype(BF16), wout.astype(BF16), wmeta.astype(BF16), apow, d_tiled


def _postmix_kernel(x_ref, ya_ref, yb_ref, ga_ref, gb_ref, lng_ref, lnb_ref,
                    wglu_ref, bglu_ref, wa_ref, wb_ref, wo_ref, l1g_ref, l1b_ref,
                    wrh_ref, wrl_ref, br_ref, h1_ref, h1b_ref, comb_ref, *, alpha, n_experts):
    h0 = _layer_norm(x_ref[...], lng_ref[...], lnb_ref[...])
    yb = yb_ref[...]
    glu = yb.astype(F32) * _sigmoid(jnp.dot(yb, wglu_ref[...], preferred_element_type=F32) + bglu_ref[...])
    merged = (ga_ref[...].astype(F32) * jnp.dot(ya_ref[...], wa_ref[...], preferred_element_type=F32)
              + gb_ref[...].astype(F32) * jnp.dot(glu.astype(BF16), wb_ref[...], preferred_element_type=F32))
    mix = jnp.dot(merged.astype(BF16), wo_ref[...], preferred_element_type=F32)
    h1 = _layer_norm(alpha * h0 + mix, l1g_ref[...], l1b_ref[...])
    h1_ref[...] = h1
    hb = h1.astype(BF16)
    h1b_ref[...] = hb

    hl = (h1 - hb.astype(F32)).astype(BF16)
    logits = (jnp.dot(hb, wrh_ref[...], preferred_element_type=F32)
              + jnp.dot(hb, wrl_ref[...], preferred_element_type=F32)
              + jnp.dot(hl, wrh_ref[...], preferred_element_type=F32)) + br_ref[...]
    lane = lax.broadcasted_iota(jnp.int32, logits.shape, 1)
    logits = jnp.where(lane < n_experts, logits, -jnp.inf)
    comb = jnp.zeros_like(logits)
    denom = jnp.zeros((logits.shape[0], 1), F32)
    top = None
    for _ in range(TOP_K):
        mx = jnp.max(logits, axis=-1, keepdims=True)
        idx = jnp.min(jnp.where(logits == mx, lane, LANES), axis=-1, keepdims=True)
        hit = lane == idx
        if top is None:
            top = mx
        w = jnp.exp(mx - top)
        denom = denom + w
        comb = comb + jnp.where(hit, w, 0.0)
        logits = jnp.where(hit, -jnp.inf, logits)
    comb_ref[...] = comb / denom


def _postmix(x2, ya, yb, ga, gb, ln_g, ln_b, w_glu, b_glu, w_a, w_b, w_o, l1g, l1b,
             wr_hi, wr_lo, b_r, *, tm, alpha, n_experts):
    t, dm = x2.shape
    sw = ya.shape[1]
    rows = lambda i: (i, 0)
    const = lambda i: (0, 0)
    full = lambda a: pl.BlockSpec(a.shape, const)
    return pl.pallas_call(
        functools.partial(_postmix_kernel, alpha=alpha, n_experts=n_experts),
        grid=(t // tm,),
        in_specs=[pl.BlockSpec((tm, dm), rows), pl.BlockSpec((tm, sw), rows), pl.BlockSpec((tm, sw), rows),
                  pl.BlockSpec((tm, dm), rows), pl.BlockSpec((tm, dm), rows),
                  full(ln_g), full(ln_b), full(w_glu), full(b_glu), full(w_a), full(w_b), full(w_o),
                  full(l1g), full(l1b), full(wr_hi), full(wr_lo), full(b_r)],
        out_specs=[pl.BlockSpec((tm, dm), rows), pl.BlockSpec((tm, dm), rows),
                   pl.BlockSpec((tm, LANES), rows)],
        out_shape=[jax.ShapeDtypeStruct((t, dm), F32), jax.ShapeDtypeStruct((t, dm), BF16),
                   jax.ShapeDtypeStruct((t, LANES), F32)],
        compiler_params=pltpu.CompilerParams(
            dimension_semantics=("arbitrary",), vmem_limit_bytes=48 * 1024 * 1024),
        name="postmix_router",
    )(x2, ya, yb, ga, gb, ln_g, ln_b, w_glu, b_glu, w_a, w_b, w_o, l1g, l1b, wr_hi, wr_lo, b_r)


def _moe_kernel(hb_ref, h1_ref, comb_ref, wg_ref, bg_ref, wu_ref, bu_ref, wd_ref, bd_ref,
                l2g_ref, l2b_ref, o_ref, acc_ref, *, alpha):
    e = pl.program_id(1)

    @pl.when(e == 0)
    def _():
        acc_ref[...] = jnp.zeros_like(acc_ref)

    x = hb_ref[...]
    gate = jnp.minimum(jnp.dot(x, wg_ref[0], preferred_element_type=F32) + bg_ref[0], SWIGLU_LIMIT)
    up = jnp.clip(jnp.dot(x, wu_ref[0], preferred_element_type=F32) + bu_ref[0], -SWIGLU_LIMIT, SWIGLU_LIMIT)
    act = (up + 1.0) * gate * _sigmoid(SWIGLU_ALPHA * gate)
    y = jnp.dot(act.astype(BF16), wd_ref[0], preferred_element_type=F32) + bd_ref[0]
    lane = lax.broadcasted_iota(jnp.int32, comb_ref.shape, 1)
    w_e = jnp.sum(jnp.where(lane == e, comb_ref[...], 0.0), axis=-1, keepdims=True)
    acc_ref[...] += w_e * y

    @pl.when(e == pl.num_programs(1) - 1)
    def _():
        o_ref[...] = _layer_norm(alpha * h1_ref[...] + acc_ref[...], l2g_ref[...], l2b_ref[...])


def _moe(h1b, h1, comb, w_gate, b_gate, w_up, b_up, w_down, b_down, l2g, l2b, *, tm, alpha):
    t, dm = h1.shape
    n_e, _, de = w_gate.shape
    rows = lambda i, e: (i, 0)
    const = lambda i, e: (0, 0)
    per_e = lambda i, e: (e, 0, 0)
    return pl.pallas_call(
        functools.partial(_moe_kernel, alpha=alpha),
        grid=(t // tm, n_e),
        in_specs=[pl.BlockSpec((tm, dm), rows), pl.BlockSpec((tm, dm), rows), pl.BlockSpec((tm, LANES), rows),
                  pl.BlockSpec((1, dm, de), per_e), pl.BlockSpec((1, 1, de), per_e),
                  pl.BlockSpec((1, dm, de), per_e), pl.BlockSpec((1, 1, de), per_e),
                  pl.BlockSpec((1, de, dm), per_e), pl.BlockSpec((1, 1, dm), per_e),
                  pl.BlockSpec((1, dm), const), pl.BlockSpec((1, dm), const)],
        out_specs=pl.BlockSpec((tm, dm), rows),
        out_shape=jax.ShapeDtypeStruct((t, dm), F32),
        scratch_shapes=[pltpu.VMEM((tm, dm), F32)],
        compiler_params=pltpu.CompilerParams(
            dimension_semantics=("arbitrary", "arbitrary"), vmem_limit_bytes=56 * 1024 * 1024),
        name="moe_experts",
    )(h1b, h1, comb, w_gate, b_gate, w_up, b_up, w_down, b_down, l2g, l2b)


def _pick_tile(n, want):
    t = min(n, want)
    assert n % t == 0, (n, t)
    return t


def kernel(x, meta, ln_in_g, ln_in_b, w_in, b_f, w_up_a, w_up_b, w_o, ssm_a_re, ssm_a_im, ssm_log_dt,
           ssm_b_re, ssm_b_im, ssm_c_re, ssm_c_im, ssm_d, w_glu, b_glu, ln1_g, ln1_b, w_router, b_router,
           w_gate, b_gate, w_up, b_up, w_down, b_down, ln2_g, ln2_b):
    bsz, seq, dm = x.shape
    depth = w_in.shape[0]
    assert depth == 1 and meta.shape[0] == N_META
    alpha = (2.0 * depth) ** 0.25
    n_groups = ssm_a_re.shape[1]
    n_experts = w_router.shape[-1]
    aw = ATT_WIDTH
    t = bsz * seq
    x2 = x.reshape(t, dm)
    row = lambda a: a.reshape(1, -1)

    f_off = 3 * aw
    u_off = f_off + ATT_HEADS
    w0 = w_in[0]
    w_main = jnp.concatenate([w0[:, :f_off], w0[:, u_off:]], axis=1).astype(BF16)
    w_f = jnp.pad(w0[:, f_off:u_off], ((0, 0), (0, LANES - ATT_HEADS))).astype(BF16)
    b_f_pad = jnp.pad(b_f[0], (0, LANES - ATT_HEADS)).reshape(1, LANES)
    ln_g, ln_b = row(ln_in_g), row(ln_in_b)

    tm = _pick_tile(seq, 512)
    q, k, v, u, ga, gb, fcum = _inproj(x2, ln_g, ln_b, w_main, w_f, b_f_pad, tm=tm, tiles_per_seq=seq // tm)
    _, k_m, v_m, u_m, _, _, fcum_m = _inproj(meta, ln_g, ln_b, w_main, w_f, b_f_pad, tm=N_META, tiles_per_seq=1)

    pad_m = META_KEYS_PAD - N_META
    k_meta = jnp.pad(k_m, ((0, pad_m), (0, 0)))
    v_meta = jnp.pad(v_m, ((0, pad_m), (0, 0)))
    fm = fcum_m[:, :ATT_HEADS]
    fk_meta = jnp.pad((fm - fm[N_META - 1:N_META, :]).T, ((0, 0), (0, pad_m)), constant_values=BIG)
    fk_rows = jnp.transpose(fcum[:, :ATT_HEADS].reshape(bsz, seq, ATT_HEADS), (0, 2, 1))
    tq = _pick_tile(seq, 512)
    y_a = _attn(q, k, v, fcum, fk_rows, k_meta, v_meta, fk_meta, bsz=bsz, seq=seq, tq=tq)

    chunk = SSM_CHUNK
    n_chunks = seq // chunk
    n_steps = max(1, (n_chunks - 1).bit_length())
    toep, wst, wout, wmeta, apow, d_tiled = _ssm_tables(
        ssm_a_re[0], ssm_a_im[0], ssm_log_dt[0], ssm_b_re[0], ssm_b_im[0], ssm_c_re[0], ssm_c_im[0], ssm_d[0],
        chunk=chunk, n_steps=n_steps)
    u_g = jnp.transpose(u.reshape(bsz * n_chunks, chunk, n_groups, SSM_GROUP), (2, 0, 1, 3))
    u_g = u_g.reshape(n_groups, bsz * n_chunks, chunk * SSM_GROUP)
    um_g = jnp.transpose(u_m.reshape(N_META, n_groups, SSM_GROUP), (1, 0, 2)).reshape(n_groups, 1, N_META * SSM_GROUP)
    um_g = jnp.pad(um_g, ((0, 0), (0, 15), (0, 0)))
    y_g = _ssm(u_g, toep, wst, wout, um_g, wmeta, apow, d_tiled, chunks_per_seq=n_chunks)
    y_b = jnp.transpose(y_g.reshape(n_groups, bsz * n_chunks, chunk, SSM_GROUP), (1, 2, 0, 3)).reshape(t, -1)

    wr = jnp.pad(w_router[0], ((0, 0), (0, LANES - n_experts)))
    wr_hi = wr.astype(BF16)
    wr_lo = (wr - wr_hi.astype(F32)).astype(BF16)
    b_r = jnp.pad(b_router[0], (0, LANES - n_experts)).reshape(1, LANES)
    tm2 = _pick_tile(t, 512)
    h1, h1b, comb = _postmix(
        x2, y_a, y_b, ga, gb, ln_g, ln_b, w_glu[0].astype(BF16), row(b_glu[0]),
        w_up_a[0].astype(BF16), w_up_b[0].astype(BF16), w_o[0].astype(BF16), row(ln1_g[0]), row(ln1_b[0]),
        wr_hi, wr_lo, b_r, tm=tm2, alpha=alpha, n_experts=n_experts)

    tm3 = _pick_tile(t, 1024)
    out = _moe(h1b, h1, comb, w_gate[0].astype(BF16), b_gate[0][:, None, :], w_up[0].astype(BF16),
               b_up[0][:, None, :], w_down[0].astype(BF16), b_down[0][:, None, :],
               row(ln2_g[0]), row(ln2_b[0]), tm=tm3, alpha=alpha)
    return out.reshape(bsz, seq, dm)
```

```python
import functools
import math

import jax
import jax.numpy as jnp
from jax import lax
from jax.experimental import pallas as pl
from jax.experimental.pallas import tpu as pltpu

F32 = jnp.float32
BF16 = jnp.bfloat16

N_META = 16
ATT_HEADS = 8
HEAD_DIM = 64
ATT_WIDTH = ATT_HEADS * HEAD_DIM
SSM_GROUP = 16
SSM_STATE = 64
TOP_K = 4
SWIGLU_LIMIT = 7.0
SWIGLU_ALPHA = 1.702
LN_EPS = 1e-5

LANES = 128
SSM_CHUNK = 64
META_KEYS_PAD = LANES
BIG = 1e30


def _layer_norm(x, g, b):
    mu = jnp.mean(x, axis=-1, keepdims=True)
    xc = x - mu
    var = jnp.mean(xc * xc, axis=-1, keepdims=True)
    return xc * lax.rsqrt(var + LN_EPS) * g + b


def _log_sigmoid(z):
    return jnp.minimum(z, 0.0) - jnp.log1p(jnp.exp(-jnp.abs(z)))


def _sigmoid(z):
    return 1.0 / (1.0 + jnp.exp(-z))


def _split3(x):
    hi = x.astype(BF16)
    r1 = x - hi.astype(F32)
    mid = r1.astype(BF16)
    lo = (r1 - mid.astype(F32)).astype(BF16)
    return hi, mid, lo


def _inproj_kernel(x_ref, g_ref, b_ref, w_ref, wf_ref, bf_ref,
                   q_ref, k_ref, v_ref, u_ref, ga_ref, gb_ref, fc_ref, carry_ref,
                   *, tiles_per_seq):
    i = pl.program_id(0)
    tm = x_ref.shape[0]
    h = _layer_norm(x_ref[...], g_ref[...], b_ref[...]).astype(BF16)

    def proj(lo, hi):
        return jnp.dot(h, w_ref[:, lo:hi], preferred_element_type=F32)

    aw = ATT_WIDTH
    q_ref[...] = (proj(0, aw) * (1.0 / math.sqrt(HEAD_DIM))).astype(BF16)
    k_ref[...] = proj(aw, 2 * aw).astype(BF16)
    v_ref[...] = proj(2 * aw, 3 * aw).astype(BF16)
    u_ref[...] = proj(3 * aw, 4 * aw).astype(BF16)
    dm = ga_ref.shape[1]
    ga_ref[...] = _sigmoid(proj(4 * aw, 4 * aw + dm)).astype(BF16)
    gb_ref[...] = _sigmoid(proj(4 * aw + dm, 4 * aw + 2 * dm)).astype(BF16)

    zf = jnp.dot(h, wf_ref[...], preferred_element_type=F32) + bf_ref[...]
    lf = _log_sigmoid(zf)

    @pl.when(i % tiles_per_seq == 0)
    def _():
        carry_ref[...] = jnp.zeros_like(carry_ref)

    row = lax.broadcasted_iota(jnp.int32, (tm, tm), 0)
    col = lax.broadcasted_iota(jnp.int32, (tm, tm), 1)
    tri = (col <= row).astype(BF16)
    hi, mid, lo = _split3(lf)
    cs = (jnp.dot(tri, hi, preferred_element_type=F32)
          + jnp.dot(tri, mid, preferred_element_type=F32)
          + jnp.dot(tri, lo, preferred_element_type=F32)) + carry_ref[...]
    fc_ref[...] = cs
    carry_ref[...] = cs[tm - 1:tm, :]


def _inproj(x2, ln_g, ln_b, w_main, w_f, b_f, *, tm, tiles_per_seq):
    t, dm = x2.shape
    aw = ATT_WIDTH
    n_main = w_main.shape[1]
    const = lambda i: (0, 0)
    rows = lambda i: (i, 0)
    out_shape = [jax.ShapeDtypeStruct((t, aw), BF16)] * 4 + \
                [jax.ShapeDtypeStruct((t, dm), BF16)] * 2 + \
                [jax.ShapeDtypeStruct((t, LANES), F32)]
    out_specs = [pl.BlockSpec((tm, aw), rows)] * 4 + [pl.BlockSpec((tm, dm), rows)] * 2 + \
                [pl.BlockSpec((tm, LANES), rows)]
    return pl.pallas_call(
        functools.partial(_inproj_kernel, tiles_per_seq=tiles_per_seq),
        grid=(t // tm,),
        in_specs=[pl.BlockSpec((tm, dm), rows),
                  pl.BlockSpec((1, dm), const), pl.BlockSpec((1, dm), const),
                  pl.BlockSpec((dm, n_main), const),
                  pl.BlockSpec((dm, LANES), const), pl.BlockSpec((1, LANES), const)],
        out_specs=out_specs,
        out_shape=out_shape,
        scratch_shapes=[pltpu.VMEM((1, LANES), F32)],
        compiler_params=pltpu.CompilerParams(
            dimension_semantics=("arbitrary",), vmem_limit_bytes=56 * 1024 * 1024),
        name="inproj",
    )(x2, ln_g, ln_b, w_main, w_f, b_f)


def _attn_kernel(qi_tab, ki_tab, q_ref, k_ref, v_ref, fq_ref, fk_ref, km_ref, vm_ref, fkm_ref,
                 o_ref, m_ref, l_ref, acc_ref):
    p = pl.program_id(1)
    qi = qi_tab[p]
    ki = ki_tab[p]
    tq = q_ref.shape[0]
    lane = lax.broadcasted_iota(jnp.int32, (1, LANES), 1)
    first_head = lane < HEAD_DIM

    def process(k_blk, v_blk, fk_rows, mask):
        for hp in range(ATT_HEADS // 2):
            sl = slice(LANES * hp, LANES * (hp + 1))
            q2 = q_ref[:, sl]
            k2 = k_blk[:, sl]
            v2 = v_blk[:, sl]
            alphas = []
            pv = None
            for half in range(2):
                h = 2 * hp + half
                sel = first_head if half == 0 else jnp.logical_not(first_head)
                qm = jnp.where(sel, q2, jnp.zeros_like(q2))
                s = lax.dot_general(qm, k2, (((1,), (1,)), ((), ())), preferred_element_type=F32)
                s = s + fq_ref[:, h:h + 1] - fk_rows[h:h + 1, :]
                if mask is not None:
                    s = jnp.where(mask, s, -jnp.inf)
                m_old = m_ref[h]
                m_new = jnp.maximum(m_old, jnp.max(s, axis=-1, keepdims=True))
                alpha = jnp.exp(m_old - m_new)
                pm = jnp.exp(s - m_new)
                l_ref[h] = alpha * l_ref[h] + jnp.sum(pm, axis=-1, keepdims=True)
                m_ref[h] = m_new
                vmask = jnp.where(sel, v2, jnp.zeros_like(v2))
                d = jnp.dot(pm.astype(BF16), vmask, preferred_element_type=F32)
                pv = d if pv is None else pv + d
                alphas.append(alpha)
            alpha2 = jnp.where(first_head, alphas[0], alphas[1])
            acc_ref[:, sl] = alpha2 * acc_ref[:, sl] + pv

    @pl.when(ki == 0)
    def _():
        m_ref[...] = jnp.full_like(m_ref, -jnp.inf)
        l_ref[...] = jnp.zeros_like(l_ref)
        acc_ref[...] = jnp.zeros_like(acc_ref)
        process(km_ref[...], vm_ref[...], fkm_ref[...], None)

    @pl.when(ki < qi)
    def _():
        process(k_ref[...], v_ref[...], fk_ref[0], None)

    @pl.when(ki == qi)
    def _():
        tk = k_ref.shape[0]
        row = lax.broadcasted_iota(jnp.int32, (tq, tk), 0)
        col = lax.broadcasted_iota(jnp.int32, (tq, tk), 1)
        process(k_ref[...], v_ref[...], fk_ref[0], col <= row)
        for hp in range(ATT_HEADS // 2):
            sl = slice(LANES * hp, LANES * (hp + 1))
            inv = jnp.where(first_head, 1.0 / l_ref[2 * hp], 1.0 / l_ref[2 * hp + 1])
            o_ref[:, sl] = (acc_ref[:, sl] * inv).astype(o_ref.dtype)


def _attn(q, k, v, fq, fk_rows, k_meta, v_meta, fk_meta, *, bsz, seq, tq):
    nq = seq // tq
    pairs = [(a, b) for a in range(nq) for b in range(a + 1)]
    qi_tab = jnp.asarray([a for a, _ in pairs], jnp.int32)
    ki_tab = jnp.asarray([b for _, b in pairs], jnp.int32)
    w = q.shape[1]
    qmap = lambda b, p, qt, kt: (b * nq + qt[p], 0)
    kmap = lambda b, p, qt, kt: (b * nq + kt[p], 0)
    const = lambda b, p, qt, kt: (0, 0)
    grid_spec = pltpu.PrefetchScalarGridSpec(
        num_scalar_prefetch=2,
        grid=(bsz, len(pairs)),
        in_specs=[pl.BlockSpec((tq, w), qmap),
                  pl.BlockSpec((tq, w), kmap),
                  pl.BlockSpec((tq, w), kmap),
                  pl.BlockSpec((tq, LANES), qmap),
                  pl.BlockSpec((1, ATT_HEADS, tq), lambda b, p, qt, kt: (b, 0, kt[p])),
                  pl.BlockSpec((META_KEYS_PAD, w), const),
                  pl.BlockSpec((META_KEYS_PAD, w), const),
                  pl.BlockSpec((ATT_HEADS, META_KEYS_PAD), const)],
        out_specs=pl.BlockSpec((tq, w), qmap),
        scratch_shapes=[pltpu.VMEM((ATT_HEADS, tq, 1), F32),
                        pltpu.VMEM((ATT_HEADS, tq, 1), F32),
                        pltpu.VMEM((tq, w), F32)],
    )
    return pl.pallas_call(
        _attn_kernel,
        grid_spec=grid_spec,
        out_shape=jax.ShapeDtypeStruct((bsz * seq, w), BF16),
        compiler_params=pltpu.CompilerParams(
            dimension_semantics=("arbitrary", "arbitrary"), vmem_limit_bytes=48 * 1024 * 1024),
        name="fox_attention",
    )(qi_tab, ki_tab, q, k, v, fq, fk_rows, k_meta, v_meta, fk_meta)


def _ssm_kernel(u_ref, toep_ref, wst_ref, wout_ref, um_ref, wm_ref, apow_ref, d_ref, y_ref,
                *, chunks_per_seq, n_steps):
    u = u_ref[0]
    rows = u.shape[0]
    p2 = 2 * SSM_STATE
    y = jnp.dot(u, toep_ref[0], preferred_element_type=F32)
    s = jnp.dot(u, wst_ref[0], preferred_element_type=F32)
    x0 = jnp.dot(um_ref[0], wm_ref[0], preferred_element_type=F32)[0:1, :]

    lane = lax.broadcasted_iota(jnp.int32, (1, p2), 1)
    re_half = lane < SSM_STATE
    j = lax.broadcasted_iota(jnp.int32, (rows, 1), 0) % chunks_per_seq

    def cmul(step, z):
        ar = apow_ref[0, step:step + 1, :]
        ai = apow_ref[0, n_steps + step:n_steps + step + 1, :]
        return ar * z + ai * pltpu.roll(z, SSM_STATE, axis=1)

    s = s + jnp.where(j == 0, cmul(0, jnp.broadcast_to(x0, s.shape)), 0.0)
    for step in range(n_steps):
        sh = 1 << step
        prev = pltpu.roll(s, sh, axis=0)
        s = s + jnp.where(j >= sh, cmul(step, prev), 0.0)
    x_in = jnp.where(j == 0, x0, pltpu.roll(s, 1, axis=0))
    del re_half
    y = y + jnp.dot(x_in.astype(BF16), wout_ref[0], preferred_element_type=F32)
    y = y + d_ref[0] * u.astype(F32)
    y_ref[0] = jax.nn.gelu(y).astype(y_ref.dtype)


def _ssm(u_g, toep, wst, wout, u_meta, wmeta, apow, d_tiled, *, chunks_per_seq):
    g, rows, tc = u_g.shape
    n_steps = apow.shape[1] // 2
    grp = lambda i: (i, 0, 0)
    return pl.pallas_call(
        functools.partial(_ssm_kernel, chunks_per_seq=chunks_per_seq, n_steps=n_steps),
        grid=(g,),
        in_specs=[pl.BlockSpec((1, rows, tc), grp),
                  pl.BlockSpec((1, tc, tc), grp),
                  pl.BlockSpec((1, tc, 2 * SSM_STATE), grp),
                  pl.BlockSpec((1, 2 * SSM_STATE, tc), grp),
                  pl.BlockSpec((1,) + u_meta.shape[1:], grp),
                  pl.BlockSpec((1,) + wmeta.shape[1:], grp),
                  pl.BlockSpec((1,) + apow.shape[1:], grp),
                  pl.BlockSpec((1, 1, tc), grp)],
        out_specs=pl.BlockSpec((1, rows, tc), grp),
        out_shape=jax.ShapeDtypeStruct((g, rows, tc), BF16),
        compiler_params=pltpu.CompilerParams(
            dimension_semantics=("arbitrary",), vmem_limit_bytes=48 * 1024 * 1024),
        name="s5_ssm",
    )(u_g, toep, wst, wout, u_meta, wmeta, apow, d_tiled)


def _ssm_tables(a_re, a_im, log_dt, b_re, b_im, c_re, c_im, d_skip, *, chunk, n_steps):
    g, p = a_re.shape
    c = b_re.shape[-1]
    dt = jnp.exp(log_dt)[:, None]
    mag = jnp.exp(a_re * dt)
    ang = a_im * dt
    lb_re, lb_im = mag * jnp.cos(ang), mag * jnp.sin(ang)
    den = a_re * a_re + a_im * a_im
    z_re, z_im = lb_re - 1.0, lb_im
    coef_re = (z_re * a_re + z_im * a_im) / den
    coef_im = (z_im * a_re - z_re * a_im) / den
    bb_re = coef_re[..., None] * b_re - coef_im[..., None] * b_im
    bb_im = coef_re[..., None] * b_im + coef_im[..., None] * b_re

    def step(carry, _):
        cr, ci = carry
        return (cr * lb_re - ci * lb_im, cr * lb_im + ci * lb_re), (cr, ci)
    _, (pw_re, pw_im) = lax.scan(step, (jnp.ones_like(lb_re), jnp.zeros_like(lb_re)), None,
                                 length=chunk + 1)

    e_re = c_re[None] * pw_re[:, :, None, :] - c_im[None] * pw_im[:, :, None, :]
    e_im = c_re[None] * pw_im[:, :, None, :] + c_im[None] * pw_re[:, :, None, :]
    kern = (jnp.einsum('tgcp,gpd->tgcd', e_re[:chunk], bb_re)
            - jnp.einsum('tgcp,gpd->tgcd', e_im[:chunk], bb_im))
    kt = jnp.transpose(kern, (1, 3, 0, 2))
    kt = jnp.concatenate([jnp.zeros_like(kt), kt], axis=2)
    rows = [kt[:, :, chunk - tp:2 * chunk - tp, :] for tp in range(chunk)]
    toep = jnp.stack(rows, axis=1).reshape(g, chunk * c, chunk * c)

    def in_to_state(n):
        wr = pw_re[n - 1::-1][:n, :, :, None] * bb_re[None] - pw_im[n - 1::-1][:n, :, :, None] * bb_im[None]
        wi = pw_re[n - 1::-1][:n, :, :, None] * bb_im[None] + pw_im[n - 1::-1][:n, :, :, None] * bb_re[None]
        w = jnp.concatenate([wr, wi], axis=2)
        return jnp.transpose(w, (1, 0, 3, 2)).reshape(g, n * c, 2 * p)
    wst = in_to_state(chunk)
    wmeta = in_to_state(N_META)

    wout = jnp.concatenate([e_re[1:chunk + 1], -e_im[1:chunk + 1]], axis=3)
    wout = jnp.transpose(wout, (1, 3, 0, 2)).reshape(g, 2 * p, chunk * c)

    ar, ai = pw_re[chunk], pw_im[chunk]
    rows_r, rows_i = [], []
    for _ in range(n_steps):
        rows_r.append(jnp.concatenate([ar, ar], axis=-1))
        rows_i.append(jnp.concatenate([-ai, ai], axis=-1))
        ar, ai = ar * ar - ai * ai, 2.0 * ar * ai
    apow = jnp.stack(rows_r + rows_i, axis=1)
    d_tiled = jnp.tile(d_skip, (1, chunk))[:, None, :]
    return toep.astype(BF16), wst.astype(BF16), wout.astype(BF16), wmeta.astype(BF16), apow, d_tiled


def _postmix_kernel(x_ref, ya_ref, yb_ref, ga_ref, gb_ref, lng_ref, lnb_ref,
                    wglu_ref, bglu_ref, wa_ref, wb_ref, wo_ref, l1g_ref, l1b_ref,
                    wrh_ref, wrl_ref, br_ref, h1_ref, ridx_ref, rw_ref, cnt_ref, carry_ref,
                    *, alpha, n_experts):
    i = pl.program_id(0)
    tm = x_ref.shape[0]
    h0 = _layer_norm(x_ref[...], lng_ref[...], lnb_ref[...])
    yb = yb_ref[...]
    glu = yb.astype(F32) * _sigmoid(jnp.dot(yb, wglu_ref[...], preferred_element_type=F32) + bglu_ref[...])
    merged = (ga_ref[...].astype(F32) * jnp.dot(ya_ref[...], wa_ref[...], preferred_element_type=F32)
              + gb_ref[...].astype(F32) * jnp.dot(glu.astype(BF16), wb_ref[...], preferred_element_type=F32))
    mix = jnp.dot(merged.astype(BF16), wo_ref[...], preferred_element_type=F32)
    h1 = _layer_norm(alpha * h0 + mix, l1g_ref[...], l1b_ref[...])
    h1_ref[...] = h1.reshape(h1_ref.shape)
    hb = h1.astype(BF16)

    hl = (h1 - hb.astype(F32)).astype(BF16)
    logits = (jnp.dot(hb, wrh_ref[...], preferred_element_type=F32)
              + jnp.dot(hb, wrl_ref[...], preferred_element_type=F32)
              + jnp.dot(hl, wrh_ref[...], preferred_element_type=F32)) + br_ref[...]
    lane = lax.broadcasted_iota(jnp.int32, logits.shape, 1)
    logits = jnp.where(lane < n_experts, logits, -jnp.inf)
    denom = jnp.zeros((tm, 1), F32)
    top = None
    hits, idxs, ws = [], [], []
    for _ in range(TOP_K):
        mx = jnp.max(logits, axis=-1, keepdims=True)
        idx = jnp.min(jnp.where(logits == mx, lane, LANES), axis=-1, keepdims=True)
        hit = lane == idx
        if top is None:
            top = mx
        w = jnp.exp(mx - top)
        denom = denom + w
        hits.append(hit)
        idxs.append(idx)
        ws.append(w)
        logits = jnp.where(hit, -jnp.inf, logits)

    @pl.when(i == 0)
    def _():
        carry_ref[...] = jnp.zeros_like(carry_ref)

    sel = jnp.zeros(logits.shape, F32)
    for hit in hits:
        sel = sel + jnp.where(hit, 1.0, 0.0)
    row = lax.broadcasted_iota(jnp.int32, (tm, tm), 0)
    col = lax.broadcasted_iota(jnp.int32, (tm, tm), 1)
    before = (col < row).astype(BF16)
    seen = jnp.dot(before, sel.astype(BF16), preferred_element_type=F32) + carry_ref[...]
    ridx = jnp.zeros(logits.shape, jnp.int32)
    rw = jnp.zeros(logits.shape, F32)
    for kk in range(TOP_K):
        rank = jnp.sum(jnp.where(hits[kk], seen, 0.0), axis=-1, keepdims=True).astype(jnp.int32)
        ridx = jnp.where(lane == kk, idxs[kk], ridx)
        ridx = jnp.where(lane == TOP_K + kk, rank, ridx)
        rw = jnp.where(lane == kk, ws[kk] / denom, rw)
    ridx_ref[...] = ridx
    rw_ref[...] = rw
    total = carry_ref[...] + jnp.sum(sel, axis=0, keepdims=True)
    carry_ref[...] = total
    cnt_ref[...] = jnp.broadcast_to(total, cnt_ref.shape)


def _postmix(x2, ya, yb, ga, gb, ln_g, ln_b, w_glu, b_glu, w_a, w_b, w_o, l1g, l1b,
             wr_hi, wr_lo, b_r, *, tm, alpha, n_experts):
    t, dm = x2.shape
    sw = ya.shape[1]
    rows = lambda i: (i, 0)
    const = lambda i: (0, 0)
    full = lambda a: pl.BlockSpec(a.shape, const)
    return pl.pallas_call(
        functools.partial(_postmix_kernel, alpha=alpha, n_experts=n_experts),
        grid=(t // tm,),
        in_specs=[pl.BlockSpec((tm, dm), rows), pl.BlockSpec((tm, sw), rows), pl.BlockSpec((tm, sw), rows),
                  pl.BlockSpec((tm, dm), rows), pl.BlockSpec((tm, dm), rows),
                  full(ln_g), full(ln_b), full(w_glu), full(b_glu), full(w_a), full(w_b), full(w_o),
                  full(l1g), full(l1b), full(wr_hi), full(wr_lo), full(b_r)],
        out_specs=[pl.BlockSpec((tm, 1, dm), lambda i: (i, 0, 0)), pl.BlockSpec((tm, LANES), rows),
                   pl.BlockSpec((tm, LANES), rows), pl.BlockSpec((8, LANES), const)],
        out_shape=[jax.ShapeDtypeStruct((t, 1, dm), F32), jax.ShapeDtypeStruct((t, LANES), jnp.int32),
                   jax.ShapeDtypeStruct((t, LANES), F32), jax.ShapeDtypeStruct((8, LANES), F32)],
        scratch_shapes=[pltpu.VMEM((1, LANES), F32)],
        compiler_params=pltpu.CompilerParams(
            dimension_semantics=("arbitrary",), vmem_limit_bytes=48 * 1024 * 1024),
        name="postmix_router",
    )(x2, ya, yb, ga, gb, ln_g, ln_b, w_glu, b_glu, w_a, w_b, w_o, l1g, l1b, wr_hi, wr_lo, b_r)


DMA_WINDOW = 512
MOE_ROW_TILE = 512


def _row_copy_loop(n_rows, start_copy, wait_copy):
    def body(i, carry):
        start_copy(i)

        @pl.when(i >= DMA_WINDOW)
        def _():
            wait_copy()
        return carry
    lax.fori_loop(0, n_rows, body, 0)
    lax.fori_loop(0, jnp.minimum(n_rows, DMA_WINDOW), lambda i, c: (wait_copy(), c)[1], 0)


def _dispatch_kernel(dest_ref, pad_ref, npad_ref, h_hbm, xs_hbm, sem):
    def wait_copy():
        pltpu.make_async_copy(h_hbm.at[0], xs_hbm.at[0], sem).wait()

    def start_token(i):
        pltpu.make_async_copy(h_hbm.at[i // TOP_K], xs_hbm.at[dest_ref[i]], sem).start()

    def start_pad(i):
        pltpu.make_async_copy(h_hbm.at[0], xs_hbm.at[pad_ref[i]], sem).start()

    _row_copy_loop(dest_ref.shape[0], start_token, wait_copy)
    _row_copy_loop(npad_ref[0], start_pad, wait_copy)


def _dispatch(dest, pad_dest, n_pad, h1_rows, n_rows):
    t, _, dm = h1_rows.shape
    return pl.pallas_call(
        _dispatch_kernel,
        grid_spec=pltpu.PrefetchScalarGridSpec(
            num_scalar_prefetch=3, grid=(1,),
            in_specs=[pl.BlockSpec(memory_space=pl.ANY)],
            out_specs=pl.BlockSpec(memory_space=pl.ANY),
            scratch_shapes=[pltpu.SemaphoreType.DMA(())]),
        out_shape=jax.ShapeDtypeStruct((n_rows, 1, dm), F32),
        compiler_params=pltpu.CompilerParams(dimension_semantics=("arbitrary",)),
        name="moe_dispatch",
    )(dest, pad_dest, n_pad, h1_rows)


def _moe_kernel(te_ref, nv_ref, xs_ref, wg_ref, bg_ref, wu_ref, bu_ref, wd_ref, bd_ref, ys_ref, x2_ref):
    i = pl.program_id(0)

    @pl.when(i < nv_ref[0])
    def _():
        x2_ref[...] = xs_ref[...].reshape(x2_ref.shape)
        x = x2_ref[...].astype(BF16)
        gate = jnp.minimum(jnp.dot(x, wg_ref[0], preferred_element_type=F32) + bg_ref[0], SWIGLU_LIMIT)
        up = jnp.clip(jnp.dot(x, wu_ref[0], preferred_element_type=F32) + bu_ref[0],
                      -SWIGLU_LIMIT, SWIGLU_LIMIT)
        act = (up + 1.0) * gate * _sigmoid(SWIGLU_ALPHA * gate)
        y = jnp.dot(act.astype(BF16), wd_ref[0], preferred_element_type=F32) + bd_ref[0]
        ys_ref[...] = y.reshape(ys_ref.shape)


def _moe(tile_expert, n_valid, xs, w_gate, b_gate, w_up, b_up, w_down, b_down, *, tm):
    n_rows, _, dm = xs.shape
    _, _, de = w_gate.shape
    rows = lambda i, te, nv: (jnp.minimum(i, nv[0] - 1), 0, 0)
    per_e = lambda i, te, nv: (te[i], 0, 0)
    return pl.pallas_call(
        _moe_kernel,
        grid_spec=pltpu.PrefetchScalarGridSpec(
            num_scalar_prefetch=2, grid=(n_rows // tm,),
            in_specs=[pl.BlockSpec((tm, 1, dm), rows),
                      pl.BlockSpec((1, dm, de), per_e), pl.BlockSpec((1, 1, de), per_e),
                      pl.BlockSpec((1, dm, de), per_e), pl.BlockSpec((1, 1, de), per_e),
                      pl.BlockSpec((1, de, dm), per_e), pl.BlockSpec((1, 1, dm), per_e)],
            out_specs=pl.BlockSpec((tm, 1, dm), rows),
            scratch_shapes=[pltpu.VMEM((tm, dm), F32)]),
        out_shape=jax.ShapeDtypeStruct((n_rows, 1, dm), F32),
        compiler_params=pltpu.CompilerParams(
            dimension_semantics=("arbitrary",), vmem_limit_bytes=56 * 1024 * 1024),
        name="moe_experts",
    )(tile_expert, n_valid, xs, w_gate, b_gate, w_up, b_up, w_down, b_down)


def _combine_kernel(dest_ref, ys_hbm, h1_ref, rw_ref, l2g_ref, l2b_ref, o_ref, buf_ref, rows2_ref, sem,
                    *, alpha):
    i = pl.program_id(0)
    n = pl.num_programs(0)
    tm, dm = o_ref.shape
    per_tile = TOP_K * tm

    def copy(tile, slot, j):
        src = dest_ref[(tile * tm + j % tm) * TOP_K + j // tm]
        return pltpu.make_async_copy(ys_hbm.at[src], buf_ref.at[slot, j], sem.at[slot])

    def start_tile(tile, slot):
        lax.fori_loop(0, per_tile, lambda j, c: (copy(tile, slot, j).start(), c)[1], 0)

    @pl.when(i == 0)
    def _():
        start_tile(0, 0)

    @pl.when(i + 1 < n)
    def _():
        start_tile(i + 1, (i + 1) % 2)

    slot = i % 2

    def wait_row(j, c):
        pltpu.make_async_copy(ys_hbm.at[0], buf_ref.at[slot, 0], sem.at[slot]).wait()
        return c
    lax.fori_loop(0, per_tile, wait_row, 0)
    rows2_ref[...] = buf_ref[slot].reshape(rows2_ref.shape)
    rw = rw_ref[...]
    ffn = jnp.zeros((tm, dm), F32)
    for kk in range(TOP_K):
        ffn = ffn + rw[:, kk:kk + 1] * rows2_ref[kk * tm:(kk + 1) * tm, :]
    rows2_ref[0:tm, :] = h1_ref[...].reshape(tm, dm)
    o_ref[...] = _layer_norm(alpha * rows2_ref[0:tm, :] + ffn, l2g_ref[...], l2b_ref[...])


def _combine(dest, ys, h1_rows, rw, l2g, l2b, *, tm, alpha):
    t, _, dm = h1_rows.shape
    rows = lambda i, d: (i, 0)
    const = lambda i, d: (0, 0)
    return pl.pallas_call(
        functools.partial(_combine_kernel, alpha=alpha),
        grid_spec=pltpu.PrefetchScalarGridSpec(
            num_scalar_prefetch=1, grid=(t // tm,),
            in_specs=[pl.BlockSpec(memory_space=pl.ANY),
                      pl.BlockSpec((tm, 1, dm), lambda i, d: (i, 0, 0)),
                      pl.BlockSpec((tm, LANES), rows),
                      pl.BlockSpec((1, dm), const), pl.BlockSpec((1, dm), const)],
            out_specs=pl.BlockSpec((tm, dm), rows),
            scratch_shapes=[pltpu.VMEM((2, TOP_K * tm, 1, dm), F32),
                            pltpu.VMEM((TOP_K * tm, dm), F32),
                            pltpu.SemaphoreType.DMA((2,))]),
        out_shape=jax.ShapeDtypeStruct((t, dm), F32),
        compiler_params=pltpu.CompilerParams(
            dimension_semantics=("arbitrary",), vmem_limit_bytes=48 * 1024 * 1024),
        name="moe_combine_ln2",
    )(dest, ys, h1_rows, rw, l2g, l2b)


def _routing_tables(ridx, counts, *, n_experts, tm, n_rows):
    e_idx = ridx[:, :TOP_K]
    rank = ridx[:, TOP_K:2 * TOP_K]
    cnt = counts[0, :n_experts].astype(jnp.int32)
    padded = (cnt + tm - 1) // tm * tm
    ends = jnp.cumsum(padded)
    offs = ends - padded
    experts = jnp.arange(n_experts, dtype=jnp.int32)
    base = jnp.sum(jnp.where(e_idx[..., None] == experts, offs, 0), axis=-1)
    dest = (base + rank).reshape(-1)
    tile_start = jnp.arange(n_rows // tm, dtype=jnp.int32) * tm
    tile_expert = jnp.minimum(jnp.sum(ends[None, :] <= tile_start[:, None], axis=1), n_experts - 1)
    n_valid = (ends[-1] // tm).reshape(1)
    n_fill = padded - cnt
    fill_end = jnp.cumsum(n_fill)
    j = jnp.arange(n_experts * tm, dtype=jnp.int32)
    e_j = jnp.minimum(jnp.sum(fill_end[None, :] <= j[:, None], axis=1), n_experts - 1)
    pad_dest = offs[e_j] + cnt[e_j] + (j - (fill_end[e_j] - n_fill[e_j]))
    pad_dest = jnp.clip(pad_dest, 0, n_rows - 1)
    return dest, tile_expert.astype(jnp.int32), n_valid.astype(jnp.int32), pad_dest.astype(jnp.int32), \
        fill_end[-1].reshape(1).astype(jnp.int32)


def _pick_tile(n, want):
    t = min(n, want)
    assert n % t == 0, (n, t)
    return t


def kernel(x, meta, ln_in_g, ln_in_b, w_in, b_f, w_up_a, w_up_b, w_o, ssm_a_re, ssm_a_im, ssm_log_dt,
           ssm_b_re, ssm_b_im, ssm_c_re, ssm_c_im, ssm_d, w_glu, b_glu, ln1_g, ln1_b, w_router, b_router,
           w_gate, b_gate, w_up, b_up, w_down, b_down, ln2_g, ln2_b):
    bsz, seq, dm = x.shape
    depth = w_in.shape[0]
    assert depth == 1 and meta.shape[0] == N_META
    alpha = (2.0 * depth) ** 0.25
    n_groups = ssm_a_re.shape[1]
    n_experts = w_router.shape[-1]
    aw = ATT_WIDTH
    t = bsz * seq
    x2 = x.reshape(t, dm)
    row = lambda a: a.reshape(1, -1)

    f_off = 3 * aw
    u_off = f_off + ATT_HEADS
    w0 = w_in[0]
    w_main = jnp.concatenate([w0[:, :f_off], w0[:, u_off:]], axis=1).astype(BF16)
    w_f = jnp.pad(w0[:, f_off:u_off], ((0, 0), (0, LANES - ATT_HEADS))).astype(BF16)
    b_f_pad = jnp.pad(b_f[0], (0, LANES - ATT_HEADS)).reshape(1, LANES)
    ln_g, ln_b = row(ln_in_g), row(ln_in_b)

    tm = _pick_tile(seq, 512)
    q, k, v, u, ga, gb, fcum = _inproj(x2, ln_g, ln_b, w_main, w_f, b_f_pad, tm=tm, tiles_per_seq=seq // tm)
    _, k_m, v_m, u_m, _, _, fcum_m = _inproj(meta, ln_g, ln_b, w_main, w_f, b_f_pad, tm=N_META, tiles_per_seq=1)

    pad_m = META_KEYS_PAD - N_META
    k_meta = jnp.pad(k_m, ((0, pad_m), (0, 0)))
    v_meta = jnp.pad(v_m, ((0, pad_m), (0, 0)))
    fm = fcum_m[:, :ATT_HEADS]
    fk_meta = jnp.pad((fm - fm[N_META - 1:N_META, :]).T, ((0, 0), (0, pad_m)), constant_values=BIG)
    fk_rows = jnp.transpose(fcum[:, :ATT_HEADS].reshape(bsz, seq, ATT_HEADS), (0, 2, 1))
    tq = _pick_tile(seq, 512)
    y_a = _attn(q, k, v, fcum, fk_rows, k_meta, v_meta, fk_meta, bsz=bsz, seq=seq, tq=tq)

    chunk = SSM_CHUNK
    n_chunks = seq // chunk
    n_steps = max(1, (n_chunks - 1).bit_length())
    toep, wst, wout, wmeta, apow, d_tiled = _ssm_tables(
        ssm_a_re[0], ssm_a_im[0], ssm_log_dt[0], ssm_b_re[0], ssm_b_im[0], ssm_c_re[0], ssm_c_im[0], ssm_d[0],
        chunk=chunk, n_steps=n_steps)
    u_g = jnp.transpose(u.reshape(bsz * n_chunks, chunk, n_groups, SSM_GROUP), (2, 0, 1, 3))
    u_g = u_g.reshape(n_groups, bsz * n_chunks, chunk * SSM_GROUP)
    um_g = jnp.transpose(u_m.reshape(N_META, n_groups, SSM_GROUP), (1, 0, 2)).reshape(n_groups, 1, N_META * SSM_GROUP)
    um_g = jnp.pad(um_g, ((0, 0), (0, 15), (0, 0)))
    y_g = _ssm(u_g, toep, wst, wout, um_g, wmeta, apow, d_tiled, chunks_per_seq=n_chunks)
    y_b = jnp.transpose(y_g.reshape(n_groups, bsz * n_chunks, chunk, SSM_GROUP), (1, 2, 0, 3)).reshape(t, -1)

    wr = jnp.pad(w_router[0], ((0, 0), (0, LANES - n_experts)))
    wr_hi = wr.astype(BF16)
    wr_lo = (wr - wr_hi.astype(F32)).astype(BF16)
    b_r = jnp.pad(b_router[0], (0, LANES - n_experts)).reshape(1, LANES)
    tm2 = _pick_tile(t, 512)
    h1_rows, ridx, rw, counts = _postmix(
        x2, y_a, y_b, ga, gb, ln_g, ln_b, w_glu[0].astype(BF16), row(b_glu[0]),
        w_up_a[0].astype(BF16), w_up_b[0].astype(BF16), w_o[0].astype(BF16), row(ln1_g[0]), row(ln1_b[0]),
        wr_hi, wr_lo, b_r, tm=tm2, alpha=alpha, n_experts=n_experts)

    tm3 = MOE_ROW_TILE
    n_rows = TOP_K * t + n_experts * tm3
    dest, tile_expert, n_valid, pad_dest, n_pad = _routing_tables(
        ridx, counts, n_experts=n_experts, tm=tm3, n_rows=n_rows)
    xs = _dispatch(dest, pad_dest, n_pad, h1_rows, n_rows)
    ys = _moe(tile_expert, n_valid, xs, w_gate[0].astype(BF16), b_gate[0][:, None, :], w_up[0].astype(BF16),
              b_up[0][:, None, :], w_down[0].astype(BF16), b_down[0][:, None, :], tm=tm3)
    tm4 = _pick_tile(t, 256)
    out = _combine(dest, ys, h1_rows, rw, row(ln2_g[0]), row(ln2_b[0]), tm=tm4, alpha=alpha)
    return out.reshape(bsz, seq, dm)
```

```python
import functools
import math

import jax
import jax.numpy as jnp
from jax import lax
from jax.experimental import pallas as pl
from jax.experimental.pallas import tpu as pltpu

F32 = jnp.float32
BF16 = jnp.bfloat16

N_META = 16
ATT_HEADS = 8
HEAD_DIM = 64
ATT_WIDTH = ATT_HEADS * HEAD_DIM
SSM_GROUP = 16
SSM_STATE = 64
TOP_K = 4
SWIGLU_LIMIT = 7.0
SWIGLU_ALPHA = 1.702
LN_EPS = 1e-5

LANES = 128
SSM_CHUNK = 64
META_KEYS_PAD = LANES
BIG = 1e30


def _layer_norm(x, g, b):
    mu = jnp.mean(x, axis=-1, keepdims=True)
    xc = x - mu
    var = jnp.mean(xc * xc, axis=-1, keepdims=True)
    return xc * lax.rsqrt(var + LN_EPS) * g + b


def _log_sigmoid(z):
    return jnp.minimum(z, 0.0) - jnp.log1p(jnp.exp(-jnp.abs(z)))


def _sigmoid(z):
    return 1.0 / (1.0 + jnp.exp(-z))


def _split3(x):
    hi = x.astype(BF16)
    r1 = x - hi.astype(F32)
    mid = r1.astype(BF16)
    lo = (r1 - mid.astype(F32)).astype(BF16)
    return hi, mid, lo


def _inproj_kernel(x_ref, g_ref, b_ref, w_ref, wf_ref, bf_ref,
                   q_ref, k_ref, v_ref, u_ref, ga_ref, gb_ref, fc_ref, carry_ref,
                   *, tiles_per_seq):
    i = pl.program_id(0)
    tm = x_ref.shape[0]
    h = _layer_norm(x_ref[...], g_ref[...], b_ref[...]).astype(BF16)

    def proj(lo, hi):
        return jnp.dot(h, w_ref[:, lo:hi], preferred_element_type=F32)

    aw = ATT_WIDTH
    q_ref[...] = (proj(0, aw) * (1.0 / math.sqrt(HEAD_DIM))).astype(BF16)
    k_ref[...] = proj(aw, 2 * aw).astype(BF16)
    v_ref[...] = proj(2 * aw, 3 * aw).astype(BF16)
    u_ref[...] = proj(3 * aw, 4 * aw).astype(BF16)
    dm = ga_ref.shape[1]
    ga_ref[...] = _sigmoid(proj(4 * aw, 4 * aw + dm)).astype(BF16)
    gb_ref[...] = _sigmoid(proj(4 * aw + dm, 4 * aw + 2 * dm)).astype(BF16)

    zf = jnp.dot(h, wf_ref[...], preferred_element_type=F32) + bf_ref[...]
    lf = _log_sigmoid(zf)

    @pl.when(i % tiles_per_seq == 0)
    def _():
        carry_ref[...] = jnp.zeros_like(carry_ref)

    row = lax.broadcasted_iota(jnp.int32, (tm, tm), 0)
    col = lax.broadcasted_iota(jnp.int32, (tm, tm), 1)
    tri = (col <= row).astype(BF16)
    hi, mid, lo = _split3(lf)
    cs = (jnp.dot(tri, hi, preferred_element_type=F32)
          + jnp.dot(tri, mid, preferred_element_type=F32)
          + jnp.dot(tri, lo, preferred_element_type=F32)) + carry_ref[...]
    fc_ref[...] = cs
    carry_ref[...] = cs[tm - 1:tm, :]


def _inproj(x2, ln_g, ln_b, w_main, w_f, b_f, *, tm, tiles_per_seq):
    t, dm = x2.shape
    aw = ATT_WIDTH
    n_main = w_main.shape[1]
    const = lambda i: (0, 0)
    rows = lambda i: (i, 0)
    out_shape = [jax.ShapeDtypeStruct((t, aw), BF16)] * 4 + \
                [jax.ShapeDtypeStruct((t, dm), BF16)] * 2 + \
                [jax.ShapeDtypeStruct((t, LANES), F32)]
    out_specs = [pl.BlockSpec((tm, aw), rows)] * 4 + [pl.BlockSpec((tm, dm), rows)] * 2 + \
                [pl.BlockSpec((tm, LANES), rows)]
    return pl.pallas_call(
        functools.partial(_inproj_kernel, tiles_per_seq=tiles_per_seq),
        grid=(t // tm,),
        in_specs=[pl.BlockSpec((tm, dm), rows),
                  pl.BlockSpec((1, dm), const), pl.BlockSpec((1, dm), const),
                  pl.BlockSpec((dm, n_main), const),
                  pl.BlockSpec((dm, LANES), const), pl.BlockSpec((1, LANES), const)],
        out_specs=out_specs,
        out_shape=out_shape,
        scratch_shapes=[pltpu.VMEM((1, LANES), F32)],
        compiler_params=pltpu.CompilerParams(
            dimension_semantics=("arbitrary",), vmem_limit_bytes=56 * 1024 * 1024),
        name="inproj",
    )(x2, ln_g, ln_b, w_main, w_f, b_f)


def _attn_kernel(qi_tab, ki_tab, q_ref, k_ref, v_ref, fk_ref, f0_ref, km_ref, vm_ref, fkm_ref,
                 o_ref, m_ref, acc_ref):
    p = pl.program_id(1)
    qi = qi_tab[p]
    ki = ki_tab[p]
    tq = q_ref.shape[0]
    lane = lax.broadcasted_iota(jnp.int32, (1, LANES), 1)
    first_head = lane < HEAD_DIM
    f0 = f0_ref[0][:, 0:1]

    def process(k_blk, v_blk, fk_rows, mask):
        n = k_blk.shape[0]
        bias = fk_rows - f0
        for hp in range(ATT_HEADS // 2):
            sl = slice(LANES * hp, LANES * (hp + 1))
            q2 = q_ref[:, sl]
            k2 = k_blk[:, sl]
            v2 = v_blk[:, sl]
            for half in range(2):
                h = 2 * hp + half
                sel = first_head if half == 0 else jnp.logical_not(first_head)
                qm = jnp.where(sel, q2, jnp.zeros_like(q2))
                s = lax.dot_general(qm, k2, (((1,), (1,)), ((), ())), preferred_element_type=F32)
                s = s - bias[h:h + 1, :]
                if mask is not None:
                    s = jnp.where(mask, s, -jnp.inf)
                m_old = m_ref[h]
                m_new = jnp.maximum(m_old, jnp.broadcast_to(jnp.max(s, axis=-1, keepdims=True), m_old.shape))
                m_ref[h] = m_new
                pm = jnp.concatenate(
                    [jnp.exp(s[:, c * LANES:(c + 1) * LANES] - m_new) for c in range(n // LANES)], axis=1)
                v_ones = jnp.where(sel, v2, jnp.ones_like(v2))
                pv = jnp.dot(pm.astype(BF16), v_ones, preferred_element_type=F32)
                acc_ref[h] = jnp.exp(m_old - m_new) * acc_ref[h] + pv

    @pl.when(ki == 0)
    def _():
        m_ref[...] = jnp.full_like(m_ref, -jnp.inf)
        acc_ref[...] = jnp.zeros_like(acc_ref)
        process(km_ref[...], vm_ref[...], fkm_ref[...], None)

    @pl.when(ki < qi)
    def _():
        process(k_ref[...], v_ref[...], fk_ref[0], None)

    @pl.when(ki == qi)
    def _():
        tk = k_ref.shape[0]
        row = lax.broadcasted_iota(jnp.int32, (tq, tk), 0)
        col = lax.broadcasted_iota(jnp.int32, (tq, tk), 1)
        process(k_ref[...], v_ref[...], fk_ref[0], col <= row)
        for hp in range(ATT_HEADS // 2):
            a0 = acc_ref[2 * hp]
            a1 = acc_ref[2 * hp + 1]
            num = jnp.where(first_head, a0, a1)
            den = jnp.where(first_head, pltpu.roll(a0, HEAD_DIM, axis=1), pltpu.roll(a1, HEAD_DIM, axis=1))
            o_ref[:, LANES * hp:LANES * (hp + 1)] = (num / den).astype(o_ref.dtype)


def _attn(q, k, v, fk_rows, f0, k_meta, v_meta, fk_meta, *, bsz, seq, tq):
    nq = seq // tq
    pairs = [(a, b) for a in range(nq) for b in range(a + 1)]
    qi_tab = jnp.asarray([a for a, _ in pairs], jnp.int32)
    ki_tab = jnp.asarray([b for _, b in pairs], jnp.int32)
    w = q.shape[1]
    qmap = lambda b, p, qt, kt: (b * nq + qt[p], 0)
    kmap = lambda b, p, qt, kt: (b * nq + kt[p], 0)
    const = lambda b, p, qt, kt: (0, 0)
    grid_spec = pltpu.PrefetchScalarGridSpec(
        num_scalar_prefetch=2,
        grid=(bsz, len(pairs)),
        in_specs=[pl.BlockSpec((tq, w), qmap),
                  pl.BlockSpec((tq, w), kmap),
                  pl.BlockSpec((tq, w), kmap),
                  pl.BlockSpec((1, ATT_HEADS, tq), lambda b, p, qt, kt: (b, 0, kt[p])),
                  pl.BlockSpec((1, ATT_HEADS, LANES), lambda b, p, qt, kt: (b * nq + qt[p], 0, 0)),
                  pl.BlockSpec((META_KEYS_PAD, w), const),
                  pl.BlockSpec((META_KEYS_PAD, w), const),
                  pl.BlockSpec((ATT_HEADS, META_KEYS_PAD), const)],
        out_specs=pl.BlockSpec((tq, w), qmap),
        scratch_shapes=[pltpu.VMEM((ATT_HEADS, tq, LANES), F32),
                        pltpu.VMEM((ATT_HEADS, tq, LANES), F32)],
    )
    return pl.pallas_call(
        _attn_kernel,
        grid_spec=grid_spec,
        out_shape=jax.ShapeDtypeStruct((bsz * seq, w), BF16),
        compiler_params=pltpu.CompilerParams(
            dimension_semantics=("arbitrary", "arbitrary"), vmem_limit_bytes=48 * 1024 * 1024),
        name="fox_attention",
    )(qi_tab, ki_tab, q, k, v, fk_rows, f0, k_meta, v_meta, fk_meta)


def _ssm_kernel(u_ref, kt_ref, wst_ref, wout_ref, um_ref, wm_ref, apow_ref, d_ref, y_ref, toep_ref,
                *, chunks_per_seq, n_steps):
    u = u_ref[0]
    rows = u.shape[0]
    p2 = 2 * SSM_STATE
    kt = kt_ref[0]
    n_c, tc = kt.shape
    per_vreg = LANES // n_c
    ext = jnp.concatenate([jnp.zeros_like(kt), kt], axis=1)
    for sub in range(per_vreg):
        rot = ext if sub == 0 else pltpu.roll(ext, n_c * sub, axis=1)
        for whole in range(tc // LANES):
            tp = per_vreg * whole + sub
            toep_ref[tp * n_c:(tp + 1) * n_c, :] = rot[:, tc - LANES * whole:2 * tc - LANES * whole].astype(BF16)
    y = jnp.dot(u, toep_ref[...], preferred_element_type=F32)
    s = jnp.dot(u, wst_ref[0], preferred_element_type=F32)
    x0 = jnp.dot(um_ref[0], wm_ref[0], preferred_element_type=F32)[0:1, :]

    lane = lax.broadcasted_iota(jnp.int32, (1, p2), 1)
    re_half = lane < SSM_STATE
    j = lax.broadcasted_iota(jnp.int32, (rows, 1), 0) % chunks_per_seq

    def cmul(step, z):
        ar = apow_ref[0, step:step + 1, :]
        ai = apow_ref[0, n_steps + step:n_steps + step + 1, :]
        return ar * z + ai * pltpu.roll(z, SSM_STATE, axis=1)

    s = s + jnp.where(j == 0, cmul(0, jnp.broadcast_to(x0, s.shape)), 0.0)
    for step in range(n_steps):
        sh = 1 << step
        prev = pltpu.roll(s, sh, axis=0)
        s = s + jnp.where(j >= sh, cmul(step, prev), 0.0)
    x_in = jnp.where(j == 0, x0, pltpu.roll(s, 1, axis=0))
    del re_half
    y = y + jnp.dot(x_in.astype(BF16), wout_ref[0], preferred_element_type=F32)
    y = y + d_ref[0] * u.astype(F32)
    y_ref[0] = jax.nn.gelu(y).astype(y_ref.dtype)


def _ssm(u_g, kt, wst, wout, u_meta, wmeta, apow, d_tiled, *, chunks_per_seq):
    g, rows, tc = u_g.shape
    n_steps = apow.shape[1] // 2
    grp = lambda i: (i, 0, 0)
    return pl.pallas_call(
        functools.partial(_ssm_kernel, chunks_per_seq=chunks_per_seq, n_steps=n_steps),
        grid=(g,),
        in_specs=[pl.BlockSpec((1, rows, tc), grp),
                  pl.BlockSpec((1,) + kt.shape[1:], grp),
                  pl.BlockSpec((1, tc, 2 * SSM_STATE), grp),
                  pl.BlockSpec((1, 2 * SSM_STATE, tc), grp),
                  pl.BlockSpec((1,) + u_meta.shape[1:], grp),
                  pl.BlockSpec((1,) + wmeta.shape[1:], grp),
                  pl.BlockSpec((1,) + apow.shape[1:], grp),
                  pl.BlockSpec((1, 1, tc), grp)],
        out_specs=pl.BlockSpec((1, rows, tc), grp),
        out_shape=jax.ShapeDtypeStruct((g, rows, tc), BF16),
        scratch_shapes=[pltpu.VMEM((tc, tc), BF16)],
        compiler_params=pltpu.CompilerParams(
            dimension_semantics=("arbitrary",), vmem_limit_bytes=48 * 1024 * 1024),
        name="s5_ssm",
    )(u_g, kt, wst, wout, u_meta, wmeta, apow, d_tiled)


def _ssm_tables(a_re, a_im, log_dt, b_re, b_im, c_re, c_im, d_skip, *, chunk, n_steps):
    g, p = a_re.shape
    c = b_re.shape[-1]
    dt = jnp.exp(log_dt)[:, None]
    mag = jnp.exp(a_re * dt)
    ang = a_im * dt
    lb_re, lb_im = mag * jnp.cos(ang), mag * jnp.sin(ang)
    den = a_re * a_re + a_im * a_im
    z_re, z_im = lb_re - 1.0, lb_im
    coef_re = (z_re * a_re + z_im * a_im) / den
    coef_im = (z_im * a_re - z_re * a_im) / den
    bb_re = coef_re[..., None] * b_re - coef_im[..., None] * b_im
    bb_im = coef_re[..., None] * b_im + coef_im[..., None] * b_re

    def step(carry, _):
        cr, ci = carry
        return (cr * lb_re - ci * lb_im, cr * lb_im + ci * lb_re), (cr, ci)
    _, (pw_re, pw_im) = lax.scan(step, (jnp.ones_like(lb_re), jnp.zeros_like(lb_re)), None,
                                 length=chunk + 1)

    e_re = c_re[None] * pw_re[:, :, None, :] - c_im[None] * pw_im[:, :, None, :]
    e_im = c_re[None] * pw_im[:, :, None, :] + c_im[None] * pw_re[:, :, None, :]
    kern = (jnp.einsum('tgcp,gpd->tgcd', e_re[:chunk], bb_re)
            - jnp.einsum('tgcp,gpd->tgcd', e_im[:chunk], bb_im))
    kt = jnp.transpose(kern, (1, 3, 0, 2)).reshape(g, c, chunk * c)

    def in_to_state(n):
        wr = pw_re[n - 1::-1][:n, :, :, None] * bb_re[None] - pw_im[n - 1::-1][:n, :, :, None] * bb_im[None]
        wi = pw_re[n - 1::-1][:n, :, :, None] * bb_im[None] + pw_im[n - 1::-1][:n, :, :, None] * bb_re[None]
        w = jnp.concatenate([wr, wi], axis=2)
        return jnp.transpose(w, (1, 0, 3, 2)).reshape(g, n * c, 2 * p)
    wst = in_to_state(chunk)
    wmeta = in_to_state(N_META)

    wout = jnp.concatenate([e_re[1:chunk + 1], -e_im[1:chunk + 1]], axis=3)
    wout = jnp.transpose(wout, (1, 3, 0, 2)).reshape(g, 2 * p, chunk * c)

    ar, ai = pw_re[chunk], pw_im[chunk]
    rows_r, rows_i = [], []
    for _ in range(n_steps):
        rows_r.append(jnp.concatenate([ar, ar], axis=-1))
        rows_i.append(jnp.concatenate([-ai, ai], axis=-1))
        ar, ai = ar * ar - ai * ai, 2.0 * ar * ai
    apow = jnp.stack(rows_r + rows_i, axis=1)
    d_tiled = jnp.tile(d_skip, (1, chunk))[:, None, :]
    return kt, wst.astype(BF16), wout.astype(BF16), wmeta.astype(BF16), apow, d_tiled


def _postmix_kernel(x_ref, ya_ref, yb_ref, ga_ref, gb_ref, lng_ref, lnb_ref,
                    wglu_ref, bglu_ref, wa_ref, wb_ref, wo_ref, l1g_ref, l1b_ref,
                    wrh_ref, wrl_ref, br_ref, h1_ref, ridx_ref, rw_ref, cnt_ref, carry_ref,
                    *, alpha, n_experts):
    i = pl.program_id(0)
    tm = x_ref.shape[0]
    h0 = _layer_norm(x_ref[...], lng_ref[...], lnb_ref[...])
    yb = yb_ref[...]
    glu = yb.astype(F32) * _sigmoid(jnp.dot(yb, wglu_ref[...], preferred_element_type=F32) + bglu_ref[...])
    merged = (ga_ref[...].astype(F32) * jnp.dot(ya_ref[...], wa_ref[...], preferred_element_type=F32)
              + gb_ref[...].astype(F32) * jnp.dot(glu.astype(BF16), wb_ref[...], preferred_element_type=F32))
    mix = jnp.dot(merged.astype(BF16), wo_ref[...], preferred_element_type=F32)
    h1 = _layer_norm(alpha * h0 + mix, l1g_ref[...], l1b_ref[...])
    h1_ref[...] = h1.reshape(h1_ref.shape)
    hb = h1.astype(BF16)

    hl = (h1 - hb.astype(F32)).astype(BF16)
    logits = (jnp.dot(hb, wrh_ref[...], preferred_element_type=F32)
              + jnp.dot(hb, wrl_ref[...], preferred_element_type=F32)
              + jnp.dot(hl, wrh_ref[...], preferred_element_type=F32)) + br_ref[...]
    lane = lax.broadcasted_iota(jnp.int32, logits.shape, 1)
    logits = jnp.where(lane < n_experts, logits, -jnp.inf)
    denom = jnp.zeros((tm, 1), F32)
    top = None
    hits, idxs, ws = [], [], []
    for _ in range(TOP_K):
        mx = jnp.max(logits, axis=-1, keepdims=True)
        idx = jnp.min(jnp.where(logits == mx, lane, LANES), axis=-1, keepdims=True)
        hit = lane == idx
        if top is None:
            top = mx
        w = jnp.exp(mx - top)
        denom = denom + w
        hits.append(hit)
        idxs.append(idx)
        ws.append(w)
        logits = jnp.where(hit, -jnp.inf, logits)

    @pl.when(i == 0)
    def _():
        carry_ref[...] = jnp.zeros_like(carry_ref)

    sel = jnp.zeros(logits.shape, F32)
    for hit in hits:
        sel = sel + jnp.where(hit, 1.0, 0.0)
    row = lax.broadcasted_iota(jnp.int32, (tm, tm), 0)
    col = lax.broadcasted_iota(jnp.int32, (tm, tm), 1)
    before = (col < row).astype(BF16)
    seen = jnp.dot(before, sel.astype(BF16), preferred_element_type=F32) + carry_ref[...]
    ridx = jnp.zeros(logits.shape, jnp.int32)
    rw = jnp.zeros(logits.shape, F32)
    for kk in range(TOP_K):
        rank = jnp.sum(jnp.where(hits[kk], seen, 0.0), axis=-1, keepdims=True).astype(jnp.int32)
        ridx = jnp.where(lane == kk, idxs[kk], ridx)
        ridx = jnp.where(lane == TOP_K + kk, rank, ridx)
        rw = jnp.where(lane == kk, ws[kk] / denom, rw)
    ridx_ref[...] = ridx
    rw_ref[...] = rw
    total = carry_ref[...] + jnp.sum(sel, axis=0, keepdims=True)
    carry_ref[...] = total
    cnt_ref[...] = jnp.broadcast_to(total, cnt_ref.shape)


def _postmix(x2, ya, yb, ga, gb, ln_g, ln_b, w_glu, b_glu, w_a, w_b, w_o, l1g, l1b,
             wr_hi, wr_lo, b_r, *, tm, alpha, n_experts):
    t, dm = x2.shape
    sw = ya.shape[1]
    rows = lambda i: (i, 0)
    const = lambda i: (0, 0)
    full = lambda a: pl.BlockSpec(a.shape, const)
    return pl.pallas_call(
        functools.partial(_postmix_kernel, alpha=alpha, n_experts=n_experts),
        grid=(t // tm,),
        in_specs=[pl.BlockSpec((tm, dm), rows), pl.BlockSpec((tm, sw), rows), pl.BlockSpec((tm, sw), rows),
                  pl.BlockSpec((tm, dm), rows), pl.BlockSpec((tm, dm), rows),
                  full(ln_g), full(ln_b), full(w_glu), full(b_glu), full(w_a), full(w_b), full(w_o),
                  full(l1g), full(l1b), full(wr_hi), full(wr_lo), full(b_r)],
        out_specs=[pl.BlockSpec((tm, 1, dm), lambda i: (i, 0, 0)), pl.BlockSpec((tm, LANES), rows),
                   pl.BlockSpec((tm, LANES), rows), pl.BlockSpec((8, LANES), const)],
        out_shape=[jax.ShapeDtypeStruct((t, 1, dm), F32), jax.ShapeDtypeStruct((t, LANES), jnp.int32),
                   jax.ShapeDtypeStruct((t, LANES), F32), jax.ShapeDtypeStruct((8, LANES), F32)],
        scratch_shapes=[pltpu.VMEM((1, LANES), F32)],
        compiler_params=pltpu.CompilerParams(
            dimension_semantics=("arbitrary",), vmem_limit_bytes=48 * 1024 * 1024),
        name="postmix_router",
    )(x2, ya, yb, ga, gb, ln_g, ln_b, w_glu, b_glu, w_a, w_b, w_o, l1g, l1b, wr_hi, wr_lo, b_r)


MOE_ROW_TILE = 512


def _dispatch_kernel(dest_ref, pad_ref, npad_ref, nv_ref, h_ref, xs_hbm, sem):
    i = pl.program_id(0)
    tm = h_ref.shape[0]
    base = i * (tm * TOP_K)

    def start_token(r, c):
        for kk in range(TOP_K):
            pltpu.make_async_copy(h_ref.at[r], xs_hbm.at[dest_ref[base + r * TOP_K + kk]], sem).start()
        return c
    lax.fori_loop(0, tm, start_token, 0, unroll=4)
    done = xs_hbm.at[pl.ds(0, tm * TOP_K)]
    pltpu.make_async_copy(done, done, sem).wait()

    @pl.when(i == pl.num_programs(0) - 1)
    def _():
        def start_pad(j, c):
            pltpu.make_async_copy(h_ref.at[0], xs_hbm.at[pad_ref[j]], sem).start()
            return c

        def wait_pad(j, c):
            pltpu.make_async_copy(h_ref.at[0], xs_hbm.at[0], sem).wait()
            return c
        lax.fori_loop(0, npad_ref[0], start_pad, 0)
        lax.fori_loop(0, npad_ref[0], wait_pad, 0)

        def fill_tile(j, c):
            cp = pltpu.make_async_copy(h_ref, xs_hbm.at[pl.ds(j * tm, tm)], sem)
            cp.start()
            cp.wait()
            return c
        lax.fori_loop(nv_ref[0], xs_hbm.shape[0] // tm, fill_tile, 0)


def _dispatch(dest, pad_dest, n_pad, n_valid, h1_rows, n_rows, *, tm):
    t, _, dm = h1_rows.shape
    assert t % tm == 0 and n_rows % tm == 0
    return pl.pallas_call(
        _dispatch_kernel,
        grid_spec=pltpu.PrefetchScalarGridSpec(
            num_scalar_prefetch=4, grid=(t // tm,),
            in_specs=[pl.BlockSpec((tm, 1, dm), lambda i, d, p, n, v: (i, 0, 0))],
            out_specs=pl.BlockSpec(memory_space=pl.ANY),
            scratch_shapes=[pltpu.SemaphoreType.DMA(())]),
        out_shape=jax.ShapeDtypeStruct((n_rows, 1, dm), F32),
        compiler_params=pltpu.CompilerParams(dimension_semantics=("arbitrary",)),
        name="moe_dispatch",
    )(dest, pad_dest, n_pad, n_valid, h1_rows)


def _moe_kernel(te_ref, nv_ref, xs_ref, wg_ref, bg_ref, wu_ref, bu_ref, wd_ref, bd_ref, ys_ref,
                x2_ref, wgb_ref, wub_ref, wdb_ref):
    i = pl.program_id(0)
    valid = i < nv_ref[0]
    new_expert = jnp.logical_or(i == 0, te_ref[i] != te_ref[jnp.maximum(i - 1, 0)])

    @pl.when(jnp.logical_and(valid, new_expert))
    def _():
        wgb_ref[...] = wg_ref[0].astype(BF16)
        wub_ref[...] = wu_ref[0].astype(BF16)
        wdb_ref[...] = wd_ref[0].astype(BF16)

    @pl.when(valid)
    def _():
        x2_ref[...] = xs_ref[...].reshape(x2_ref.shape)
        x = x2_ref[...].astype(BF16)
        gate = jnp.minimum(jnp.dot(x, wgb_ref[...], preferred_element_type=F32) + bg_ref[0], SWIGLU_LIMIT)
        up = jnp.clip(jnp.dot(x, wub_ref[...], preferred_element_type=F32) + bu_ref[0],
                      -SWIGLU_LIMIT, SWIGLU_LIMIT)
        act = (up + 1.0) * gate * _sigmoid(SWIGLU_ALPHA * gate)
        y = jnp.dot(act.astype(BF16), wdb_ref[...], preferred_element_type=F32) + bd_ref[0]
        ys_ref[...] = y.reshape(ys_ref.shape)

    @pl.when(jnp.logical_not(valid))
    def _():
        ys_ref[...] = jnp.zeros_like(ys_ref)


def _moe(tile_expert, n_valid, xs, w_gate, b_gate, w_up, b_up, w_down, b_down, *, tm):
    n_rows, _, dm = xs.shape
    _, _, de = w_gate.shape
    rows = lambda i, te, nv: (jnp.minimum(i, nv[0] - 1), 0, 0)
    out_rows = lambda i, te, nv: (i, 0, 0)
    per_e = lambda i, te, nv: (te[i], 0, 0)
    return pl.pallas_call(
        _moe_kernel,
        grid_spec=pltpu.PrefetchScalarGridSpec(
            num_scalar_prefetch=2, grid=(n_rows // tm,),
            in_specs=[pl.BlockSpec((tm, 1, dm), rows),
                      pl.BlockSpec((1, dm, de), per_e), pl.BlockSpec((1, 1, de), per_e),
                      pl.BlockSpec((1, dm, de), per_e), pl.BlockSpec((1, 1, de), per_e),
                      pl.BlockSpec((1, de, dm), per_e), pl.BlockSpec((1, 1, dm), per_e)],
            out_specs=pl.BlockSpec((tm, 1, dm), out_rows),
            scratch_shapes=[pltpu.VMEM((tm, dm), F32), pltpu.VMEM((dm, de), BF16),
                            pltpu.VMEM((dm, de), BF16), pltpu.VMEM((de, dm), BF16)]),
        out_shape=jax.ShapeDtypeStruct((n_rows, 1, dm), F32),
        compiler_params=pltpu.CompilerParams(
            dimension_semantics=("arbitrary",), vmem_limit_bytes=56 * 1024 * 1024),
        name="moe_experts",
    )(tile_expert, n_valid, xs, w_gate, b_gate, w_up, b_up, w_down, b_down)


def _combine_kernel(gidx_ref, ys_hbm, h1_ref, rw_ref, l2g_ref, l2b_ref, o_ref, buf_ref, rows2_ref, sem,
                    *, alpha):
    i = pl.program_id(0)
    n = pl.num_programs(0)
    tm, dm = o_ref.shape
    per_tile = TOP_K * tm

    def start_tile(tile, slot):
        base = tile * per_tile

        def body(j, c):
            pltpu.make_async_copy(ys_hbm.at[gidx_ref[base + j]], buf_ref.at[slot, j], sem.at[slot]).start()
            return c
        lax.fori_loop(0, per_tile, body, 0, unroll=8)

    @pl.when(i == 0)
    def _():
        start_tile(0, 0)

    @pl.when(i + 1 < n)
    def _():
        start_tile(i + 1, (i + 1) % 2)

    slot = i % 2
    pltpu.make_async_copy(ys_hbm.at[pl.ds(0, per_tile)], buf_ref.at[slot], sem.at[slot]).wait()
    rows2_ref[...] = buf_ref[slot].reshape(rows2_ref.shape)
    rw = rw_ref[...]
    ffn = jnp.zeros((tm, dm), F32)
    for kk in range(TOP_K):
        ffn = ffn + rw[:, kk:kk + 1] * rows2_ref[kk * tm:(kk + 1) * tm, :]
    rows2_ref[0:tm, :] = h1_ref[...].reshape(tm, dm)
    o_ref[...] = _layer_norm(alpha * rows2_ref[0:tm, :] + ffn, l2g_ref[...], l2b_ref[...])


def _combine(dest, ys, h1_rows, rw, l2g, l2b, *, tm, alpha):
    t, _, dm = h1_rows.shape
    rows = lambda i, d: (i, 0)
    const = lambda i, d: (0, 0)
    return pl.pallas_call(
        functools.partial(_combine_kernel, alpha=alpha),
        grid_spec=pltpu.PrefetchScalarGridSpec(
            num_scalar_prefetch=1, grid=(t // tm,),
            in_specs=[pl.BlockSpec(memory_space=pl.ANY),
                      pl.BlockSpec((tm, 1, dm), lambda i, d: (i, 0, 0)),
                      pl.BlockSpec((tm, LANES), rows),
                      pl.BlockSpec((1, dm), const), pl.BlockSpec((1, dm), const)],
            out_specs=pl.BlockSpec((tm, dm), rows),
            scratch_shapes=[pltpu.VMEM((2, TOP_K * tm, 1, dm), F32),
                            pltpu.VMEM((TOP_K * tm, dm), F32),
                            pltpu.SemaphoreType.DMA((2,))]),
        out_shape=jax.ShapeDtypeStruct((t, dm), F32),
        compiler_params=pltpu.CompilerParams(
            dimension_semantics=("arbitrary",), vmem_limit_bytes=48 * 1024 * 1024),
        name="moe_combine_ln2",
    )(dest, ys, h1_rows, rw, l2g, l2b)


def _routing_tables(ridx, counts, *, n_experts, tm, n_rows):
    e_idx = ridx[:, :TOP_K]
    rank = ridx[:, TOP_K:2 * TOP_K]
    cnt = counts[0, :n_experts].astype(jnp.int32)
    padded = (cnt + tm - 1) // tm * tm
    ends = jnp.cumsum(padded)
    offs = ends - padded
    experts = jnp.arange(n_experts, dtype=jnp.int32)
    base = jnp.sum(jnp.where(e_idx[..., None] == experts, offs, 0), axis=-1)
    dest = (base + rank).reshape(-1)
    tile_start = jnp.arange(n_rows // tm, dtype=jnp.int32) * tm
    tile_expert = jnp.minimum(jnp.sum(ends[None, :] <= tile_start[:, None], axis=1), n_experts - 1)
    n_valid = (ends[-1] // tm).reshape(1)
    n_fill = padded - cnt
    fill_end = jnp.cumsum(n_fill)
    j = jnp.arange(n_experts * tm, dtype=jnp.int32)
    e_j = jnp.minimum(jnp.sum(fill_end[None, :] <= j[:, None], axis=1), n_experts - 1)
    pad_dest = offs[e_j] + cnt[e_j] + (j - (fill_end[e_j] - n_fill[e_j]))
    pad_dest = jnp.clip(pad_dest, 0, n_rows - 1)
    return dest, tile_expert.astype(jnp.int32), n_valid.astype(jnp.int32), pad_dest.astype(jnp.int32), \
        fill_end[-1].reshape(1).astype(jnp.int32)


def _pick_tile(n, want):
    t = min(n, want)
    assert n % t == 0, (n, t)
    return t


def kernel(x, meta, ln_in_g, ln_in_b, w_in, b_f, w_up_a, w_up_b, w_o, ssm_a_re, ssm_a_im, ssm_log_dt,
           ssm_b_re, ssm_b_im, ssm_c_re, ssm_c_im, ssm_d, w_glu, b_glu, ln1_g, ln1_b, w_router, b_router,
           w_gate, b_gate, w_up, b_up, w_down, b_down, ln2_g, ln2_b):
    bsz, seq, dm = x.shape
    depth = w_in.shape[0]
    assert depth == 1 and meta.shape[0] == N_META
    alpha = (2.0 * depth) ** 0.25
    n_groups = ssm_a_re.shape[1]
    n_experts = w_router.shape[-1]
    aw = ATT_WIDTH
    t = bsz * seq
    x2 = x.reshape(t, dm)
    row = lambda a: a.reshape(1, -1)

    f_off = 3 * aw
    u_off = f_off + ATT_HEADS
    w0 = w_in[0]
    w_main = jnp.concatenate([w0[:, :f_off], w0[:, u_off:]], axis=1).astype(BF16)
    w_f = jnp.pad(w0[:, f_off:u_off], ((0, 0), (0, LANES - ATT_HEADS))).astype(BF16)
    b_f_pad = jnp.pad(b_f[0], (0, LANES - ATT_HEADS)).reshape(1, LANES)
    ln_g, ln_b = row(ln_in_g), row(ln_in_b)

    tm = _pick_tile(seq, 512)
    q, k, v, u, ga, gb, fcum = _inproj(x2, ln_g, ln_b, w_main, w_f, b_f_pad, tm=tm, tiles_per_seq=seq // tm)
    _, k_m, v_m, u_m, _, _, fcum_m = _inproj(meta, ln_g, ln_b, w_main, w_f, b_f_pad, tm=N_META, tiles_per_seq=1)

    pad_m = META_KEYS_PAD - N_META
    k_meta = jnp.pad(k_m, ((0, pad_m), (0, 0)))
    v_meta = jnp.pad(v_m, ((0, pad_m), (0, 0)))
    fm = fcum_m[:, :ATT_HEADS]
    fk_meta = jnp.pad((fm - fm[N_META - 1:N_META, :]).T, ((0, 0), (0, pad_m)), constant_values=BIG)
    fk_rows = jnp.transpose(fcum[:, :ATT_HEADS].reshape(bsz, seq, ATT_HEADS), (0, 2, 1))
    tq = _pick_tile(seq, 512)
    f0 = jnp.broadcast_to(fcum[::tq, :ATT_HEADS, None], (t // tq, ATT_HEADS, LANES))
    y_a = _attn(q, k, v, fk_rows, f0, k_meta, v_meta, fk_meta, bsz=bsz, seq=seq, tq=tq)

    chunk = SSM_CHUNK
    n_chunks = seq // chunk
    n_steps = max(1, (n_chunks - 1).bit_length())
    kt, wst, wout, wmeta, apow, d_tiled = _ssm_tables(
        ssm_a_re[0], ssm_a_im[0], ssm_log_dt[0], ssm_b_re[0], ssm_b_im[0], ssm_c_re[0], ssm_c_im[0], ssm_d[0],
        chunk=chunk, n_steps=n_steps)
    u_g = jnp.transpose(u.reshape(bsz * n_chunks, chunk, n_groups, SSM_GROUP), (2, 0, 1, 3))
    u_g = u_g.reshape(n_groups, bsz * n_chunks, chunk * SSM_GROUP)
    um_g = jnp.transpose(u_m.reshape(N_META, n_groups, SSM_GROUP), (1, 0, 2)).reshape(n_groups, 1, N_META * SSM_GROUP)
    um_g = jnp.pad(um_g, ((0, 0), (0, 15), (0, 0)))
    y_g = _ssm(u_g, kt, wst, wout, um_g, wmeta, apow, d_tiled, chunks_per_seq=n_chunks)
    y_b = jnp.transpose(y_g.reshape(n_groups, bsz * n_chunks, chunk, SSM_GROUP), (1, 2, 0, 3)).reshape(t, -1)

    wr = jnp.pad(w_router[0], ((0, 0), (0, LANES - n_experts)))
    wr_hi = wr.astype(BF16)
    wr_lo = (wr - wr_hi.astype(F32)).astype(BF16)
    b_r = jnp.pad(b_router[0], (0, LANES - n_experts)).reshape(1, LANES)
    tm2 = _pick_tile(t, 512)
    h1_rows, ridx, rw, counts = _postmix(
        x2, y_a, y_b, ga, gb, ln_g, ln_b, w_glu[0].astype(BF16), row(b_glu[0]),
        w_up_a[0].astype(BF16), w_up_b[0].astype(BF16), w_o[0].astype(BF16), row(ln1_g[0]), row(ln1_b[0]),
        wr_hi, wr_lo, b_r, tm=tm2, alpha=alpha, n_experts=n_experts)

    tm3 = MOE_ROW_TILE
    n_rows = TOP_K * t + n_experts * tm3
    dest, tile_expert, n_valid, pad_dest, n_pad = _routing_tables(
        ridx, counts, n_experts=n_experts, tm=tm3, n_rows=n_rows)
    xs = _dispatch(dest, pad_dest, n_pad, n_valid, h1_rows, n_rows, tm=tm3)
    ys = _moe(tile_expert, n_valid, xs, w_gate[0], b_gate[0][:, None, :], w_up[0],
              b_up[0][:, None, :], w_down[0], b_down[0][:, None, :], tm=tm3)
    tm4 = _pick_tile(t, 256)
    gather_idx = jnp.transpose(dest.reshape(t // tm4, tm4, TOP_K), (0, 2, 1)).reshape(-1)
    out = _combine(gather_idx, ys, h1_rows, rw, row(ln2_g[0]), row(ln2_b[0]), tm=tm4, alpha=alpha)
    return out.reshape(bsz, seq, dm)
```

```python
import functools
import math

import jax
import jax.numpy as jnp
from jax import lax
from jax.experimental import pallas as pl
from jax.experimental.pallas import tpu as pltpu

F32 = jnp.float32
BF16 = jnp.bfloat16

N_META = 16
ATT_HEADS = 8
HEAD_DIM = 64
ATT_WIDTH = ATT_HEADS * HEAD_DIM
SSM_GROUP = 16
SSM_STATE = 64
TOP_K = 4
SWIGLU_LIMIT = 7.0
SWIGLU_ALPHA = 1.702
LN_EPS = 1e-5

LANES = 128
SSM_CHUNK = 64
META_KEYS_PAD = LANES
BIG = 1e30
LOG2E = 1.4426950408889634


def _layer_norm(x, g, b):
    mu = jnp.mean(x, axis=-1, keepdims=True)
    xc = x - mu
    var = jnp.mean(xc * xc, axis=-1, keepdims=True)
    return xc * lax.rsqrt(var + LN_EPS) * g + b


def _log_sigmoid(z):
    return jnp.minimum(z, 0.0) - jnp.log1p(jnp.exp(-jnp.abs(z)))


def _sigmoid(z):
    return 1.0 / (1.0 + jnp.exp(-z))


def _split3(x):
    hi = x.astype(BF16)
    r1 = x - hi.astype(F32)
    mid = r1.astype(BF16)
    lo = (r1 - mid.astype(F32)).astype(BF16)
    return hi, mid, lo


def _inproj_kernel(x_ref, g_ref, b_ref, w_ref, wf_ref, bf_ref,
                   q_ref, k_ref, v_ref, u_ref, ga_ref, gb_ref, fc_ref, carry_ref,
                   *, tiles_per_seq):
    i = pl.program_id(0)
    tm = x_ref.shape[0]
    h = _layer_norm(x_ref[...], g_ref[...], b_ref[...]).astype(BF16)

    def proj(lo, hi):
        return jnp.dot(h, w_ref[:, lo:hi], preferred_element_type=F32)

    aw = ATT_WIDTH
    q_ref[...] = (proj(0, aw) * (LOG2E / math.sqrt(HEAD_DIM))).astype(BF16)
    k_ref[...] = proj(aw, 2 * aw).astype(BF16)
    v_ref[...] = proj(2 * aw, 3 * aw).astype(BF16)
    u_ref[...] = proj(3 * aw, 4 * aw).astype(BF16)
    dm = ga_ref.shape[1]
    ga_ref[...] = _sigmoid(proj(4 * aw, 4 * aw + dm)).astype(BF16)
    gb_ref[...] = _sigmoid(proj(4 * aw + dm, 4 * aw + 2 * dm)).astype(BF16)

    zf = jnp.dot(h, wf_ref[...], preferred_element_type=F32) + bf_ref[...]
    lf = _log_sigmoid(zf)

    @pl.when(i % tiles_per_seq == 0)
    def _():
        carry_ref[...] = jnp.zeros_like(carry_ref)

    row = lax.broadcasted_iota(jnp.int32, (tm, tm), 0)
    col = lax.broadcasted_iota(jnp.int32, (tm, tm), 1)
    tri = (col <= row).astype(BF16)
    hi, mid, lo = _split3(lf)
    cs = (jnp.dot(tri, hi, preferred_element_type=F32)
          + jnp.dot(tri, mid, preferred_element_type=F32)
          + jnp.dot(tri, lo, preferred_element_type=F32)) + carry_ref[...]
    fc_ref[...] = cs
    carry_ref[...] = cs[tm - 1:tm, :]


def _inproj(x2, ln_g, ln_b, w_main, w_f, b_f, *, tm, tiles_per_seq):
    t, dm = x2.shape
    aw = ATT_WIDTH
    n_main = w_main.shape[1]
    const = lambda i: (0, 0)
    rows = lambda i: (i, 0)
    out_shape = [jax.ShapeDtypeStruct((t, aw), BF16)] * 4 + \
                [jax.ShapeDtypeStruct((t, dm), BF16)] * 2 + \
                [jax.ShapeDtypeStruct((t, LANES), F32)]
    out_specs = [pl.BlockSpec((tm, aw), rows)] * 4 + [pl.BlockSpec((tm, dm), rows)] * 2 + \
                [pl.BlockSpec((tm, LANES), rows)]
    return pl.pallas_call(
        functools.partial(_inproj_kernel, tiles_per_seq=tiles_per_seq),
        grid=(t // tm,),
        in_specs=[pl.BlockSpec((tm, dm), rows),
                  pl.BlockSpec((1, dm), const), pl.BlockSpec((1, dm), const),
                  pl.BlockSpec((dm, n_main), const),
                  pl.BlockSpec((dm, LANES), const), pl.BlockSpec((1, LANES), const)],
        out_specs=out_specs,
        out_shape=out_shape,
        scratch_shapes=[pltpu.VMEM((1, LANES), F32)],
        compiler_params=pltpu.CompilerParams(
            dimension_semantics=("arbitrary",), vmem_limit_bytes=56 * 1024 * 1024),
        name="inproj",
    )(x2, ln_g, ln_b, w_main, w_f, b_f)


def _attn_kernel(qi_tab, ki_tab, q_ref, k_ref, v_ref, fk_ref, f0_ref, km_ref, vm_ref, fkm_ref,
                 o_ref, m_ref, acc_ref):
    p = pl.program_id(1)
    qi = qi_tab[p]
    ki = ki_tab[p]
    tq = q_ref.shape[0]
    lane = lax.broadcasted_iota(jnp.int32, (1, LANES), 1)
    first_head = lane < HEAD_DIM
    f0 = f0_ref[0][:, 0:1]

    def process(k_blk, v_blk, fk_rows, mask):
        n = k_blk.shape[0]
        bias = (fk_rows - f0) * LOG2E
        for hp in range(ATT_HEADS // 2):
            sl = slice(LANES * hp, LANES * (hp + 1))
            q2 = q_ref[:, sl]
            k2 = k_blk[:, sl]
            v2 = v_blk[:, sl]
            for half in range(2):
                h = 2 * hp + half
                sel = first_head if half == 0 else jnp.logical_not(first_head)
                qm = jnp.where(sel, q2, jnp.zeros_like(q2))
                s = lax.dot_general(qm, k2, (((1,), (1,)), ((), ())), preferred_element_type=F32)
                s = s - bias[h:h + 1, :]
                if mask is not None:
                    s = jnp.where(mask, s, -jnp.inf)
                m_old = m_ref[h]
                m_new = jnp.maximum(m_old, jnp.broadcast_to(jnp.max(s, axis=-1, keepdims=True), m_old.shape))
                m_ref[h] = m_new
                pm = jnp.concatenate(
                    [jnp.exp2(s[:, c * LANES:(c + 1) * LANES] - m_new) for c in range(n // LANES)], axis=1)
                v_ones = jnp.where(sel, v2, jnp.ones_like(v2))
                pv = jnp.dot(pm.astype(BF16), v_ones, preferred_element_type=F32)
                acc_ref[h] = jnp.exp2(m_old - m_new) * acc_ref[h] + pv

    @pl.when(ki == 0)
    def _():
        m_ref[...] = jnp.full_like(m_ref, -jnp.inf)
        acc_ref[...] = jnp.zeros_like(acc_ref)
        process(km_ref[...], vm_ref[...], fkm_ref[...], None)

    @pl.when(ki < qi)
    def _():
        process(k_ref[...], v_ref[...], fk_ref[0], None)

    @pl.when(ki == qi)
    def _():
        tk = k_ref.shape[0]
        row = lax.broadcasted_iota(jnp.int32, (tq, tk), 0)
        col = lax.broadcasted_iota(jnp.int32, (tq, tk), 1)
        process(k_ref[...], v_ref[...], fk_ref[0], col <= row)
        for hp in range(ATT_HEADS // 2):
            a0 = acc_ref[2 * hp]
            a1 = acc_ref[2 * hp + 1]
            num = jnp.where(first_head, a0, a1)
            den = jnp.where(first_head, pltpu.roll(a0, HEAD_DIM, axis=1), pltpu.roll(a1, HEAD_DIM, axis=1))
            o_ref[:, LANES * hp:LANES * (hp + 1)] = (num / den).astype(o_ref.dtype)


def _attn(q, k, v, fk_rows, f0, k_meta, v_meta, fk_meta, *, bsz, seq, tq):
    nq = seq // tq
    pairs = [(a, b) for a in range(nq) for b in range(a + 1)]
    qi_tab = jnp.asarray([a for a, _ in pairs], jnp.int32)
    ki_tab = jnp.asarray([b for _, b in pairs], jnp.int32)
    w = q.shape[1]
    qmap = lambda b, p, qt, kt: (b * nq + qt[p], 0)
    kmap = lambda b, p, qt, kt: (b * nq + kt[p], 0)
    const = lambda b, p, qt, kt: (0, 0)
    grid_spec = pltpu.PrefetchScalarGridSpec(
        num_scalar_prefetch=2,
        grid=(bsz, len(pairs)),
        in_specs=[pl.BlockSpec((tq, w), qmap),
                  pl.BlockSpec((tq, w), kmap),
                  pl.BlockSpec((tq, w), kmap),
                  pl.BlockSpec((1, ATT_HEADS, tq), lambda b, p, qt, kt: (b, 0, kt[p])),
                  pl.BlockSpec((1, ATT_HEADS, LANES), lambda b, p, qt, kt: (b * nq + qt[p], 0, 0)),
                  pl.BlockSpec((META_KEYS_PAD, w), const),
                  pl.BlockSpec((META_KEYS_PAD, w), const),
                  pl.BlockSpec((ATT_HEADS, META_KEYS_PAD), const)],
        out_specs=pl.BlockSpec((tq, w), qmap),
        scratch_shapes=[pltpu.VMEM((ATT_HEADS, tq, LANES), F32),
                        pltpu.VMEM((ATT_HEADS, tq, LANES), F32)],
    )
    return pl.pallas_call(
        _attn_kernel,
        grid_spec=grid_spec,
        out_shape=jax.ShapeDtypeStruct((bsz * seq, w), BF16),
        compiler_params=pltpu.CompilerParams(
            dimension_semantics=("arbitrary", "arbitrary"), vmem_limit_bytes=48 * 1024 * 1024),
        name="fox_attention",
    )(qi_tab, ki_tab, q, k, v, fk_rows, f0, k_meta, v_meta, fk_meta)


def _ssm_kernel(u_ref, kt_ref, wst_ref, wout_ref, um_ref, wm_ref, apow_ref, d_ref, y_ref, toep_ref,
                *, chunks_per_seq, n_steps):
    u = u_ref[0]
    rows = u.shape[0]
    p2 = 2 * SSM_STATE
    kt = kt_ref[0]
    n_c, tc = kt.shape
    per_vreg = LANES // n_c
    ext = jnp.concatenate([jnp.zeros_like(kt), kt], axis=1)
    for sub in range(per_vreg):
        rot = ext if sub == 0 else pltpu.roll(ext, n_c * sub, axis=1)
        for whole in range(tc // LANES):
            tp = per_vreg * whole + sub
            toep_ref[tp * n_c:(tp + 1) * n_c, :] = rot[:, tc - LANES * whole:2 * tc - LANES * whole].astype(BF16)
    y = jnp.dot(u, toep_ref[...], preferred_element_type=F32)
    s = jnp.dot(u, wst_ref[0], preferred_element_type=F32)
    x0 = jnp.dot(um_ref[0], wm_ref[0], preferred_element_type=F32)[0:1, :]

    lane = lax.broadcasted_iota(jnp.int32, (1, p2), 1)
    re_half = lane < SSM_STATE
    j = lax.broadcasted_iota(jnp.int32, (rows, 1), 0) % chunks_per_seq

    def cmul(step, z):
        ar = apow_ref[0, step:step + 1, :]
        ai = apow_ref[0, n_steps + step:n_steps + step + 1, :]
        return ar * z + ai * pltpu.roll(z, SSM_STATE, axis=1)

    s = s + jnp.where(j == 0, cmul(0, jnp.broadcast_to(x0, s.shape)), 0.0)
    for step in range(n_steps):
        sh = 1 << step
        prev = pltpu.roll(s, sh, axis=0)
        s = s + jnp.where(j >= sh, cmul(step, prev), 0.0)
    x_in = jnp.where(j == 0, x0, pltpu.roll(s, 1, axis=0))
    del re_half
    y = y + jnp.dot(x_in.astype(BF16), wout_ref[0], preferred_element_type=F32)
    y = y + d_ref[0] * u.astype(F32)
    y_ref[0] = jax.nn.gelu(y).astype(y_ref.dtype)


def _ssm(u_g, kt, wst, wout, u_meta, wmeta, apow, d_tiled, *, chunks_per_seq):
    g, rows, tc = u_g.shape
    n_steps = apow.shape[1] // 2
    grp = lambda i: (i, 0, 0)
    return pl.pallas_call(
        functools.partial(_ssm_kernel, chunks_per_seq=chunks_per_seq, n_steps=n_steps),
        grid=(g,),
        in_specs=[pl.BlockSpec((1, rows, tc), grp),
                  pl.BlockSpec((1,) + kt.shape[1:], grp),
                  pl.BlockSpec((1, tc, 2 * SSM_STATE), grp),
                  pl.BlockSpec((1, 2 * SSM_STATE, tc), grp),
                  pl.BlockSpec((1,) + u_meta.shape[1:], grp),
                  pl.BlockSpec((1,) + wmeta.shape[1:], grp),
                  pl.BlockSpec((1,) + apow.shape[1:], grp),
                  pl.BlockSpec((1, 1, tc), grp)],
        out_specs=pl.BlockSpec((1, rows, tc), grp),
        out_shape=jax.ShapeDtypeStruct((g, rows, tc), BF16),
        scratch_shapes=[pltpu.VMEM((tc, tc), BF16)],
        compiler_params=pltpu.CompilerParams(
            dimension_semantics=("arbitrary",), vmem_limit_bytes=48 * 1024 * 1024),
        name="s5_ssm",
    )(u_g, kt, wst, wout, u_meta, wmeta, apow, d_tiled)


def _ssm_tables(a_re, a_im, log_dt, b_re, b_im, c_re, c_im, d_skip, *, chunk, n_steps):
    g, p = a_re.shape
    c = b_re.shape[-1]
    dt = jnp.exp(log_dt)[:, None]
    mag = jnp.exp(a_re * dt)
    ang = a_im * dt
    lb_re, lb_im = mag * jnp.cos(ang), mag * jnp.sin(ang)
    den = a_re * a_re + a_im * a_im
    z_re, z_im = lb_re - 1.0, lb_im
    coef_re = (z_re * a_re + z_im * a_im) / den
    coef_im = (z_im * a_re - z_re * a_im) / den
    bb_re = coef_re[..., None] * b_re - coef_im[..., None] * b_im
    bb_im = coef_re[..., None] * b_im + coef_im[..., None] * b_re

    def step(carry, _):
        cr, ci = carry
        return (cr * lb_re - ci * lb_im, cr * lb_im + ci * lb_re), (cr, ci)
    _, (pw_re, pw_im) = lax.scan(step, (jnp.ones_like(lb_re), jnp.zeros_like(lb_re)), None,
                                 length=chunk + 1)

    e_re = c_re[None] * pw_re[:, :, None, :] - c_im[None] * pw_im[:, :, None, :]
    e_im = c_re[None] * pw_im[:, :, None, :] + c_im[None] * pw_re[:, :, None, :]
    kern = (jnp.einsum('tgcp,gpd->tgcd', e_re[:chunk], bb_re)
            - jnp.einsum('tgcp,gpd->tgcd', e_im[:chunk], bb_im))
    kt = jnp.transpose(kern, (1, 3, 0, 2)).reshape(g, c, chunk * c)

    def in_to_state(n):
        wr = pw_re[n - 1::-1][:n, :, :, None] * bb_re[None] - pw_im[n - 1::-1][:n, :, :, None] * bb_im[None]
        wi = pw_re[n - 1::-1][:n, :, :, None] * bb_im[None] + pw_im[n - 1::-1][:n, :, :, None] * bb_re[None]
        w = jnp.concatenate([wr, wi], axis=2)
        return jnp.transpose(w, (1, 0, 3, 2)).reshape(g, n * c, 2 * p)
    wst = in_to_state(chunk)
    wmeta = in_to_state(N_META)

    wout = jnp.concatenate([e_re[1:chunk + 1], -e_im[1:chunk + 1]], axis=3)
    wout = jnp.transpose(wout, (1, 3, 0, 2)).reshape(g, 2 * p, chunk * c)

    ar, ai = pw_re[chunk], pw_im[chunk]
    rows_r, rows_i = [], []
    for _ in range(n_steps):
        rows_r.append(jnp.concatenate([ar, ar], axis=-1))
        rows_i.append(jnp.concatenate([-ai, ai], axis=-1))
        ar, ai = ar * ar - ai * ai, 2.0 * ar * ai
    apow = jnp.stack(rows_r + rows_i, axis=1)
    d_tiled = jnp.tile(d_skip, (1, chunk))[:, None, :]
    return kt, wst.astype(BF16), wout.astype(BF16), wmeta.astype(BF16), apow, d_tiled


def _postmix_kernel(x_ref, ya_ref, yb_ref, ga_ref, gb_ref, lng_ref, lnb_ref,
                    wglu_ref, bglu_ref, wa_ref, wb_ref, wo_ref, l1g_ref, l1b_ref,
                    wrh_ref, wrl_ref, br_ref, h1_ref, ridx_ref, rw_ref, cnt_ref, carry_ref,
                    *, alpha, n_experts):
    i = pl.program_id(0)
    tm = x_ref.shape[0]
    h0 = _layer_norm(x_ref[...], lng_ref[...], lnb_ref[...])
    yb = yb_ref[...]
    glu = yb.astype(F32) * _sigmoid(jnp.dot(yb, wglu_ref[...], preferred_element_type=F32) + bglu_ref[...])
    merged = (ga_ref[...].astype(F32) * jnp.dot(ya_ref[...], wa_ref[...], preferred_element_type=F32)
              + gb_ref[...].astype(F32) * jnp.dot(glu.astype(BF16), wb_ref[...], preferred_element_type=F32))
    mix = jnp.dot(merged.astype(BF16), wo_ref[...], preferred_element_type=F32)
    h1 = _layer_norm(alpha * h0 + mix, l1g_ref[...], l1b_ref[...])
    h1_ref[...] = h1.reshape(h1_ref.shape)
    hb = h1.astype(BF16)

    hl = (h1 - hb.astype(F32)).astype(BF16)
    logits = (jnp.dot(hb, wrh_ref[...], preferred_element_type=F32)
              + jnp.dot(hb, wrl_ref[...], preferred_element_type=F32)
              + jnp.dot(hl, wrh_ref[...], preferred_element_type=F32)) + br_ref[...]
    lane = lax.broadcasted_iota(jnp.int32, logits.shape, 1)
    logits = jnp.where(lane < n_experts, logits, -jnp.inf)
    denom = jnp.zeros((tm, 1), F32)
    top = None
    hits, idxs, ws = [], [], []
    for _ in range(TOP_K):
        mx = jnp.max(logits, axis=-1, keepdims=True)
        idx = jnp.min(jnp.where(logits == mx, lane, LANES), axis=-1, keepdims=True)
        hit = lane == idx
        if top is None:
            top = mx
        w = jnp.exp(mx - top)
        denom = denom + w
        hits.append(hit)
        idxs.append(idx)
        ws.append(w)
        logits = jnp.where(hit, -jnp.inf, logits)

    @pl.when(i == 0)
    def _():
        carry_ref[...] = jnp.zeros_like(carry_ref)

    sel = jnp.zeros(logits.shape, F32)
    for hit in hits:
        sel = sel + jnp.where(hit, 1.0, 0.0)
    row = lax.broadcasted_iota(jnp.int32, (tm, tm), 0)
    col = lax.broadcasted_iota(jnp.int32, (tm, tm), 1)
    before = (col < row).astype(BF16)
    seen = jnp.dot(before, sel.astype(BF16), preferred_element_type=F32) + carry_ref[...]
    ridx = jnp.zeros(logits.shape, jnp.int32)
    rw = jnp.zeros(logits.shape, F32)
    for kk in range(TOP_K):
        rank = jnp.sum(jnp.where(hits[kk], seen, 0.0), axis=-1, keepdims=True).astype(jnp.int32)
        ridx = jnp.where(lane == kk, idxs[kk], ridx)
        ridx = jnp.where(lane == TOP_K + kk, rank, ridx)
        rw = jnp.where(lane == kk, ws[kk] / denom, rw)
    ridx_ref[...] = ridx
    rw_ref[...] = rw
    total = carry_ref[...] + jnp.sum(sel, axis=0, keepdims=True)
    carry_ref[...] = total
    cnt_ref[...] = jnp.broadcast_to(total, cnt_ref.shape)


def _postmix(x2, ya, yb, ga, gb, ln_g, ln_b, w_glu, b_glu, w_a, w_b, w_o, l1g, l1b,
             wr_hi, wr_lo, b_r, *, tm, alpha, n_experts):
    t, dm = x2.shape
    sw = ya.shape[1]
    rows = lambda i: (i, 0)
    const = lambda i: (0, 0)
    full = lambda a: pl.BlockSpec(a.shape, const)
    return pl.pallas_call(
        functools.partial(_postmix_kernel, alpha=alpha, n_experts=n_experts),
        grid=(t // tm,),
        in_specs=[pl.BlockSpec((tm, dm), rows), pl.BlockSpec((tm, sw), rows), pl.BlockSpec((tm, sw), rows),
                  pl.BlockSpec((tm, dm), rows), pl.BlockSpec((tm, dm), rows),
                  full(ln_g), full(ln_b), full(w_glu), full(b_glu), full(w_a), full(w_b), full(w_o),
                  full(l1g), full(l1b), full(wr_hi), full(wr_lo), full(b_r)],
        out_specs=[pl.BlockSpec((tm, 1, dm), lambda i: (i, 0, 0)), pl.BlockSpec((tm, LANES), rows),
                   pl.BlockSpec((tm, LANES), rows), pl.BlockSpec((8, LANES), const)],
        out_shape=[jax.ShapeDtypeStruct((t, 1, dm), F32), jax.ShapeDtypeStruct((t, LANES), jnp.int32),
                   jax.ShapeDtypeStruct((t, LANES), F32), jax.ShapeDtypeStruct((8, LANES), F32)],
        scratch_shapes=[pltpu.VMEM((1, LANES), F32)],
        compiler_params=pltpu.CompilerParams(
            dimension_semantics=("arbitrary",), vmem_limit_bytes=48 * 1024 * 1024),
        name="postmix_router",
    )(x2, ya, yb, ga, gb, ln_g, ln_b, w_glu, b_glu, w_a, w_b, w_o, l1g, l1b, wr_hi, wr_lo, b_r)


MOE_ROW_TILE = 512


def _moe_kernel(te_ref, nv_ref, inv_ref, h_hbm, wg_ref, bg_ref, wu_ref, bu_ref, wd_ref, bd_ref, o_hbm,
                xbuf_ref, ybuf_ref, x2_ref, wgb_ref, wub_ref, wdb_ref, sem_g, sem_s, *, n_tokens):
    i = pl.program_id(0)
    n = pl.num_programs(0)
    _, tm, _, dm = xbuf_ref.shape
    cur = i % 2
    oth = (i + 1) % 2

    def start_gather(tile, slot):
        base = tile * tm

        def body(r, c):
            s = inv_ref[base + r]
            tok = s & (n_tokens - 1) if n_tokens & (n_tokens - 1) == 0 else s % n_tokens
            pltpu.make_async_copy(h_hbm.at[tok], xbuf_ref.at[slot, r], sem_g.at[slot]).start()
            return c
        lax.fori_loop(0, tm, body, 0, unroll=8)

    def wait_gather(slot):
        pltpu.make_async_copy(h_hbm.at[pl.ds(0, tm)], xbuf_ref.at[slot], sem_g.at[slot]).wait()

    def start_scatter(tile, slot):
        base = tile * tm

        def body(r, c):
            pltpu.make_async_copy(ybuf_ref.at[slot, r], o_hbm.at[inv_ref[base + r]], sem_s.at[slot]).start()
            return c
        lax.fori_loop(0, tm, body, 0, unroll=8)

    def wait_scatter(slot):
        pltpu.make_async_copy(ybuf_ref.at[slot], o_hbm.at[pl.ds(0, tm)], sem_s.at[slot]).wait()

    @pl.when(i == 0)
    def _():
        ybuf_ref[...] = jnp.zeros_like(ybuf_ref)
        start_gather(0, 0)

    @pl.when(i >= 1)
    def _():
        wait_scatter(cur)

    valid = i < nv_ref[0]
    new_expert = jnp.logical_or(i == 0, te_ref[i] != te_ref[jnp.maximum(i - 1, 0)])

    @pl.when(jnp.logical_and(valid, new_expert))
    def _():
        wgb_ref[...] = wg_ref[0].astype(BF16)
        wub_ref[...] = wu_ref[0].astype(BF16)
        wdb_ref[...] = wd_ref[0].astype(BF16)

    def row_copies():
        wait_gather(cur)
        start_gather(jnp.minimum(i + 1, n - 1), oth)
        start_scatter(jnp.maximum(i - 1, 0), oth)

    @pl.when(valid)
    def _():
        row_copies()
        x2_ref[...] = xbuf_ref[cur].reshape(x2_ref.shape)
        x = x2_ref[...].astype(BF16)
        gate = jnp.minimum(jnp.dot(x, wgb_ref[...], preferred_element_type=F32) + bg_ref[0], SWIGLU_LIMIT)
        up = jnp.clip(jnp.dot(x, wub_ref[...], preferred_element_type=F32) + bu_ref[0],
                      -SWIGLU_LIMIT, SWIGLU_LIMIT)
        act = (up + 1.0) * gate * _sigmoid(SWIGLU_ALPHA * gate)
        y = jnp.dot(act.astype(BF16), wdb_ref[...], preferred_element_type=F32) + bd_ref[0]
        ybuf_ref[cur] = y.reshape(tm, 1, dm)

    @pl.when(jnp.logical_not(valid))
    def _():
        row_copies()

    @pl.when(i == n - 1)
    def _():
        wait_gather(oth)
        wait_scatter(oth)
        start_scatter(i, cur)
        wait_scatter(cur)


def _moe(tile_expert, n_valid, inv, h1_rows, w_gate, b_gate, w_up, b_up, w_down, b_down, *, tm):
    t, _, dm = h1_rows.shape
    _, _, de = w_gate.shape
    n_rows = inv.shape[0]
    per_e = lambda i, te, nv, iv: (te[i], 0, 0)
    return pl.pallas_call(
        functools.partial(_moe_kernel, n_tokens=t),
        grid_spec=pltpu.PrefetchScalarGridSpec(
            num_scalar_prefetch=3, grid=(n_rows // tm,),
            in_specs=[pl.BlockSpec(memory_space=pl.ANY),
                      pl.BlockSpec((1, dm, de), per_e), pl.BlockSpec((1, 1, de), per_e),
                      pl.BlockSpec((1, dm, de), per_e), pl.BlockSpec((1, 1, de), per_e),
                      pl.BlockSpec((1, de, dm), per_e), pl.BlockSpec((1, 1, dm), per_e)],
            out_specs=pl.BlockSpec(memory_space=pl.ANY),
            scratch_shapes=[pltpu.VMEM((2, tm, 1, dm), F32), pltpu.VMEM((2, tm, 1, dm), F32),
                            pltpu.VMEM((tm, dm), F32), pltpu.VMEM((dm, de), BF16),
                            pltpu.VMEM((dm, de), BF16), pltpu.VMEM((de, dm), BF16),
                            pltpu.SemaphoreType.DMA((2,)), pltpu.SemaphoreType.DMA((2,))]),
        out_shape=jax.ShapeDtypeStruct((n_rows, 1, dm), F32),
        compiler_params=pltpu.CompilerParams(
            dimension_semantics=("arbitrary",), vmem_limit_bytes=56 * 1024 * 1024),
        name="moe_experts",
    )(tile_expert, n_valid, inv, h1_rows, w_gate, b_gate, w_up, b_up, w_down, b_down)


def _combine_kernel(*refs, alpha):
    y_refs = refs[:TOP_K]
    h1_ref, rw_ref, l2g_ref, l2b_ref, o_ref, rows2_ref = refs[TOP_K:]
    tm, dm = o_ref.shape
    rw = rw_ref[...]
    ffn = jnp.zeros((tm, dm), F32)
    for kk in range(TOP_K):
        rows2_ref[...] = y_refs[kk][...].reshape(tm, dm)
        ffn = ffn + rw[:, kk:kk + 1] * rows2_ref[...]
    rows2_ref[...] = h1_ref[...].reshape(tm, dm)
    o_ref[...] = _layer_norm(alpha * rows2_ref[...] + ffn, l2g_ref[...], l2b_ref[...])


def _combine(y_slots, h1_rows, rw, l2g, l2b, *, tm, alpha):
    t, _, dm = h1_rows.shape
    rows = lambda i: (i, 0)
    const = lambda i: (0, 0)
    slot_specs = [pl.BlockSpec((tm, 1, dm), functools.partial(lambda i, kk: (kk * (t // tm) + i, 0, 0), kk=kk))
                  for kk in range(TOP_K)]
    return pl.pallas_call(
        functools.partial(_combine_kernel, alpha=alpha),
        grid=(t // tm,),
        in_specs=slot_specs + [pl.BlockSpec((tm, 1, dm), lambda i: (i, 0, 0)),
                               pl.BlockSpec((tm, LANES), rows),
                               pl.BlockSpec((1, dm), const), pl.BlockSpec((1, dm), const)],
        out_specs=pl.BlockSpec((tm, dm), rows),
        out_shape=jax.ShapeDtypeStruct((t, dm), F32),
        scratch_shapes=[pltpu.VMEM((tm, dm), F32)],
        compiler_params=pltpu.CompilerParams(
            dimension_semantics=("arbitrary",), vmem_limit_bytes=48 * 1024 * 1024),
        name="moe_combine_ln2",
    )(*([y_slots] * TOP_K), h1_rows, rw, l2g, l2b)


INVERT_CHUNK = 8192


def _invert_kernel(rows_ref, slots_ref, inv_ref):
    def body(p, c):
        inv_ref[rows_ref[p]] = slots_ref[p]
        return c
    lax.fori_loop(0, rows_ref.shape[0], body, 0, unroll=8)


def _invert(rows, slots):
    n = rows.shape[0]
    ch = math.gcd(n, INVERT_CHUNK)
    chunk = pl.BlockSpec((ch,), lambda i: (i,), memory_space=pltpu.SMEM)
    return pl.pallas_call(
        _invert_kernel,
        grid=(n // ch,),
        in_specs=[chunk, chunk],
        out_specs=pl.BlockSpec(memory_space=pltpu.SMEM),
        out_shape=jax.ShapeDtypeStruct((n,), jnp.int32),
        compiler_params=pltpu.CompilerParams(dimension_semantics=("arbitrary",)),
        name="moe_row_slots",
    )(rows, slots)


def _routing_tables(ridx, counts, *, n_experts, tm):
    t = ridx.shape[0]
    n_real = TOP_K * t
    n_fill = n_experts * tm
    e_idx = ridx[:, :TOP_K]
    rank = ridx[:, TOP_K:2 * TOP_K]
    cnt = counts[0, :n_experts].astype(jnp.int32)
    padded = (cnt + tm - 1) // tm * tm
    ends = jnp.cumsum(padded)
    offs = ends - padded
    experts = jnp.arange(n_experts, dtype=jnp.int32)
    base = jnp.sum(jnp.where(e_idx[..., None] == experts, offs, 0), axis=-1)
    token_rows = (base + rank).reshape(-1)
    token_slots = (jnp.arange(TOP_K, dtype=jnp.int32)[None, :] * t
                   + jnp.arange(t, dtype=jnp.int32)[:, None]).reshape(-1)
    fill_cnt = padded - cnt
    fill_end = jnp.cumsum(fill_cnt)
    shift = jnp.concatenate([offs + cnt - (fill_end - fill_cnt), jnp.full((1,), n_real, jnp.int32)])
    j = jnp.arange(n_fill, dtype=jnp.int32)
    fill_rows = j + shift[0] + jnp.sum(
        jnp.where(j[:, None] >= fill_end[None, :], (shift[1:] - shift[:-1])[None, :], 0), axis=1)
    rows = jnp.concatenate([token_rows, fill_rows])
    slots = jnp.concatenate([token_slots, n_real + j])
    inv = _invert(rows, slots)
    n_tiles = (n_real + n_fill) // tm
    tile_start = jnp.arange(n_tiles, dtype=jnp.int32) * tm
    tile_expert = jnp.minimum(jnp.sum(ends[None, :] <= tile_start[:, None], axis=1), n_experts - 1)
    n_valid = (ends[-1] // tm).reshape(1)
    return inv.astype(jnp.int32), tile_expert.astype(jnp.int32), n_valid.astype(jnp.int32)


def _pick_tile(n, want):
    t = min(n, want)
    assert n % t == 0, (n, t)
    return t


def kernel(x, meta, ln_in_g, ln_in_b, w_in, b_f, w_up_a, w_up_b, w_o, ssm_a_re, ssm_a_im, ssm_log_dt,
           ssm_b_re, ssm_b_im, ssm_c_re, ssm_c_im, ssm_d, w_glu, b_glu, ln1_g, ln1_b, w_router, b_router,
           w_gate, b_gate, w_up, b_up, w_down, b_down, ln2_g, ln2_b):
    bsz, seq, dm = x.shape
    depth = w_in.shape[0]
    assert depth == 1 and meta.shape[0] == N_META
    alpha = (2.0 * depth) ** 0.25
    n_groups = ssm_a_re.shape[1]
    n_experts = w_router.shape[-1]
    aw = ATT_WIDTH
    t = bsz * seq
    x2 = x.reshape(t, dm)
    row = lambda a: a.reshape(1, -1)

    f_off = 3 * aw
    u_off = f_off + ATT_HEADS
    w0 = w_in[0]
    w_main = jnp.concatenate([w0[:, :f_off], w0[:, u_off:]], axis=1).astype(BF16)
    w_f = jnp.pad(w0[:, f_off:u_off], ((0, 0), (0, LANES - ATT_HEADS))).astype(BF16)
    b_f_pad = jnp.pad(b_f[0], (0, LANES - ATT_HEADS)).reshape(1, LANES)
    ln_g, ln_b = row(ln_in_g), row(ln_in_b)

    tm = _pick_tile(seq, 512)
    q, k, v, u, ga, gb, fcum = _inproj(x2, ln_g, ln_b, w_main, w_f, b_f_pad, tm=tm, tiles_per_seq=seq // tm)
    _, k_m, v_m, u_m, _, _, fcum_m = _inproj(meta, ln_g, ln_b, w_main, w_f, b_f_pad, tm=N_META, tiles_per_seq=1)

    pad_m = META_KEYS_PAD - N_META
    k_meta = jnp.pad(k_m, ((0, pad_m), (0, 0)))
    v_meta = jnp.pad(v_m, ((0, pad_m), (0, 0)))
    fm = fcum_m[:, :ATT_HEADS]
    fk_meta = jnp.pad((fm - fm[N_META - 1:N_META, :]).T, ((0, 0), (0, pad_m)), constant_values=BIG)
    fk_rows = jnp.transpose(fcum[:, :ATT_HEADS].reshape(bsz, seq, ATT_HEADS), (0, 2, 1))
    tq = _pick_tile(seq, 512)
    f0 = jnp.broadcast_to(fcum[::tq, :ATT_HEADS, None], (t // tq, ATT_HEADS, LANES))
    y_a = _attn(q, k, v, fk_rows, f0, k_meta, v_meta, fk_meta, bsz=bsz, seq=seq, tq=tq)

    chunk = SSM_CHUNK
    n_chunks = seq // chunk
    n_steps = max(1, (n_chunks - 1).bit_length())
    kt, wst, wout, wmeta, apow, d_tiled = _ssm_tables(
        ssm_a_re[0], ssm_a_im[0], ssm_log_dt[0], ssm_b_re[0], ssm_b_im[0], ssm_c_re[0], ssm_c_im[0], ssm_d[0],
        chunk=chunk, n_steps=n_steps)
    u_g = jnp.transpose(u.reshape(bsz * n_chunks, chunk, n_groups, SSM_GROUP), (2, 0, 1, 3))
    u_g = u_g.reshape(n_groups, bsz * n_chunks, chunk * SSM_GROUP)
    um_g = jnp.transpose(u_m.reshape(N_META, n_groups, SSM_GROUP), (1, 0, 2)).reshape(n_groups, 1, N_META * SSM_GROUP)
    um_g = jnp.pad(um_g, ((0, 0), (0, 15), (0, 0)))
    y_g = _ssm(u_g, kt, wst, wout, um_g, wmeta, apow, d_tiled, chunks_per_seq=n_chunks)
    y_b = jnp.transpose(y_g.reshape(n_groups, bsz * n_chunks, chunk, SSM_GROUP), (1, 2, 0, 3)).reshape(t, -1)

    wr = jnp.pad(w_router[0], ((0, 0), (0, LANES - n_experts)))
    wr_hi = wr.astype(BF16)
    wr_lo = (wr - wr_hi.astype(F32)).astype(BF16)
    b_r = jnp.pad(b_router[0], (0, LANES - n_experts)).reshape(1, LANES)
    tm2 = _pick_tile(t, 512)
    h1_rows, ridx, rw, counts = _postmix(
        x2, y_a, y_b, ga, gb, ln_g, ln_b, w_glu[0].astype(BF16), row(b_glu[0]),
        w_up_a[0].astype(BF16), w_up_b[0].astype(BF16), w_o[0].astype(BF16), row(ln1_g[0]), row(ln1_b[0]),
        wr_hi, wr_lo, b_r, tm=tm2, alpha=alpha, n_experts=n_experts)

    tm3 = MOE_ROW_TILE
    inv, tile_expert, n_valid = _routing_tables(ridx, counts, n_experts=n_experts, tm=tm3)
    y_slots = _moe(tile_expert, n_valid, inv, h1_rows, w_gate[0], b_gate[0][:, None, :], w_up[0],
                   b_up[0][:, None, :], w_down[0], b_down[0][:, None, :], tm=tm3)
    tm4 = _pick_tile(t, 256)
    out = _combine(y_slots, h1_rows, rw, row(ln2_g[0]), row(ln2_b[0]), tm=tm4, alpha=alpha)
    return out.reshape(bsz, seq, dm)
```

```python
import functools
import math

import jax
import jax.numpy as jnp
from jax import lax
from jax.experimental import pallas as pl
from jax.experimental.pallas import tpu as pltpu

F32 = jnp.float32
BF16 = jnp.bfloat16

N_META = 16
ATT_HEADS = 8
HEAD_DIM = 64
ATT_WIDTH = ATT_HEADS * HEAD_DIM
SSM_GROUP = 16
SSM_STATE = 64
TOP_K = 4
SWIGLU_LIMIT = 7.0
SWIGLU_ALPHA = 1.702
LN_EPS = 1e-5

LANES = 128
SSM_CHUNK = 64
META_KEYS_PAD = LANES
BIG = 1e30
LOG2E = 1.4426950408889634


def _layer_norm(x, g, b):
    mu = jnp.mean(x, axis=-1, keepdims=True)
    xc = x - mu
    var = jnp.mean(xc * xc, axis=-1, keepdims=True)
    return xc * lax.rsqrt(var + LN_EPS) * g + b


def _log_sigmoid(z):
    return jnp.minimum(z, 0.0) - jnp.log1p(jnp.exp(-jnp.abs(z)))


def _sigmoid(z):
    return 1.0 / (1.0 + jnp.exp(-z))


def _split3(x):
    hi = x.astype(BF16)
    r1 = x - hi.astype(F32)
    mid = r1.astype(BF16)
    lo = (r1 - mid.astype(F32)).astype(BF16)
    return hi, mid, lo


def _inproj_kernel(x_ref, g_ref, b_ref, w_ref, wf_ref, bf_ref,
                   q_ref, k_ref, v_ref, u_ref, ga_ref, gb_ref, fc_ref, carry_ref,
                   *, tiles_per_seq):
    i = pl.program_id(0)
    tm = x_ref.shape[0]
    h = _layer_norm(x_ref[...], g_ref[...], b_ref[...]).astype(BF16)

    def proj(lo, hi):
        return jnp.dot(h, w_ref[:, lo:hi], preferred_element_type=F32)

    aw = ATT_WIDTH
    u_ref[...] = proj(3 * aw, 4 * aw).astype(BF16)
    dm = ga_ref.shape[1]
    ga_ref[...] = _sigmoid(proj(4 * aw, 4 * aw + dm)).astype(BF16)
    gb_ref[...] = _sigmoid(proj(4 * aw + dm, 4 * aw + 2 * dm)).astype(BF16)

    zf = jnp.dot(h, wf_ref[...], preferred_element_type=F32) + bf_ref[...]
    lf = _log_sigmoid(zf)

    @pl.when(i % tiles_per_seq == 0)
    def _():
        carry_ref[...] = jnp.zeros_like(carry_ref)

    row = lax.broadcasted_iota(jnp.int32, (tm, tm), 0)
    col = lax.broadcasted_iota(jnp.int32, (tm, tm), 1)
    tri = (col <= row).astype(BF16)
    hi, mid, lo = _split3(lf)
    cs = (jnp.dot(tri, hi, preferred_element_type=F32)
          + jnp.dot(tri, mid, preferred_element_type=F32)
          + jnp.dot(tri, lo, preferred_element_type=F32)) + carry_ref[...]
    fc_ref[...] = cs
    carry_ref[...] = cs[tm - 1:tm, :]

    lane = lax.broadcasted_iota(jnp.int32, (1, LANES), 1)
    own = lane < HEAD_DIM
    bias_lanes = jnp.logical_and(lane >= HEAD_DIM, lane < HEAD_DIM + 3)
    pieces = [p.astype(F32) for p in _split3(-LOG2E * cs)]

    def slabs(x, extra):
        out = []
        for hp in range(ATT_HEADS // 2):
            x2 = x[:, LANES * hp:LANES * (hp + 1)]
            out.append(jnp.where(own, x2, extra(2 * hp)))
            out.append(jnp.where(own, pltpu.roll(x2, HEAD_DIM, axis=1), extra(2 * hp + 1)))
        return jnp.concatenate(out, axis=1).astype(BF16)

    def bias_extra(h):
        e = jnp.zeros((tm, LANES), F32)
        for j, piece in enumerate(pieces):
            e = jnp.where(lane == HEAD_DIM + j, pltpu.roll(piece, (HEAD_DIM + j - h) % LANES, axis=1), e)
        return e

    q_ref[...] = slabs(proj(0, aw) * (LOG2E / math.sqrt(HEAD_DIM)), lambda h: jnp.where(bias_lanes, 1.0, 0.0))
    k_ref[...] = slabs(proj(aw, 2 * aw), bias_extra)
    v_ref[...] = slabs(proj(2 * aw, 3 * aw), lambda h: jnp.ones((1, LANES), F32))


def _inproj(x2, ln_g, ln_b, w_main, w_f, b_f, *, tm, tiles_per_seq):
    t, dm = x2.shape
    aw = ATT_WIDTH
    n_main = w_main.shape[1]
    const = lambda i: (0, 0)
    rows = lambda i: (i, 0)
    slab = ATT_HEADS * LANES
    out_shape = [jax.ShapeDtypeStruct((t, slab), BF16)] * 3 + [jax.ShapeDtypeStruct((t, aw), BF16)] + \
                [jax.ShapeDtypeStruct((t, dm), BF16)] * 2 + \
                [jax.ShapeDtypeStruct((t, LANES), F32)]
    out_specs = [pl.BlockSpec((tm, slab), rows)] * 3 + [pl.BlockSpec((tm, aw), rows)] + \
                [pl.BlockSpec((tm, dm), rows)] * 2 + [pl.BlockSpec((tm, LANES), rows)]
    return pl.pallas_call(
        functools.partial(_inproj_kernel, tiles_per_seq=tiles_per_seq),
        grid=(t // tm,),
        in_specs=[pl.BlockSpec((tm, dm), rows),
                  pl.BlockSpec((1, dm), const), pl.BlockSpec((1, dm), const),
                  pl.BlockSpec((dm, n_main), const),
                  pl.BlockSpec((dm, LANES), const), pl.BlockSpec((1, LANES), const)],
        out_specs=out_specs,
        out_shape=out_shape,
        scratch_shapes=[pltpu.VMEM((1, LANES), F32)],
        compiler_params=pltpu.CompilerParams(
            dimension_semantics=("arbitrary",), vmem_limit_bytes=56 * 1024 * 1024),
        name="inproj",
    )(x2, ln_g, ln_b, w_main, w_f, b_f)


def _attn_kernel(qi_tab, ki_tab, qt_ref, k_ref, vt_ref, km_ref, vmt_ref, o_ref, m_ref, acc_ref, st_ref):
    p = pl.program_id(1)
    qi = qi_tab[p]
    ki = ki_tab[p]
    tq = qt_ref.shape[1]

    def process(k_blk, vt_blk, mask):
        n = k_blk.shape[0]

        def logits(h):
            sl = slice(LANES * h, LANES * (h + 1))
            st = jnp.dot(k_blk[:, sl], qt_ref[sl, :], preferred_element_type=F32)
            if mask is not None:
                st = jnp.where(mask, st, -jnp.inf)
            st_ref[h % 2, 0:n] = st
            m_old = m_ref[h]
            m_new = jnp.maximum(m_old, jnp.max(st_ref[h % 2, 0:n], axis=0, keepdims=True))
            m_ref[h] = m_new
            return m_old, m_new

        def weigh(h, m_old, m_new):
            sl = slice(LANES * h, LANES * (h + 1))
            pt = jnp.exp2(st_ref[h % 2, 0:n] - m_new[0:1, :])
            alpha = jnp.exp2(m_old - m_new)
            acc_ref[h] = alpha[0:1, :] * acc_ref[h] + jnp.dot(
                vt_blk[sl, :], pt.astype(BF16), preferred_element_type=F32)

        ms = logits(0)
        for h in range(ATT_HEADS):
            nxt = logits(h + 1) if h + 1 < ATT_HEADS else None
            weigh(h, *ms)
            ms = nxt

    @pl.when(ki == 0)
    def _():
        m_ref[...] = jnp.full_like(m_ref, -jnp.inf)
        acc_ref[...] = jnp.zeros_like(acc_ref)
        process(km_ref[...], vmt_ref[...], None)

    @pl.when(ki < qi)
    def _():
        process(k_ref[...], vt_ref[...], None)

    @pl.when(ki == qi)
    def _():
        tk = k_ref.shape[0]
        key = lax.broadcasted_iota(jnp.int32, (tk, tq), 0)
        qry = lax.broadcasted_iota(jnp.int32, (tk, tq), 1)
        process(k_ref[...], vt_ref[...], key <= qry)
        for h in range(ATT_HEADS):
            a = acc_ref[h]
            o_ref[HEAD_DIM * h:HEAD_DIM * (h + 1), :] = (a[:HEAD_DIM] / a[HEAD_DIM:]).astype(o_ref.dtype)


def _attn(q_t, k, v_t, k_meta, v_meta_t, *, bsz, seq, tq):
    nq = seq // tq
    pairs = [(a, b) for a in range(nq) for b in range(a + 1)]
    qi_tab = jnp.asarray([a for a, _ in pairs], jnp.int32)
    ki_tab = jnp.asarray([b for _, b in pairs], jnp.int32)
    w = k.shape[1]
    qcol = lambda b, p, qt, kt: (0, b * nq + qt[p])
    kcol = lambda b, p, qt, kt: (0, b * nq + kt[p])
    krow = lambda b, p, qt, kt: (b * nq + kt[p], 0)
    const = lambda b, p, qt, kt: (0, 0)
    grid_spec = pltpu.PrefetchScalarGridSpec(
        num_scalar_prefetch=2,
        grid=(bsz, len(pairs)),
        in_specs=[pl.BlockSpec((w, tq), qcol),
                  pl.BlockSpec((tq, w), krow),
                  pl.BlockSpec((w, tq), kcol),
                  pl.BlockSpec((META_KEYS_PAD, w), const),
                  pl.BlockSpec((w, META_KEYS_PAD), const)],
        out_specs=pl.BlockSpec((ATT_WIDTH, tq), qcol),
        scratch_shapes=[pltpu.VMEM((ATT_HEADS, 8, tq), F32),
                        pltpu.VMEM((ATT_HEADS, LANES, tq), F32),
                        pltpu.VMEM((2, tq, tq), F32)],
    )
    return pl.pallas_call(
        _attn_kernel,
        grid_spec=grid_spec,
        out_shape=jax.ShapeDtypeStruct((ATT_WIDTH, bsz * seq), BF16),
        compiler_params=pltpu.CompilerParams(
            dimension_semantics=("arbitrary", "arbitrary"), vmem_limit_bytes=48 * 1024 * 1024),
        name="fox_attention",
    )(qi_tab, ki_tab, q_t, k, v_t, k_meta, v_meta_t)


def _ssm_kernel(u_ref, kt_ref, wst_ref, wout_ref, um_ref, wm_ref, apow_ref, d_ref, y_ref, toep_ref,
                *, chunks_per_seq, n_steps):
    u = u_ref[0]
    rows = u.shape[0]
    p2 = 2 * SSM_STATE
    kt = kt_ref[0]
    n_c, tc = kt.shape
    per_vreg = LANES // n_c
    ext = jnp.concatenate([jnp.zeros_like(kt), kt], axis=1)
    for sub in range(per_vreg):
        rot = ext if sub == 0 else pltpu.roll(ext, n_c * sub, axis=1)
        for whole in range(tc // LANES):
            tp = per_vreg * whole + sub
            toep_ref[tp * n_c:(tp + 1) * n_c, :] = rot[:, tc - LANES * whole:2 * tc - LANES * whole].astype(BF16)
    y = jnp.dot(u, toep_ref[...], preferred_element_type=F32)
    s = jnp.dot(u, wst_ref[0], preferred_element_type=F32)
    x0 = jnp.dot(um_ref[0], wm_ref[0], preferred_element_type=F32)[0:1, :]

    lane = lax.broadcasted_iota(jnp.int32, (1, p2), 1)
    re_half = lane < SSM_STATE
    j = lax.broadcasted_iota(jnp.int32, (rows, 1), 0) % chunks_per_seq

    def cmul(step, z):
        ar = apow_ref[0, step:step + 1, :]
        ai = apow_ref[0, n_steps + step:n_steps + step + 1, :]
        return ar * z + ai * pltpu.roll(z, SSM_STATE, axis=1)

    s = s + jnp.where(j == 0, cmul(0, jnp.broadcast_to(x0, s.shape)), 0.0)
    for step in range(n_steps):
        sh = 1 << step
        prev = pltpu.roll(s, sh, axis=0)
        s = s + jnp.where(j >= sh, cmul(step, prev), 0.0)
    x_in = jnp.where(j == 0, x0, pltpu.roll(s, 1, axis=0))
    del re_half
    y = y + jnp.dot(x_in.astype(BF16), wout_ref[0], preferred_element_type=F32)
    y = y + d_ref[0] * u.astype(F32)
    y_ref[0] = jax.nn.gelu(y).astype(y_ref.dtype)


def _ssm(u_g, kt, wst, wout, u_meta, wmeta, apow, d_tiled, *, chunks_per_seq):
    g, rows, tc = u_g.shape
    n_steps = apow.shape[1] // 2
    grp = lambda i: (i, 0, 0)
    return pl.pallas_call(
        functools.partial(_ssm_kernel, chunks_per_seq=chunks_per_seq, n_steps=n_steps),
        grid=(g,),
        in_specs=[pl.BlockSpec((1, rows, tc), grp),
                  pl.BlockSpec((1,) + kt.shape[1:], grp),
                  pl.BlockSpec((1, tc, 2 * SSM_STATE), grp),
                  pl.BlockSpec((1, 2 * SSM_STATE, tc), grp),
                  pl.BlockSpec((1,) + u_meta.shape[1:], grp),
                  pl.BlockSpec((1,) + wmeta.shape[1:], grp),
                  pl.BlockSpec((1,) + apow.shape[1:], grp),
                  pl.BlockSpec((1, 1, tc), grp)],
        out_specs=pl.BlockSpec((1, rows, tc), grp),
        out_shape=jax.ShapeDtypeStruct((g, rows, tc), BF16),
        scratch_shapes=[pltpu.VMEM((tc, tc), BF16)],
        compiler_params=pltpu.CompilerParams(
            dimension_semantics=("arbitrary",), vmem_limit_bytes=48 * 1024 * 1024),
        name="s5_ssm",
    )(u_g, kt, wst, wout, u_meta, wmeta, apow, d_tiled)


def _ssm_tables(a_re, a_im, log_dt, b_re, b_im, c_re, c_im, d_skip, *, chunk, n_steps):
    g, p = a_re.shape
    c = b_re.shape[-1]
    dt = jnp.exp(log_dt)[:, None]
    mag = jnp.exp(a_re * dt)
    ang = a_im * dt
    lb_re, lb_im = mag * jnp.cos(ang), mag * jnp.sin(ang)
    den = a_re * a_re + a_im * a_im
    z_re, z_im = lb_re - 1.0, lb_im
    coef_re = (z_re * a_re + z_im * a_im) / den
    coef_im = (z_im * a_re - z_re * a_im) / den
    bb_re = coef_re[..., None] * b_re - coef_im[..., None] * b_im
    bb_im = coef_re[..., None] * b_im + coef_im[..., None] * b_re

    def step(carry, _):
        cr, ci = carry
        return (cr * lb_re - ci * lb_im, cr * lb_im + ci * lb_re), (cr, ci)
    _, (pw_re, pw_im) = lax.scan(step, (jnp.ones_like(lb_re), jnp.zeros_like(lb_re)), None,
                                 length=chunk + 1)

    e_re = c_re[None] * pw_re[:, :, None, :] - c_im[None] * pw_im[:, :, None, :]
    e_im = c_re[None] * pw_im[:, :, None, :] + c_im[None] * pw_re[:, :, None, :]
    kern = (jnp.einsum('tgcp,gpd->tgcd', e_re[:chunk], bb_re)
            - jnp.einsum('tgcp,gpd->tgcd', e_im[:chunk], bb_im))
    kt = jnp.transpose(kern, (1, 3, 0, 2)).reshape(g, c, chunk * c)

    def in_to_state(n):
        wr = pw_re[n - 1::-1][:n, :, :, None] * bb_re[None] - pw_im[n - 1::-1][:n, :, :, None] * bb_im[None]
        wi = pw_re[n - 1::-1][:n, :, :, None] * bb_im[None] + pw_im[n - 1::-1][:n, :, :, None] * bb_re[None]
        w = jnp.concatenate([wr, wi], axis=2)
        return jnp.transpose(w, (1, 0, 3, 2)).reshape(g, n * c, 2 * p)
    wst = in_to_state(chunk)
    wmeta = in_to_state(N_META)

    wout = jnp.concatenate([e_re[1:chunk + 1], -e_im[1:chunk + 1]], axis=3)
    wout = jnp.transpose(wout, (1, 3, 0, 2)).reshape(g, 2 * p, chunk * c)

    ar, ai = pw_re[chunk], pw_im[chunk]
    rows_r, rows_i = [], []
    for _ in range(n_steps):
        rows_r.append(jnp.concatenate([ar, ar], axis=-1))
        rows_i.append(jnp.concatenate([-ai, ai], axis=-1))
        ar, ai = ar * ar - ai * ai, 2.0 * ar * ai
    apow = jnp.stack(rows_r + rows_i, axis=1)
    d_tiled = jnp.tile(d_skip, (1, chunk))[:, None, :]
    return kt, wst.astype(BF16), wout.astype(BF16), wmeta.astype(BF16), apow, d_tiled


def _postmix_kernel(x_ref, ya_ref, yb_ref, ga_ref, gb_ref, lng_ref, lnb_ref,
                    wglu_ref, bglu_ref, wa_ref, wb_ref, wo_ref, l1g_ref, l1b_ref,
                    wrh_ref, wrl_ref, br_ref, h1_ref, ridx_ref, rw_ref, cnt_ref, carry_ref,
                    *, alpha, n_experts):
    i = pl.program_id(0)
    tm = x_ref.shape[0]
    h0 = _layer_norm(x_ref[...], lng_ref[...], lnb_ref[...])
    yb = yb_ref[...]
    glu = yb.astype(F32) * _sigmoid(jnp.dot(yb, wglu_ref[...], preferred_element_type=F32) + bglu_ref[...])
    merged = (ga_ref[...].astype(F32) * jnp.dot(ya_ref[...], wa_ref[...], preferred_element_type=F32)
              + gb_ref[...].astype(F32) * jnp.dot(glu.astype(BF16), wb_ref[...], preferred_element_type=F32))
    mix = jnp.dot(merged.astype(BF16), wo_ref[...], preferred_element_type=F32)
    h1 = _layer_norm(alpha * h0 + mix, l1g_ref[...], l1b_ref[...])
    h1_ref[...] = h1.reshape(h1_ref.shape)
    hb = h1.astype(BF16)

    hl = (h1 - hb.astype(F32)).astype(BF16)
    logits = (jnp.dot(hb, wrh_ref[...], preferred_element_type=F32)
              + jnp.dot(hb, wrl_ref[...], preferred_element_type=F32)
              + jnp.dot(hl, wrh_ref[...], preferred_element_type=F32)) + br_ref[...]
    lane = lax.broadcasted_iota(jnp.int32, logits.shape, 1)
    logits = jnp.where(lane < n_experts, logits, -jnp.inf)
    denom = jnp.zeros((tm, 1), F32)
    top = None
    hits, idxs, ws = [], [], []
    for _ in range(TOP_K):
        mx = jnp.max(logits, axis=-1, keepdims=True)
        idx = jnp.min(jnp.where(logits == mx, lane, LANES), axis=-1, keepdims=True)
        hit = lane == idx
        if top is None:
            top = mx
        w = jnp.exp(mx - top)
        denom = denom + w
        hits.append(hit)
        idxs.append(idx)
        ws.append(w)
        logits = jnp.where(hit, -jnp.inf, logits)

    @pl.when(i == 0)
    def _():
        carry_ref[...] = jnp.zeros_like(carry_ref)

    sel = jnp.zeros(logits.shape, F32)
    for hit in hits:
        sel = sel + jnp.where(hit, 1.0, 0.0)
    row = lax.broadcasted_iota(jnp.int32, (tm, tm), 0)
    col = lax.broadcasted_iota(jnp.int32, (tm, tm), 1)
    before = (col < row).astype(BF16)
    seen = jnp.dot(before, sel.astype(BF16), preferred_element_type=F32) + carry_ref[...]
    ridx = jnp.zeros(logits.shape, jnp.int32)
    rw = jnp.zeros(logits.shape, F32)
    for kk in range(TOP_K):
        rank = jnp.sum(jnp.where(hits[kk], seen, 0.0), axis=-1, keepdims=True).astype(jnp.int32)
        ridx = jnp.where(lane == kk, idxs[kk], ridx)
        ridx = jnp.where(lane == TOP_K + kk, rank, ridx)
        rw = jnp.where(lane == kk, ws[kk] / denom, rw)
    ridx_ref[...] = ridx
    rw_ref[...] = rw
    total = carry_ref[...] + jnp.sum(sel, axis=0, keepdims=True)
    carry_ref[...] = total
    cnt_ref[...] = jnp.broadcast_to(total, cnt_ref.shape)


def _postmix(x2, ya, yb, ga, gb, ln_g, ln_b, w_glu, b_glu, w_a, w_b, w_o, l1g, l1b,
             wr_hi, wr_lo, b_r, *, tm, alpha, n_experts):
    t, dm = x2.shape
    sw = ya.shape[1]
    rows = lambda i: (i, 0)
    const = lambda i: (0, 0)
    full = lambda a: pl.BlockSpec(a.shape, const)
    return pl.pallas_call(
        functools.partial(_postmix_kernel, alpha=alpha, n_experts=n_experts),
        grid=(t // tm,),
        in_specs=[pl.BlockSpec((tm, dm), rows), pl.BlockSpec((tm, sw), rows), pl.BlockSpec((tm, sw), rows),
                  pl.BlockSpec((tm, dm), rows), pl.BlockSpec((tm, dm), rows),
                  full(ln_g), full(ln_b), full(w_glu), full(b_glu), full(w_a), full(w_b), full(w_o),
                  full(l1g), full(l1b), full(wr_hi), full(wr_lo), full(b_r)],
        out_specs=[pl.BlockSpec((tm, 1, dm), lambda i: (i, 0, 0)), pl.BlockSpec((tm, LANES), rows),
                   pl.BlockSpec((tm, LANES), rows), pl.BlockSpec((8, LANES), const)],
        out_shape=[jax.ShapeDtypeStruct((t, 1, dm), F32), jax.ShapeDtypeStruct((t, LANES), jnp.int32),
                   jax.ShapeDtypeStruct((t, LANES), F32), jax.ShapeDtypeStruct((8, LANES), F32)],
        scratch_shapes=[pltpu.VMEM((1, LANES), F32)],
        compiler_params=pltpu.CompilerParams(
            dimension_semantics=("arbitrary",), vmem_limit_bytes=48 * 1024 * 1024),
        name="postmix_router",
    )(x2, ya, yb, ga, gb, ln_g, ln_b, w_glu, b_glu, w_a, w_b, w_o, l1g, l1b, wr_hi, wr_lo, b_r)


MOE_ROW_TILE = 512


def _moe_kernel(te_ref, nv_ref, inv_ref, h_hbm, wg_ref, bg_ref, wu_ref, bu_ref, wd_ref, bd_ref, o_hbm,
                xbuf_ref, ybuf_ref, x2_ref, wgb_ref, wub_ref, wdb_ref, sem_g, sem_s, *, n_tokens):
    i = pl.program_id(0)
    n = pl.num_programs(0)
    _, tm, _, dm = xbuf_ref.shape
    cur = i % 2
    oth = (i + 1) % 2

    def start_gather(tile, slot):
        base = tile * tm

        def body(r, c):
            s = inv_ref[base + r]
            tok = s & (n_tokens - 1) if n_tokens & (n_tokens - 1) == 0 else s % n_tokens
            pltpu.make_async_copy(h_hbm.at[tok], xbuf_ref.at[slot, r], sem_g.at[slot]).start()
            return c
        lax.fori_loop(0, tm, body, 0, unroll=8)

    def wait_gather(slot):
        pltpu.make_async_copy(h_hbm.at[pl.ds(0, tm)], xbuf_ref.at[slot], sem_g.at[slot]).wait()

    def start_scatter(tile, slot):
        base = tile * tm

        def body(r, c):
            pltpu.make_async_copy(ybuf_ref.at[slot, r], o_hbm.at[inv_ref[base + r]], sem_s.at[slot]).start()
            return c
        lax.fori_loop(0, tm, body, 0, unroll=8)

    def wait_scatter(slot):
        pltpu.make_async_copy(ybuf_ref.at[slot], o_hbm.at[pl.ds(0, tm)], sem_s.at[slot]).wait()

    @pl.when(i == 0)
    def _():
        ybuf_ref[...] = jnp.zeros_like(ybuf_ref)
        start_gather(0, 0)

    @pl.when(i >= 1)
    def _():
        wait_scatter(cur)

    valid = i < nv_ref[0]
    new_expert = jnp.logical_or(i == 0, te_ref[i] != te_ref[jnp.maximum(i - 1, 0)])

    @pl.when(jnp.logical_and(valid, new_expert))
    def _():
        wgb_ref[...] = wg_ref[0].astype(BF16)
        wub_ref[...] = wu_ref[0].astype(BF16)
        wdb_ref[...] = wd_ref[0].astype(BF16)

    def row_copies():
        wait_gather(cur)
        start_gather(jnp.minimum(i + 1, n - 1), oth)
        start_scatter(jnp.maximum(i - 1, 0), oth)

    @pl.when(valid)
    def _():
        row_copies()
        x2_ref[...] = xbuf_ref[cur].reshape(x2_ref.shape)
        x = x2_ref[...].astype(BF16)
        gate = jnp.minimum(jnp.dot(x, wgb_ref[...], preferred_element_type=F32) + bg_ref[0], SWIGLU_LIMIT)
        up = jnp.clip(jnp.dot(x, wub_ref[...], preferred_element_type=F32) + bu_ref[0],
                      -SWIGLU_LIMIT, SWIGLU_LIMIT)
        act = (up + 1.0) * gate * _sigmoid(SWIGLU_ALPHA * gate)
        y = jnp.dot(act.astype(BF16), wdb_ref[...], preferred_element_type=F32) + bd_ref[0]
        ybuf_ref[cur] = y.reshape(tm, 1, dm)

    @pl.when(jnp.logical_not(valid))
    def _():
        row_copies()

    @pl.when(i == n - 1)
    def _():
        wait_gather(oth)
        wait_scatter(oth)
        start_scatter(i, cur)
        wait_scatter(cur)


def _moe(tile_expert, n_valid, inv, h1_rows, w_gate, b_gate, w_up, b_up, w_down, b_down, *, tm):
    t, _, dm = h1_rows.shape
    _, _, de = w_gate.shape
    n_rows = inv.shape[0]
    per_e = lambda i, te, nv, iv: (te[i], 0, 0)
    return pl.pallas_call(
        functools.partial(_moe_kernel, n_tokens=t),
        grid_spec=pltpu.PrefetchScalarGridSpec(
            num_scalar_prefetch=3, grid=(n_rows // tm,),
            in_specs=[pl.BlockSpec(memory_space=pl.ANY),
                      pl.BlockSpec((1, dm, de), per_e), pl.BlockSpec((1, 1, de), per_e),
                      pl.BlockSpec((1, dm, de), per_e), pl.BlockSpec((1, 1, de), per_e),
                      pl.BlockSpec((1, de, dm), per_e), pl.BlockSpec((1, 1, dm), per_e)],
            out_specs=pl.BlockSpec(memory_space=pl.ANY),
            scratch_shapes=[pltpu.VMEM((2, tm, 1, dm), F32), pltpu.VMEM((2, tm, 1, dm), F32),
                            pltpu.VMEM((tm, dm), F32), pltpu.VMEM((dm, de), BF16),
                            pltpu.VMEM((dm, de), BF16), pltpu.VMEM((de, dm), BF16),
                            pltpu.SemaphoreType.DMA((2,)), pltpu.SemaphoreType.DMA((2,))]),
        out_shape=jax.ShapeDtypeStruct((n_rows, 1, dm), F32),
        compiler_params=pltpu.CompilerParams(
            dimension_semantics=("arbitrary",), vmem_limit_bytes=56 * 1024 * 1024),
        name="moe_experts",
    )(tile_expert, n_valid, inv, h1_rows, w_gate, b_gate, w_up, b_up, w_down, b_down)


def _combine_kernel(*refs, alpha):
    y_refs = refs[:TOP_K]
    h1_ref, rw_ref, l2g_ref, l2b_ref, o_ref, rows2_ref = refs[TOP_K:]
    tm, dm = o_ref.shape
    rw = rw_ref[...]
    ffn = jnp.zeros((tm, dm), F32)
    for kk in range(TOP_K):
        rows2_ref[...] = y_refs[kk][...].reshape(tm, dm)
        ffn = ffn + rw[:, kk:kk + 1] * rows2_ref[...]
    rows2_ref[...] = h1_ref[...].reshape(tm, dm)
    o_ref[...] = _layer_norm(alpha * rows2_ref[...] + ffn, l2g_ref[...], l2b_ref[...])


def _combine(y_slots, h1_rows, rw, l2g, l2b, *, tm, alpha):
    t, _, dm = h1_rows.shape
    rows = lambda i: (i, 0)
    const = lambda i: (0, 0)
    slot_specs = [pl.BlockSpec((tm, 1, dm), functools.partial(lambda i, kk: (kk * (t // tm) + i, 0, 0), kk=kk))
                  for kk in range(TOP_K)]
    return pl.pallas_call(
        functools.partial(_combine_kernel, alpha=alpha),
        grid=(t // tm,),
        in_specs=slot_specs + [pl.BlockSpec((tm, 1, dm), lambda i: (i, 0, 0)),
                               pl.BlockSpec((tm, LANES), rows),
                               pl.BlockSpec((1, dm), const), pl.BlockSpec((1, dm), const)],
        out_specs=pl.BlockSpec((tm, dm), rows),
        out_shape=jax.ShapeDtypeStruct((t, dm), F32),
        scratch_shapes=[pltpu.VMEM((tm, dm), F32)],
        compiler_params=pltpu.CompilerParams(
            dimension_semantics=("arbitrary",), vmem_limit_bytes=48 * 1024 * 1024),
        name="moe_combine_ln2",
    )(*([y_slots] * TOP_K), h1_rows, rw, l2g, l2b)


INVERT_CHUNK = 8192


def _invert_kernel(rows_ref, slots_ref, inv_ref):
    def body(p, c):
        inv_ref[rows_ref[p]] = slots_ref[p]
        return c
    lax.fori_loop(0, rows_ref.shape[0], body, 0, unroll=8)


def _invert(rows, slots):
    n = rows.shape[0]
    ch = math.gcd(n, INVERT_CHUNK)
    chunk = pl.BlockSpec((ch,), lambda i: (i,), memory_space=pltpu.SMEM)
    return pl.pallas_call(
        _invert_kernel,
        grid=(n // ch,),
        in_specs=[chunk, chunk],
        out_specs=pl.BlockSpec(memory_space=pltpu.SMEM),
        out_shape=jax.ShapeDtypeStruct((n,), jnp.int32),
        compiler_params=pltpu.CompilerParams(dimension_semantics=("arbitrary",)),
        name="moe_row_slots",
    )(rows, slots)


def _routing_tables(ridx, counts, *, n_experts, tm):
    t = ridx.shape[0]
    n_real = TOP_K * t
    n_fill = n_experts * tm
    e_idx = ridx[:, :TOP_K]
    rank = ridx[:, TOP_K:2 * TOP_K]
    cnt = counts[0, :n_experts].astype(jnp.int32)
    padded = (cnt + tm - 1) // tm * tm
    ends = jnp.cumsum(padded)
    offs = ends - padded
    experts = jnp.arange(n_experts, dtype=jnp.int32)
    base = jnp.sum(jnp.where(e_idx[..., None] == experts, offs, 0), axis=-1)
    token_rows = (base + rank).reshape(-1)
    token_slots = (jnp.arange(TOP_K, dtype=jnp.int32)[None, :] * t
                   + jnp.arange(t, dtype=jnp.int32)[:, None]).reshape(-1)
    fill_cnt = padded - cnt
    fill_end = jnp.cumsum(fill_cnt)
    shift = jnp.concatenate([offs + cnt - (fill_end - fill_cnt), jnp.full((1,), n_real, jnp.int32)])
    j = jnp.arange(n_fill, dtype=jnp.int32)
    fill_rows = j + shift[0] + jnp.sum(
        jnp.where(j[:, None] >= fill_end[None, :], (shift[1:] - shift[:-1])[None, :], 0), axis=1)
    rows = jnp.concatenate([token_rows, fill_rows])
    slots = jnp.concatenate([token_slots, n_real + j])
    inv = _invert(rows, slots)
    n_tiles = (n_real + n_fill) // tm
    tile_start = jnp.arange(n_tiles, dtype=jnp.int32) * tm
    tile_expert = jnp.minimum(jnp.sum(ends[None, :] <= tile_start[:, None], axis=1), n_experts - 1)
    n_valid = (ends[-1] // tm).reshape(1)
    return inv.astype(jnp.int32), tile_expert.astype(jnp.int32), n_valid.astype(jnp.int32)


def _pick_tile(n, want):
    t = min(n, want)
    assert n % t == 0, (n, t)
    return t


def kernel(x, meta, ln_in_g, ln_in_b, w_in, b_f, w_up_a, w_up_b, w_o, ssm_a_re, ssm_a_im, ssm_log_dt,
           ssm_b_re, ssm_b_im, ssm_c_re, ssm_c_im, ssm_d, w_glu, b_glu, ln1_g, ln1_b, w_router, b_router,
           w_gate, b_gate, w_up, b_up, w_down, b_down, ln2_g, ln2_b):
    bsz, seq, dm = x.shape
    depth = w_in.shape[0]
    assert depth == 1 and meta.shape[0] == N_META
    alpha = (2.0 * depth) ** 0.25
    n_groups = ssm_a_re.shape[1]
    n_experts = w_router.shape[-1]
    aw = ATT_WIDTH
    t = bsz * seq
    x2 = x.reshape(t, dm)
    row = lambda a: a.reshape(1, -1)

    f_off = 3 * aw
    u_off = f_off + ATT_HEADS
    w0 = w_in[0]
    w_main = jnp.concatenate([w0[:, :f_off], w0[:, u_off:]], axis=1).astype(BF16)
    w_f = jnp.pad(w0[:, f_off:u_off], ((0, 0), (0, LANES - ATT_HEADS))).astype(BF16)
    b_f_pad = jnp.pad(b_f[0], (0, LANES - ATT_HEADS)).reshape(1, LANES)
    ln_g, ln_b = row(ln_in_g), row(ln_in_b)

    tm = _pick_tile(seq, 512)
    q, k, v, u, ga, gb, fcum = _inproj(x2, ln_g, ln_b, w_main, w_f, b_f_pad, tm=tm, tiles_per_seq=seq // tm)
    _, k_m, v_m, u_m, _, _, fcum_m = _inproj(meta, ln_g, ln_b, w_main, w_f, b_f_pad, tm=N_META, tiles_per_seq=1)

    pad_m = META_KEYS_PAD - N_META
    fm = fcum_m[:, :ATT_HEADS]
    bias_m = jnp.pad(-LOG2E * (fm - fm[N_META - 1:N_META, :]), ((0, pad_m), (0, 0)), constant_values=-BIG)
    pieces_m = jnp.stack(_split3(bias_m), axis=-1)
    k_meta = jnp.pad(k_m, ((0, pad_m), (0, 0))).reshape(META_KEYS_PAD, ATT_HEADS, LANES)
    k_meta = k_meta.at[:, :, HEAD_DIM:HEAD_DIM + 3].set(pieces_m).reshape(META_KEYS_PAD, ATT_HEADS * LANES)
    v_meta_t = jnp.pad(v_m, ((0, pad_m), (0, 0))).T
    tq = _pick_tile(seq, 512)
    y_a = _attn(q.T, k, v.T, k_meta, v_meta_t, bsz=bsz, seq=seq, tq=tq).T

    chunk = SSM_CHUNK
    n_chunks = seq // chunk
    n_steps = max(1, (n_chunks - 1).bit_length())
    kt, wst, wout, wmeta, apow, d_tiled = _ssm_tables(
        ssm_a_re[0], ssm_a_im[0], ssm_log_dt[0], ssm_b_re[0], ssm_b_im[0], ssm_c_re[0], ssm_c_im[0], ssm_d[0],
        chunk=chunk, n_steps=n_steps)
    u_g = jnp.transpose(u.reshape(bsz * n_chunks, chunk, n_groups, SSM_GROUP), (2, 0, 1, 3))
    u_g = u_g.reshape(n_groups, bsz * n_chunks, chunk * SSM_GROUP)
    um_g = jnp.transpose(u_m.reshape(N_META, n_groups, SSM_GROUP), (1, 0, 2)).reshape(n_groups, 1, N_META * SSM_GROUP)
    um_g = jnp.pad(um_g, ((0, 0), (0, 15), (0, 0)))
    y_g = _ssm(u_g, kt, wst, wout, um_g, wmeta, apow, d_tiled, chunks_per_seq=n_chunks)
    y_b = jnp.transpose(y_g.reshape(n_groups, bsz * n_chunks, chunk, SSM_GROUP), (1, 2, 0, 3)).reshape(t, -1)

    wr = jnp.pad(w_router[0], ((0, 0), (0, LANES - n_experts)))
    wr_hi = wr.astype(BF16)
    wr_lo = (wr - wr_hi.astype(F32)).astype(BF16)
    b_r = jnp.pad(b_router[0], (0, LANES - n_experts)).reshape(1, LANES)
    tm2 = _pick_tile(t, 512)
    h1_rows, ridx, rw, counts = _postmix(
        x2, y_a, y_b, ga, gb, ln_g, ln_b, w_glu[0].astype(BF16), row(b_glu[0]),
        w_up_a[0].astype(BF16), w_up_b[0].astype(BF16), w_o[0].astype(BF16), row(ln1_g[0]), row(ln1_b[0]),
        wr_hi, wr_lo, b_r, tm=tm2, alpha=alpha, n_experts=n_experts)

    tm3 = MOE_ROW_TILE
    inv, tile_expert, n_valid = _routing_tables(ridx, counts, n_experts=n_experts, tm=tm3)
    y_slots = _moe(tile_expert, n_valid, inv, h1_rows, w_gate[0], b_gate[0][:, None, :], w_up[0],
                   b_up[0][:, None, :], w_down[0], b_down[0][:, None, :], tm=tm3)
    tm4 = _pick_tile(t, 256)
    out = _combine(y_slots, h1_rows, rw, row(ln2_g[0]), row(ln2_b[0]), tm=tm4, alpha=alpha)
    return out.reshape(bsz, seq, dm)
```

```python
import functools
import math

import jax
import jax.numpy as jnp
from jax import lax
from jax.experimental import pallas as pl
from jax.experimental.pallas import tpu as pltpu

F32 = jnp.float32
BF16 = jnp.bfloat16

N_META = 16
ATT_HEADS = 8
HEAD_DIM = 64
ATT_WIDTH = ATT_HEADS * HEAD_DIM
SSM_GROUP = 16
SSM_STATE = 64
TOP_K = 4
SWIGLU_LIMIT = 7.0
SWIGLU_ALPHA = 1.702
LN_EPS = 1e-5

LANES = 128
SSM_CHUNK = 64
META_KEYS_PAD = LANES
BIG = 1e30
LOG2E = 1.4426950408889634


def _layer_norm(x, g, b):
    mu = jnp.mean(x, axis=-1, keepdims=True)
    xc = x - mu
    var = jnp.mean(xc * xc, axis=-1, keepdims=True)
    return xc * lax.rsqrt(var + LN_EPS) * g + b


def _log_sigmoid(z):
    return jnp.minimum(z, 0.0) - jnp.log1p(jnp.exp(-jnp.abs(z)))


def _sigmoid(z):
    return 1.0 / (1.0 + jnp.exp(-z))


def _split3(x):
    hi = x.astype(BF16)
    r1 = x - hi.astype(F32)
    mid = r1.astype(BF16)
    lo = (r1 - mid.astype(F32)).astype(BF16)
    return hi, mid, lo


def _inproj_kernel(x_ref, g_ref, b_ref, w_ref, wf_ref, bf_ref,
                   q_ref, k_ref, v_ref, u_ref, ga_ref, gb_ref, fc_ref, carry_ref,
                   *, tiles_per_seq):
    i = pl.program_id(0)
    tm = x_ref.shape[0]
    h = _layer_norm(x_ref[...], g_ref[...], b_ref[...]).astype(BF16)

    def proj(lo, hi):
        return jnp.dot(h, w_ref[:, lo:hi], preferred_element_type=F32)

    aw = ATT_WIDTH
    u_ref[...] = proj(3 * aw, 4 * aw).astype(BF16)
    dm = ga_ref.shape[1]
    ga_ref[...] = _sigmoid(proj(4 * aw, 4 * aw + dm)).astype(BF16)
    gb_ref[...] = _sigmoid(proj(4 * aw + dm, 4 * aw + 2 * dm)).astype(BF16)

    zf = jnp.dot(h, wf_ref[...], preferred_element_type=F32) + bf_ref[...]
    lf = _log_sigmoid(zf)

    @pl.when(i % tiles_per_seq == 0)
    def _():
        carry_ref[...] = jnp.zeros_like(carry_ref)

    row = lax.broadcasted_iota(jnp.int32, (tm, tm), 0)
    col = lax.broadcasted_iota(jnp.int32, (tm, tm), 1)
    tri = (col <= row).astype(BF16)
    hi, mid, lo = _split3(lf)
    cs = (jnp.dot(tri, hi, preferred_element_type=F32)
          + jnp.dot(tri, mid, preferred_element_type=F32)
          + jnp.dot(tri, lo, preferred_element_type=F32)) + carry_ref[...]
    fc_ref[...] = cs
    carry_ref[...] = cs[tm - 1:tm, :]

    lane = lax.broadcasted_iota(jnp.int32, (1, LANES), 1)
    own = lane < HEAD_DIM
    bias_lanes = jnp.logical_and(lane >= HEAD_DIM, lane < HEAD_DIM + 3)
    pieces = [p.astype(F32) for p in _split3(-LOG2E * cs)]

    def slabs(x, extra):
        out = []
        for hp in range(ATT_HEADS // 2):
            x2 = x[:, LANES * hp:LANES * (hp + 1)]
            out.append(jnp.where(own, x2, extra(2 * hp)))
            out.append(jnp.where(own, pltpu.roll(x2, HEAD_DIM, axis=1), extra(2 * hp + 1)))
        return jnp.concatenate(out, axis=1).astype(BF16)

    def bias_extra(h):
        e = jnp.zeros((tm, LANES), F32)
        for j, piece in enumerate(pieces):
            e = jnp.where(lane == HEAD_DIM + j, pltpu.roll(piece, (HEAD_DIM + j - h) % LANES, axis=1), e)
        return e

    q_ref[...] = slabs(proj(0, aw) * (LOG2E / math.sqrt(HEAD_DIM)), lambda h: jnp.where(bias_lanes, 1.0, 0.0))
    k_ref[...] = slabs(proj(aw, 2 * aw), bias_extra)
    v_ref[...] = slabs(proj(2 * aw, 3 * aw), lambda h: jnp.ones((1, LANES), F32))


def _inproj(x2, ln_g, ln_b, w_main, w_f, b_f, *, tm, tiles_per_seq):
    t, dm = x2.shape
    aw = ATT_WIDTH
    n_main = w_main.shape[1]
    const = lambda i: (0, 0)
    rows = lambda i: (i, 0)
    slab = ATT_HEADS * LANES
    out_shape = [jax.ShapeDtypeStruct((t, slab), BF16)] * 3 + [jax.ShapeDtypeStruct((t, aw), BF16)] + \
                [jax.ShapeDtypeStruct((t, dm), BF16)] * 2 + \
                [jax.ShapeDtypeStruct((t, LANES), F32)]
    out_specs = [pl.BlockSpec((tm, slab), rows)] * 3 + [pl.BlockSpec((tm, aw), rows)] + \
                [pl.BlockSpec((tm, dm), rows)] * 2 + [pl.BlockSpec((tm, LANES), rows)]
    return pl.pallas_call(
        functools.partial(_inproj_kernel, tiles_per_seq=tiles_per_seq),
        grid=(t // tm,),
        in_specs=[pl.BlockSpec((tm, dm), rows),
                  pl.BlockSpec((1, dm), const), pl.BlockSpec((1, dm), const),
                  pl.BlockSpec((dm, n_main), const),
                  pl.BlockSpec((dm, LANES), const), pl.BlockSpec((1, LANES), const)],
        out_specs=out_specs,
        out_shape=out_shape,
        scratch_shapes=[pltpu.VMEM((1, LANES), F32)],
        compiler_params=pltpu.CompilerParams(
            dimension_semantics=("arbitrary",), vmem_limit_bytes=56 * 1024 * 1024),
        name="inproj",
    )(x2, ln_g, ln_b, w_main, w_f, b_f)


def _attn_kernel(qi_tab, ki_tab, qt_ref, k_ref, vt_ref, km_ref, vmt_ref, o_ref, m_ref, acc_ref, st_ref):
    p = pl.program_id(1)
    qi = qi_tab[p]
    ki = ki_tab[p]
    tq = qt_ref.shape[1]

    def process(k_blk, vt_blk, mask):
        n = k_blk.shape[0]

        def logits(h):
            sl = slice(LANES * h, LANES * (h + 1))
            st = jnp.dot(k_blk[:, sl], qt_ref[sl, :], preferred_element_type=F32)
            if mask is not None:
                st = jnp.where(mask, st, -jnp.inf)
            st_ref[h % 2, 0:n] = st
            m_old = m_ref[h]
            m_new = jnp.maximum(m_old, jnp.max(st_ref[h % 2, 0:n], axis=0, keepdims=True))
            m_ref[h] = m_new
            return m_old, m_new

        def weigh(h, m_old, m_new):
            sl = slice(LANES * h, LANES * (h + 1))
            pt = jnp.exp2(st_ref[h % 2, 0:n] - m_new[0:1, :])
            alpha = jnp.exp2(m_old - m_new)
            acc_ref[h] = alpha[0:1, :] * acc_ref[h] + jnp.dot(
                vt_blk[sl, :], pt.astype(BF16), preferred_element_type=F32)

        ms = logits(0)
        for h in range(ATT_HEADS):
            nxt = logits(h + 1) if h + 1 < ATT_HEADS else None
            weigh(h, *ms)
            ms = nxt

    @pl.when(ki == 0)
    def _():
        m_ref[...] = jnp.full_like(m_ref, -jnp.inf)
        acc_ref[...] = jnp.zeros_like(acc_ref)
        process(km_ref[...], vmt_ref[...], None)

    @pl.when(ki < qi)
    def _():
        process(k_ref[...], vt_ref[...], None)

    @pl.when(ki == qi)
    def _():
        tk = k_ref.shape[0]
        key = lax.broadcasted_iota(jnp.int32, (tk, tq), 0)
        qry = lax.broadcasted_iota(jnp.int32, (tk, tq), 1)
        process(k_ref[...], vt_ref[...], key <= qry)
        for h in range(ATT_HEADS):
            a = acc_ref[h]
            o_ref[HEAD_DIM * h:HEAD_DIM * (h + 1), :] = (a[:HEAD_DIM] / a[HEAD_DIM:]).astype(o_ref.dtype)


def _attn(q_t, k, v_t, k_meta, v_meta_t, *, bsz, seq, tq):
    nq = seq // tq
    pairs = [(a, b) for a in range(nq) for b in range(a + 1)]
    qi_tab = jnp.asarray([a for a, _ in pairs], jnp.int32)
    ki_tab = jnp.asarray([b for _, b in pairs], jnp.int32)
    w = k.shape[1]
    qcol = lambda b, p, qt, kt: (0, b * nq + qt[p])
    kcol = lambda b, p, qt, kt: (0, b * nq + kt[p])
    krow = lambda b, p, qt, kt: (b * nq + kt[p], 0)
    const = lambda b, p, qt, kt: (0, 0)
    grid_spec = pltpu.PrefetchScalarGridSpec(
        num_scalar_prefetch=2,
        grid=(bsz, len(pairs)),
        in_specs=[pl.BlockSpec((w, tq), qcol),
                  pl.BlockSpec((tq, w), krow),
                  pl.BlockSpec((w, tq), kcol),
                  pl.BlockSpec((META_KEYS_PAD, w), const),
                  pl.BlockSpec((w, META_KEYS_PAD), const)],
        out_specs=pl.BlockSpec((ATT_WIDTH, tq), qcol),
        scratch_shapes=[pltpu.VMEM((ATT_HEADS, 8, tq), F32),
                        pltpu.VMEM((ATT_HEADS, LANES, tq), F32),
                        pltpu.VMEM((2, tq, tq), F32)],
    )
    return pl.pallas_call(
        _attn_kernel,
        grid_spec=grid_spec,
        out_shape=jax.ShapeDtypeStruct((ATT_WIDTH, bsz * seq), BF16),
        compiler_params=pltpu.CompilerParams(
            dimension_semantics=("arbitrary", "arbitrary"), vmem_limit_bytes=48 * 1024 * 1024),
        name="fox_attention",
    )(qi_tab, ki_tab, q_t, k, v_t, k_meta, v_meta_t)


def _ssm_kernel(u_ref, kt_ref, wst_ref, wout_ref, um_ref, wm_ref, apow_ref, d_ref, y_ref, toep_ref,
                *, chunks_per_seq, n_steps):
    u = u_ref[0]
    rows = u.shape[0]
    p2 = 2 * SSM_STATE
    kt = kt_ref[0]
    n_c, tc = kt.shape
    per_vreg = LANES // n_c
    ext = jnp.concatenate([jnp.zeros_like(kt), kt], axis=1)
    for sub in range(per_vreg):
        rot = ext if sub == 0 else pltpu.roll(ext, n_c * sub, axis=1)
        for whole in range(tc // LANES):
            tp = per_vreg * whole + sub
            toep_ref[tp * n_c:(tp + 1) * n_c, :] = rot[:, tc - LANES * whole:2 * tc - LANES * whole].astype(BF16)
    y = jnp.dot(u, toep_ref[...], preferred_element_type=F32)
    s = jnp.dot(u, wst_ref[0], preferred_element_type=F32)
    x0 = jnp.dot(um_ref[0], wm_ref[0], preferred_element_type=F32)[0:1, :]

    lane = lax.broadcasted_iota(jnp.int32, (1, p2), 1)
    re_half = lane < SSM_STATE
    j = lax.broadcasted_iota(jnp.int32, (rows, 1), 0) % chunks_per_seq

    def cmul(step, z):
        ar = apow_ref[0, step:step + 1, :]
        ai = apow_ref[0, n_steps + step:n_steps + step + 1, :]
        return ar * z + ai * pltpu.roll(z, SSM_STATE, axis=1)

    s = s + jnp.where(j == 0, cmul(0, jnp.broadcast_to(x0, s.shape)), 0.0)
    for step in range(n_steps):
        sh = 1 << step
        prev = pltpu.roll(s, sh, axis=0)
        s = s + jnp.where(j >= sh, cmul(step, prev), 0.0)
    x_in = jnp.where(j == 0, x0, pltpu.roll(s, 1, axis=0))
    del re_half
    y = y + jnp.dot(x_in.astype(BF16), wout_ref[0], preferred_element_type=F32)
    y = y + d_ref[0] * u.astype(F32)
    y_ref[0] = jax.nn.gelu(y).astype(y_ref.dtype)


def _ssm(u_g, kt, wst, wout, u_meta, wmeta, apow, d_tiled, *, chunks_per_seq):
    g, rows, tc = u_g.shape
    n_steps = apow.shape[1] // 2
    grp = lambda i: (i, 0, 0)
    return pl.pallas_call(
        functools.partial(_ssm_kernel, chunks_per_seq=chunks_per_seq, n_steps=n_steps),
        grid=(g,),
        in_specs=[pl.BlockSpec((1, rows, tc), grp),
                  pl.BlockSpec((1,) + kt.shape[1:], grp),
                  pl.BlockSpec((1, tc, 2 * SSM_STATE), grp),
                  pl.BlockSpec((1, 2 * SSM_STATE, tc), grp),
                  pl.BlockSpec((1,) + u_meta.shape[1:], grp),
                  pl.BlockSpec((1,) + wmeta.shape[1:], grp),
                  pl.BlockSpec((1,) + apow.shape[1:], grp),
                  pl.BlockSpec((1, 1, tc), grp)],
        out_specs=pl.BlockSpec((1, rows, tc), grp),
        out_shape=jax.ShapeDtypeStruct((g, rows, tc), BF16),
        scratch_shapes=[pltpu.VMEM((tc, tc), BF16)],
        compiler_params=pltpu.CompilerParams(
            dimension_semantics=("arbitrary",), vmem_limit_bytes=48 * 1024 * 1024),
        name="s5_ssm",
    )(u_g, kt, wst, wout, u_meta, wmeta, apow, d_tiled)


def _ssm_tables(a_re, a_im, log_dt, b_re, b_im, c_re, c_im, d_skip, *, chunk, n_steps):
    g, p = a_re.shape
    c = b_re.shape[-1]
    dt = jnp.exp(log_dt)[:, None]
    mag = jnp.exp(a_re * dt)
    ang = a_im * dt
    lb_re, lb_im = mag * jnp.cos(ang), mag * jnp.sin(ang)
    den = a_re * a_re + a_im * a_im
    z_re, z_im = lb_re - 1.0, lb_im
    coef_re = (z_re * a_re + z_im * a_im) / den
    coef_im = (z_im * a_re - z_re * a_im) / den
    bb_re = coef_re[..., None] * b_re - coef_im[..., None] * b_im
    bb_im = coef_re[..., None] * b_im + coef_im[..., None] * b_re

    def step(carry, _):
        cr, ci = carry
        return (cr * lb_re - ci * lb_im, cr * lb_im + ci * lb_re), (cr, ci)
    _, (pw_re, pw_im) = lax.scan(step, (jnp.ones_like(lb_re), jnp.zeros_like(lb_re)), None,
                                 length=chunk + 1)

    e_re = c_re[None] * pw_re[:, :, None, :] - c_im[None] * pw_im[:, :, None, :]
    e_im = c_re[None] * pw_im[:, :, None, :] + c_im[None] * pw_re[:, :, None, :]
    kern = (jnp.einsum('tgcp,gpd->tgcd', e_re[:chunk], bb_re)
            - jnp.einsum('tgcp,gpd->tgcd', e_im[:chunk], bb_im))
    kt = jnp.transpose(kern, (1, 3, 0, 2)).reshape(g, c, chunk * c)

    def in_to_state(n):
        wr = pw_re[n - 1::-1][:n, :, :, None] * bb_re[None] - pw_im[n - 1::-1][:n, :, :, None] * bb_im[None]
        wi = pw_re[n - 1::-1][:n, :, :, None] * bb_im[None] + pw_im[n - 1::-1][:n, :, :, None] * bb_re[None]
        w = jnp.concatenate([wr, wi], axis=2)
        return jnp.transpose(w, (1, 0, 3, 2)).reshape(g, n * c, 2 * p)
    wst = in_to_state(chunk)
    wmeta = in_to_state(N_META)

    wout = jnp.concatenate([e_re[1:chunk + 1], -e_im[1:chunk + 1]], axis=3)
    wout = jnp.transpose(wout, (1, 3, 0, 2)).reshape(g, 2 * p, chunk * c)

    ar, ai = pw_re[chunk], pw_im[chunk]
    rows_r, rows_i = [], []
    for _ in range(n_steps):
        rows_r.append(jnp.concatenate([ar, ar], axis=-1))
        rows_i.append(jnp.concatenate([-ai, ai], axis=-1))
        ar, ai = ar * ar - ai * ai, 2.0 * ar * ai
    apow = jnp.stack(rows_r + rows_i, axis=1)
    d_tiled = jnp.tile(d_skip, (1, chunk))[:, None, :]
    return kt, wst.astype(BF16), wout.astype(BF16), wmeta.astype(BF16), apow, d_tiled


def _postmix_kernel(x_ref, ya_ref, yb_ref, ga_ref, gb_ref, lng_ref, lnb_ref,
                    wglu_ref, bglu_ref, wa_ref, wb_ref, wo_ref, l1g_ref, l1b_ref,
                    wrh_ref, wrl_ref, br_ref, h1_ref, ridx_ref, rw_ref, cnt_ref, carry_ref,
                    *, alpha, n_experts):
    i = pl.program_id(0)
    tm = x_ref.shape[0]
    h0 = _layer_norm(x_ref[...], lng_ref[...], lnb_ref[...])
    yb = yb_ref[...]
    glu = yb.astype(F32) * _sigmoid(jnp.dot(yb, wglu_ref[...], preferred_element_type=F32) + bglu_ref[...])
    merged = (ga_ref[...].astype(F32) * jnp.dot(ya_ref[...], wa_ref[...], preferred_element_type=F32)
              + gb_ref[...].astype(F32) * jnp.dot(glu.astype(BF16), wb_ref[...], preferred_element_type=F32))
    mix = jnp.dot(merged.astype(BF16), wo_ref[...], preferred_element_type=F32)
    h1 = _layer_norm(alpha * h0 + mix, l1g_ref[...], l1b_ref[...])
    h1_ref[...] = h1.reshape(h1_ref.shape)
    hb = h1.astype(BF16)

    hl = (h1 - hb.astype(F32)).astype(BF16)
    logits = (jnp.dot(hb, wrh_ref[...], preferred_element_type=F32)
              + jnp.dot(hb, wrl_ref[...], preferred_element_type=F32)
              + jnp.dot(hl, wrh_ref[...], preferred_element_type=F32)) + br_ref[...]
    lane = lax.broadcasted_iota(jnp.int32, logits.shape, 1)
    logits = jnp.where(lane < n_experts, logits, -jnp.inf)
    denom = jnp.zeros((tm, 1), F32)
    top = None
    hits, idxs, ws = [], [], []
    for _ in range(TOP_K):
        mx = jnp.max(logits, axis=-1, keepdims=True)
        idx = jnp.min(jnp.where(logits == mx, lane, LANES), axis=-1, keepdims=True)
        hit = lane == idx
        if top is None:
            top = mx
        w = jnp.exp(mx - top)
        denom = denom + w
        hits.append(hit)
        idxs.append(idx)
        ws.append(w)
        logits = jnp.where(hit, -jnp.inf, logits)

    @pl.when(i == 0)
    def _():
        carry_ref[...] = jnp.zeros_like(carry_ref)

    sel = jnp.zeros(logits.shape, F32)
    for hit in hits:
        sel = sel + jnp.where(hit, 1.0, 0.0)
    row = lax.broadcasted_iota(jnp.int32, (tm, tm), 0)
    col = lax.broadcasted_iota(jnp.int32, (tm, tm), 1)
    before = (col < row).astype(BF16)
    seen = jnp.dot(before, sel.astype(BF16), preferred_element_type=F32) + carry_ref[...]
    ridx = jnp.zeros(logits.shape, jnp.int32)
    rw = jnp.zeros(logits.shape, F32)
    for kk in range(TOP_K):
        rank = jnp.sum(jnp.where(hits[kk], seen, 0.0), axis=-1, keepdims=True).astype(jnp.int32)
        ridx = jnp.where(lane == kk, idxs[kk], ridx)
        ridx = jnp.where(lane == TOP_K + kk, rank, ridx)
        rw = jnp.where(lane == kk, ws[kk] / denom, rw)
    ridx_ref[...] = ridx
    rw_ref[...] = rw
    total = carry_ref[...] + jnp.sum(sel, axis=0, keepdims=True)
    carry_ref[...] = total
    cnt_ref[...] = jnp.broadcast_to(total, cnt_ref.shape)


def _postmix(x2, ya, yb, ga, gb, ln_g, ln_b, w_glu, b_glu, w_a, w_b, w_o, l1g, l1b,
             wr_hi, wr_lo, b_r, *, tm, alpha, n_experts):
    t, dm = x2.shape
    sw = ya.shape[1]
    rows = lambda i: (i, 0)
    const = lambda i: (0, 0)
    full = lambda a: pl.BlockSpec(a.shape, const)
    return pl.pallas_call(
        functools.partial(_postmix_kernel, alpha=alpha, n_experts=n_experts),
        grid=(t // tm,),
        in_specs=[pl.BlockSpec((tm, dm), rows), pl.BlockSpec((tm, sw), rows), pl.BlockSpec((tm, sw), rows),
                  pl.BlockSpec((tm, dm), rows), pl.BlockSpec((tm, dm), rows),
                  full(ln_g), full(ln_b), full(w_glu), full(b_glu), full(w_a), full(w_b), full(w_o),
                  full(l1g), full(l1b), full(wr_hi), full(wr_lo), full(b_r)],
        out_specs=[pl.BlockSpec((tm, 1, dm), lambda i: (i, 0, 0)), pl.BlockSpec((tm, LANES), rows),
                   pl.BlockSpec((tm, LANES), rows), pl.BlockSpec((8, LANES), const)],
        out_shape=[jax.ShapeDtypeStruct((t, 1, dm), F32), jax.ShapeDtypeStruct((t, LANES), jnp.int32),
                   jax.ShapeDtypeStruct((t, LANES), F32), jax.ShapeDtypeStruct((8, LANES), F32)],
        scratch_shapes=[pltpu.VMEM((1, LANES), F32)],
        compiler_params=pltpu.CompilerParams(
            dimension_semantics=("arbitrary",), vmem_limit_bytes=48 * 1024 * 1024),
        name="postmix_router",
    )(x2, ya, yb, ga, gb, ln_g, ln_b, w_glu, b_glu, w_a, w_b, w_o, l1g, l1b, wr_hi, wr_lo, b_r)


MOE_ROW_TILE = 512
MOE_COL_CHUNK = 256


def _moe_kernel(te_ref, nv_ref, inv_ref, h_hbm, wg_ref, bg_ref, wu_ref, bu_ref, wd_ref, bd_ref, o_hbm,
                xbuf_ref, ybuf_ref, x2_ref, xb_ref, act_ref, wgb_ref, wub_ref, wdb_ref, sem_g, sem_s,
                *, n_tokens):
    i = pl.program_id(0)
    n = pl.num_programs(0)
    _, tm, _, dm = xbuf_ref.shape
    cur = i % 2
    oth = (i + 1) % 2

    def start_gather(tile, slot):
        base = tile * tm

        def body(r, c):
            s = inv_ref[base + r]
            tok = s & (n_tokens - 1) if n_tokens & (n_tokens - 1) == 0 else s % n_tokens
            pltpu.make_async_copy(h_hbm.at[tok], xbuf_ref.at[slot, r], sem_g.at[slot]).start()
            return c
        lax.fori_loop(0, tm, body, 0, unroll=8)

    def wait_gather(slot):
        pltpu.make_async_copy(h_hbm.at[pl.ds(0, tm)], xbuf_ref.at[slot], sem_g.at[slot]).wait()

    def start_scatter(tile, slot):
        base = tile * tm

        def body(r, c):
            pltpu.make_async_copy(ybuf_ref.at[slot, r], o_hbm.at[inv_ref[base + r]], sem_s.at[slot]).start()
            return c
        lax.fori_loop(0, tm, body, 0, unroll=8)

    def wait_scatter(slot):
        pltpu.make_async_copy(ybuf_ref.at[slot], o_hbm.at[pl.ds(0, tm)], sem_s.at[slot]).wait()

    @pl.when(i == 0)
    def _():
        ybuf_ref[...] = jnp.zeros_like(ybuf_ref)
        start_gather(0, 0)

    @pl.when(i >= 1)
    def _():
        wait_scatter(cur)

    valid = i < nv_ref[0]
    new_expert = jnp.logical_or(i == 0, te_ref[i] != te_ref[jnp.maximum(i - 1, 0)])

    @pl.when(jnp.logical_and(valid, new_expert))
    def _():
        wgb_ref[...] = wg_ref[0].astype(BF16)
        wub_ref[...] = wu_ref[0].astype(BF16)
        wdb_ref[...] = wd_ref[0].astype(BF16)

    def row_copies():
        wait_gather(cur)
        start_gather(jnp.minimum(i + 1, n - 1), oth)
        start_scatter(jnp.maximum(i - 1, 0), oth)

    def start_row_group(g, n_groups):
        nxt_base = jnp.minimum(i + 1, n - 1) * tm
        prv_base = jnp.maximum(i - 1, 0) * tm
        per = tm // n_groups
        for r in range(g * per, (g + 1) * per):
            s = inv_ref[nxt_base + r]
            tok = s & (n_tokens - 1) if n_tokens & (n_tokens - 1) == 0 else s % n_tokens
            pltpu.make_async_copy(h_hbm.at[tok], xbuf_ref.at[oth, r], sem_g.at[oth]).start()
            pltpu.make_async_copy(ybuf_ref.at[oth, r], o_hbm.at[inv_ref[prv_base + r]], sem_s.at[oth]).start()

    @pl.when(valid)
    def _():
        de = wgb_ref.shape[1]
        nc_up, nc_down = de // MOE_COL_CHUNK, dm // MOE_COL_CHUNK
        wait_gather(cur)
        x2_ref[...] = xbuf_ref[cur].reshape(x2_ref.shape)
        xb_ref[...] = x2_ref[...].astype(BF16)
        for c in range(nc_up):
            start_row_group(c, nc_up + nc_down)
            cols = slice(c * MOE_COL_CHUNK, (c + 1) * MOE_COL_CHUNK)
            gate = jnp.minimum(jnp.dot(xb_ref[...], wgb_ref[:, cols], preferred_element_type=F32)
                               + bg_ref[0][:, cols], SWIGLU_LIMIT)
            up = jnp.clip(jnp.dot(xb_ref[...], wub_ref[:, cols], preferred_element_type=F32)
                          + bu_ref[0][:, cols], -SWIGLU_LIMIT, SWIGLU_LIMIT)
            act_ref[:, cols] = ((up + 1.0) * gate * _sigmoid(SWIGLU_ALPHA * gate)).astype(BF16)
        for c in range(nc_down):
            start_row_group(nc_up + c, nc_up + nc_down)
            cols = slice(c * MOE_COL_CHUNK, (c + 1) * MOE_COL_CHUNK)
            x2_ref[:, cols] = jnp.dot(act_ref[...], wdb_ref[:, cols], preferred_element_type=F32) \
                + bd_ref[0][:, cols]
        ybuf_ref[cur] = x2_ref[...].reshape(tm, 1, dm)

    @pl.when(jnp.logical_not(valid))
    def _():
        row_copies()

    @pl.when(i == n - 1)
    def _():
        wait_gather(oth)
        wait_scatter(oth)
        start_scatter(i, cur)
        wait_scatter(cur)


def _moe(tile_expert, n_valid, inv, h1_rows, w_gate, b_gate, w_up, b_up, w_down, b_down, *, tm):
    t, _, dm = h1_rows.shape
    _, _, de = w_gate.shape
    n_rows = inv.shape[0]
    per_e = lambda i, te, nv, iv: (te[i], 0, 0)
    return pl.pallas_call(
        functools.partial(_moe_kernel, n_tokens=t),
        grid_spec=pltpu.PrefetchScalarGridSpec(
            num_scalar_prefetch=3, grid=(n_rows // tm,),
            in_specs=[pl.BlockSpec(memory_space=pl.ANY),
                      pl.BlockSpec((1, dm, de), per_e), pl.BlockSpec((1, 1, de), per_e),
                      pl.BlockSpec((1, dm, de), per_e), pl.BlockSpec((1, 1, de), per_e),
                      pl.BlockSpec((1, de, dm), per_e), pl.BlockSpec((1, 1, dm), per_e)],
            out_specs=pl.BlockSpec(memory_space=pl.ANY),
            scratch_shapes=[pltpu.VMEM((2, tm, 1, dm), F32), pltpu.VMEM((2, tm, 1, dm), F32),
                            pltpu.VMEM((tm, dm), F32), pltpu.VMEM((tm, dm), BF16), pltpu.VMEM((tm, de), BF16),
                            pltpu.VMEM((dm, de), BF16),
                            pltpu.VMEM((dm, de), BF16), pltpu.VMEM((de, dm), BF16),
                            pltpu.SemaphoreType.DMA((2,)), pltpu.SemaphoreType.DMA((2,))]),
        out_shape=jax.ShapeDtypeStruct((n_rows, 1, dm), F32),
        compiler_params=pltpu.CompilerParams(
            dimension_semantics=("arbitrary",), vmem_limit_bytes=56 * 1024 * 1024),
        name="moe_experts",
    )(tile_expert, n_valid, inv, h1_rows, w_gate, b_gate, w_up, b_up, w_down, b_down)


def _combine_kernel(*refs, alpha):
    y_refs = refs[:TOP_K]
    h1_ref, rw_ref, l2g_ref, l2b_ref, o_ref, rows2_ref = refs[TOP_K:]
    tm, dm = o_ref.shape
    rw = rw_ref[...]
    ffn = jnp.zeros((tm, dm), F32)
    for kk in range(TOP_K):
        rows2_ref[...] = y_refs[kk][...].reshape(tm, dm)
        ffn = ffn + rw[:, kk:kk + 1] * rows2_ref[...]
    rows2_ref[...] = h1_ref[...].reshape(tm, dm)
    o_ref[...] = _layer_norm(alpha * rows2_ref[...] + ffn, l2g_ref[...], l2b_ref[...])


def _combine(y_slots, h1_rows, rw, l2g, l2b, *, tm, alpha):
    t, _, dm = h1_rows.shape
    rows = lambda i: (i, 0)
    const = lambda i: (0, 0)
    slot_specs = [pl.BlockSpec((tm, 1, dm), functools.partial(lambda i, kk: (kk * (t // tm) + i, 0, 0), kk=kk))
                  for kk in range(TOP_K)]
    return pl.pallas_call(
        functools.partial(_combine_kernel, alpha=alpha),
        grid=(t // tm,),
        in_specs=slot_specs + [pl.BlockSpec((tm, 1, dm), lambda i: (i, 0, 0)),
                               pl.BlockSpec((tm, LANES), rows),
                               pl.BlockSpec((1, dm), const), pl.BlockSpec((1, dm), const)],
        out_specs=pl.BlockSpec((tm, dm), rows),
        out_shape=jax.ShapeDtypeStruct((t, dm), F32),
        scratch_shapes=[pltpu.VMEM((tm, dm), F32)],
        compiler_params=pltpu.CompilerParams(
            dimension_semantics=("arbitrary",), vmem_limit_bytes=48 * 1024 * 1024),
        name="moe_combine_ln2",
    )(*([y_slots] * TOP_K), h1_rows, rw, l2g, l2b)


INVERT_CHUNK = 8192


def _invert_kernel(rows_ref, slots_ref, inv_ref):
    def body(p, c):
        inv_ref[rows_ref[p]] = slots_ref[p]
        return c
    lax.fori_loop(0, rows_ref.shape[0], body, 0, unroll=8)


def _invert(rows, slots):
    n = rows.shape[0]
    ch = math.gcd(n, INVERT_CHUNK)
    chunk = pl.BlockSpec((ch,), lambda i: (i,), memory_space=pltpu.SMEM)
    return pl.pallas_call(
        _invert_kernel,
        grid=(n // ch,),
        in_specs=[chunk, chunk],
        out_specs=pl.BlockSpec(memory_space=pltpu.SMEM),
        out_shape=jax.ShapeDtypeStruct((n,), jnp.int32),
        compiler_params=pltpu.CompilerParams(dimension_semantics=("arbitrary",)),
        name="moe_row_slots",
    )(rows, slots)


def _routing_tables(ridx, counts, *, n_experts, tm):
    t = ridx.shape[0]
    n_real = TOP_K * t
    n_fill = n_experts * tm
    e_idx = ridx[:, :TOP_K]
    rank = ridx[:, TOP_K:2 * TOP_K]
    cnt = counts[0, :n_experts].astype(jnp.int32)
    padded = (cnt + tm - 1) // tm * tm
    ends = jnp.cumsum(padded)
    offs = ends - padded
    experts = jnp.arange(n_experts, dtype=jnp.int32)
    base = jnp.sum(jnp.where(e_idx[..., None] == experts, offs, 0), axis=-1)
    token_rows = (base + rank).reshape(-1)
    token_slots = (jnp.arange(TOP_K, dtype=jnp.int32)[None, :] * t
                   + jnp.arange(t, dtype=jnp.int32)[:, None]).reshape(-1)
    fill_cnt = padded - cnt
    fill_end = jnp.cumsum(fill_cnt)
    shift = jnp.concatenate([offs + cnt - (fill_end - fill_cnt), jnp.full((1,), n_real, jnp.int32)])
    j = jnp.arange(n_fill, dtype=jnp.int32)
    fill_rows = j + shift[0] + jnp.sum(
        jnp.where(j[:, None] >= fill_end[None, :], (shift[1:] - shift[:-1])[None, :], 0), axis=1)
    rows = jnp.concatenate([token_rows, fill_rows])
    slots = jnp.concatenate([token_slots, n_real + j])
    inv = _invert(rows, slots)
    n_tiles = (n_real + n_fill) // tm
    tile_start = jnp.arange(n_tiles, dtype=jnp.int32) * tm
    tile_expert = jnp.minimum(jnp.sum(ends[None, :] <= tile_start[:, None], axis=1), n_experts - 1)
    n_valid = (ends[-1] // tm).reshape(1)
    return inv.astype(jnp.int32), tile_expert.astype(jnp.int32), n_valid.astype(jnp.int32)


def _pick_tile(n, want):
    t = min(n, want)
    assert n % t == 0, (n, t)
    return t


def kernel(x, meta, ln_in_g, ln_in_b, w_in, b_f, w_up_a, w_up_b, w_o, ssm_a_re, ssm_a_im, ssm_log_dt,
           ssm_b_re, ssm_b_im, ssm_c_re, ssm_c_im, ssm_d, w_glu, b_glu, ln1_g, ln1_b, w_router, b_router,
           w_gate, b_gate, w_up, b_up, w_down, b_down, ln2_g, ln2_b):
    bsz, seq, dm = x.shape
    depth = w_in.shape[0]
    assert depth == 1 and meta.shape[0] == N_META
    alpha = (2.0 * depth) ** 0.25
    n_groups = ssm_a_re.shape[1]
    n_experts = w_router.shape[-1]
    aw = ATT_WIDTH
    t = bsz * seq
    x2 = x.reshape(t, dm)
    row = lambda a: a.reshape(1, -1)

    f_off = 3 * aw
    u_off = f_off + ATT_HEADS
    w0 = w_in[0]
    w_main = jnp.concatenate([w0[:, :f_off], w0[:, u_off:]], axis=1).astype(BF16)
    w_f = jnp.pad(w0[:, f_off:u_off], ((0, 0), (0, LANES - ATT_HEADS))).astype(BF16)
    b_f_pad = jnp.pad(b_f[0], (0, LANES - ATT_HEADS)).reshape(1, LANES)
    ln_g, ln_b = row(ln_in_g), row(ln_in_b)

    tm = _pick_tile(seq, 512)
    q, k, v, u, ga, gb, fcum = _inproj(x2, ln_g, ln_b, w_main, w_f, b_f_pad, tm=tm, tiles_per_seq=seq // tm)
    _, k_m, v_m, u_m, _, _, fcum_m = _inproj(meta, ln_g, ln_b, w_main, w_f, b_f_pad, tm=N_META, tiles_per_seq=1)

    pad_m = META_KEYS_PAD - N_META
    fm = fcum_m[:, :ATT_HEADS]
    bias_m = jnp.pad(-LOG2E * (fm - fm[N_META - 1:N_META, :]), ((0, pad_m), (0, 0)), constant_values=-BIG)
    pieces_m = jnp.stack(_split3(bias_m), axis=-1)
    k_meta = jnp.pad(k_m, ((0, pad_m), (0, 0))).reshape(META_KEYS_PAD, ATT_HEADS, LANES)
    k_meta = k_meta.at[:, :, HEAD_DIM:HEAD_DIM + 3].set(pieces_m).reshape(META_KEYS_PAD, ATT_HEADS * LANES)
    v_meta_t = jnp.pad(v_m, ((0, pad_m), (0, 0))).T
    tq = _pick_tile(seq, 512)
    y_a = _attn(q.T, k, v.T, k_meta, v_meta_t, bsz=bsz, seq=seq, tq=tq).T

    chunk = SSM_CHUNK
    n_chunks = seq // chunk
    n_steps = max(1, (n_chunks - 1).bit_length())
    kt, wst, wout, wmeta, apow, d_tiled = _ssm_tables(
        ssm_a_re[0], ssm_a_im[0], ssm_log_dt[0], ssm_b_re[0], ssm_b_im[0], ssm_c_re[0], ssm_c_im[0], ssm_d[0],
        chunk=chunk, n_steps=n_steps)
    u_g = jnp.transpose(u.reshape(bsz * n_chunks, chunk, n_groups, SSM_GROUP), (2, 0, 1, 3))
    u_g = u_g.reshape(n_groups, bsz * n_chunks, chunk * SSM_GROUP)
    um_g = jnp.transpose(u_m.reshape(N_META, n_groups, SSM_GROUP), (1, 0, 2)).reshape(n_groups, 1, N_META * SSM_GROUP)
    um_g = jnp.pad(um_g, ((0, 0), (0, 15), (0, 0)))
    y_g = _ssm(u_g, kt, wst, wout, um_g, wmeta, apow, d_tiled, chunks_per_seq=n_chunks)
    y_b = jnp.transpose(y_g.reshape(n_groups, bsz * n_chunks, chunk, SSM_GROUP), (1, 2, 0, 3)).reshape(t, -1)

    wr = jnp.pad(w_router[0], ((0, 0), (0, LANES - n_experts)))
    wr_hi = wr.astype(BF16)
    wr_lo = (wr - wr_hi.astype(F32)).astype(BF16)
    b_r = jnp.pad(b_router[0], (0, LANES - n_experts)).reshape(1, LANES)
    tm2 = _pick_tile(t, 512)
    h1_rows, ridx, rw, counts = _postmix(
        x2, y_a, y_b, ga, gb, ln_g, ln_b, w_glu[0].astype(BF16), row(b_glu[0]),
        w_up_a[0].astype(BF16), w_up_b[0].astype(BF16), w_o[0].astype(BF16), row(ln1_g[0]), row(ln1_b[0]),
        wr_hi, wr_lo, b_r, tm=tm2, alpha=alpha, n_experts=n_experts)

    tm3 = MOE_ROW_TILE
    inv, tile_expert, n_valid = _routing_tables(ridx, counts, n_experts=n_experts, tm=tm3)
    y_slots = _moe(tile_expert, n_valid, inv, h1_rows, w_gate[0], b_gate[0][:, None, :], w_up[0],
                   b_up[0][:, None, :], w_down[0], b_down[0][:, None, :], tm=tm3)
    tm4 = _pick_tile(t, 256)
    out = _combine(y_slots, h1_rows, rw, row(ln2_g[0]), row(ln2_b[0]), tm=tm4, alpha=alpha)
    return out.reshape(bsz, seq, dm)
```

```python
import functools
import math

import jax
import jax.numpy as jnp
from jax import lax
from jax.experimental import pallas as pl
from jax.experimental.pallas import tpu as pltpu

F32 = jnp.float32
BF16 = jnp.bfloat16

N_META = 16
ATT_HEADS = 8
HEAD_DIM = 64
ATT_WIDTH = ATT_HEADS * HEAD_DIM
SSM_GROUP = 16
SSM_STATE = 64
TOP_K = 4
SWIGLU_LIMIT = 7.0
SWIGLU_ALPHA = 1.702
LN_EPS = 1e-5

LANES = 128
SSM_CHUNK = 64
META_KEYS_PAD = LANES
BIG = 1e30
LOG2E = 1.4426950408889634


def _layer_norm(x, g, b):
    mu = jnp.mean(x, axis=-1, keepdims=True)
    xc = x - mu
    var = jnp.mean(xc * xc, axis=-1, keepdims=True)
    return xc * lax.rsqrt(var + LN_EPS) * g + b


def _log_sigmoid(z):
    return jnp.minimum(z, 0.0) - jnp.log1p(jnp.exp(-jnp.abs(z)))


def _sigmoid(z):
    return 1.0 / (1.0 + jnp.exp(-z))


def _split3(x):
    hi = x.astype(BF16)
    r1 = x - hi.astype(F32)
    mid = r1.astype(BF16)
    lo = (r1 - mid.astype(F32)).astype(BF16)
    return hi, mid, lo


def _inproj_kernel(x_ref, g_ref, b_ref, w_ref, wf_ref, bf_ref,
                   q_ref, k_ref, v_ref, u_ref, ga_ref, gb_ref, fc_ref, carry_ref,
                   *, tiles_per_seq):
    i = pl.program_id(0)
    tm = x_ref.shape[0]
    h = _layer_norm(x_ref[...], g_ref[...], b_ref[...]).astype(BF16)

    def proj(lo, hi):
        return jnp.dot(h, w_ref[:, lo:hi], preferred_element_type=F32)

    aw = ATT_WIDTH
    u_ref[...] = proj(3 * aw, 4 * aw).astype(BF16)
    dm = ga_ref.shape[1]
    ga_ref[...] = _sigmoid(proj(4 * aw, 4 * aw + dm)).astype(BF16)
    gb_ref[...] = _sigmoid(proj(4 * aw + dm, 4 * aw + 2 * dm)).astype(BF16)

    zf = jnp.dot(h, wf_ref[...], preferred_element_type=F32) + bf_ref[...]
    lf = _log_sigmoid(zf)

    @pl.when(i % tiles_per_seq == 0)
    def _():
        carry_ref[...] = jnp.zeros_like(carry_ref)

    row = lax.broadcasted_iota(jnp.int32, (tm, tm), 0)
    col = lax.broadcasted_iota(jnp.int32, (tm, tm), 1)
    tri = (col <= row).astype(BF16)
    hi, mid, lo = _split3(lf)
    cs = (jnp.dot(tri, hi, preferred_element_type=F32)
          + jnp.dot(tri, mid, preferred_element_type=F32)
          + jnp.dot(tri, lo, preferred_element_type=F32)) + carry_ref[...]
    fc_ref[...] = cs
    carry_ref[...] = cs[tm - 1:tm, :]

    lane = lax.broadcasted_iota(jnp.int32, (1, LANES), 1)
    own = lane < HEAD_DIM
    bias_lanes = jnp.logical_and(lane >= HEAD_DIM, lane < HEAD_DIM + 3)
    pieces = [p.astype(F32) for p in _split3(-LOG2E * cs)]

    def slabs(x, extra):
        out = []
        for hp in range(ATT_HEADS // 2):
            x2 = x[:, LANES * hp:LANES * (hp + 1)]
            out.append(jnp.where(own, x2, extra(2 * hp)))
            out.append(jnp.where(own, pltpu.roll(x2, HEAD_DIM, axis=1), extra(2 * hp + 1)))
        return jnp.concatenate(out, axis=1).astype(BF16)

    def bias_extra(h):
        e = jnp.zeros((tm, LANES), F32)
        for j, piece in enumerate(pieces):
            e = jnp.where(lane == HEAD_DIM + j, pltpu.roll(piece, (HEAD_DIM + j - h) % LANES, axis=1), e)
        return e

    q_ref[...] = slabs(proj(0, aw) * (LOG2E / math.sqrt(HEAD_DIM)), lambda h: jnp.where(bias_lanes, 1.0, 0.0))
    k_ref[...] = slabs(proj(aw, 2 * aw), bias_extra)
    v_ref[...] = slabs(proj(2 * aw, 3 * aw), lambda h: jnp.ones((1, LANES), F32))


def _inproj(x2, ln_g, ln_b, w_main, w_f, b_f, *, tm, tiles_per_seq):
    t, dm = x2.shape
    aw = ATT_WIDTH
    n_main = w_main.shape[1]
    const = lambda i: (0, 0)
    rows = lambda i: (i, 0)
    slab = ATT_HEADS * LANES
    out_shape = [jax.ShapeDtypeStruct((t, slab), BF16)] * 3 + [jax.ShapeDtypeStruct((t, aw), BF16)] + \
                [jax.ShapeDtypeStruct((t, dm), BF16)] * 2 + \
                [jax.ShapeDtypeStruct((t, LANES), F32)]
    out_specs = [pl.BlockSpec((tm, slab), rows)] * 3 + [pl.BlockSpec((tm, aw), rows)] + \
                [pl.BlockSpec((tm, dm), rows)] * 2 + [pl.BlockSpec((tm, LANES), rows)]
    return pl.pallas_call(
        functools.partial(_inproj_kernel, tiles_per_seq=tiles_per_seq),
        grid=(t // tm,),
        in_specs=[pl.BlockSpec((tm, dm), rows),
                  pl.BlockSpec((1, dm), const), pl.BlockSpec((1, dm), const),
                  pl.BlockSpec((dm, n_main), const),
                  pl.BlockSpec((dm, LANES), const), pl.BlockSpec((1, LANES), const)],
        out_specs=out_specs,
        out_shape=out_shape,
        scratch_shapes=[pltpu.VMEM((1, LANES), F32)],
        compiler_params=pltpu.CompilerParams(
            dimension_semantics=("arbitrary",), vmem_limit_bytes=56 * 1024 * 1024),
        name="inproj",
    )(x2, ln_g, ln_b, w_main, w_f, b_f)


def _attn_kernel(qi_tab, ki_tab, qt_ref, k_ref, vt_ref, km_ref, vmt_ref, o_ref, m_ref, acc_ref, st_ref):
    p = pl.program_id(1)
    qi = qi_tab[p]
    ki = ki_tab[p]
    tq = qt_ref.shape[1]

    def process(k_blk, vt_blk, mask):
        n = k_blk.shape[0]

        def logits(h):
            sl = slice(LANES * h, LANES * (h + 1))
            st = jnp.dot(k_blk[:, sl], qt_ref[sl, :], preferred_element_type=F32)
            if mask is not None:
                st = jnp.where(mask, st, -jnp.inf)
            st_ref[h % 2, 0:n] = st
            m_old = m_ref[h]
            m_new = jnp.maximum(m_old, jnp.max(st_ref[h % 2, 0:n], axis=0, keepdims=True))
            m_ref[h] = m_new
            return m_old, m_new

        def weigh(h, m_old, m_new):
            sl = slice(LANES * h, LANES * (h + 1))
            pt = jnp.exp2(st_ref[h % 2, 0:n] - m_new[0:1, :])
            alpha = jnp.exp2(m_old - m_new)
            acc_ref[h] = alpha[0:1, :] * acc_ref[h] + jnp.dot(
                vt_blk[sl, :], pt.astype(BF16), preferred_element_type=F32)

        ms = logits(0)
        for h in range(ATT_HEADS):
            nxt = logits(h + 1) if h + 1 < ATT_HEADS else None
            weigh(h, *ms)
            ms = nxt

    @pl.when(ki == 0)
    def _():
        m_ref[...] = jnp.full_like(m_ref, -jnp.inf)
        acc_ref[...] = jnp.zeros_like(acc_ref)
        process(km_ref[...], vmt_ref[...], None)

    @pl.when(ki < qi)
    def _():
        process(k_ref[...], vt_ref[...], None)

    @pl.when(ki == qi)
    def _():
        tk = k_ref.shape[0]
        key = lax.broadcasted_iota(jnp.int32, (tk, tq), 0)
        qry = lax.broadcasted_iota(jnp.int32, (tk, tq), 1)
        process(k_ref[...], vt_ref[...], key <= qry)
        for h in range(ATT_HEADS):
            a = acc_ref[h]
            o_ref[HEAD_DIM * h:HEAD_DIM * (h + 1), :] = (a[:HEAD_DIM] / a[HEAD_DIM:]).astype(o_ref.dtype)


def _attn(q_t, k, v_t, k_meta, v_meta_t, *, bsz, seq, tq):
    nq = seq // tq
    pairs = [(a, b) for a in range(nq) for b in range(a + 1)]
    qi_tab = jnp.asarray([a for a, _ in pairs], jnp.int32)
    ki_tab = jnp.asarray([b for _, b in pairs], jnp.int32)
    w = k.shape[1]
    qcol = lambda b, p, qt, kt: (0, b * nq + qt[p])
    kcol = lambda b, p, qt, kt: (0, b * nq + kt[p])
    krow = lambda b, p, qt, kt: (b * nq + kt[p], 0)
    const = lambda b, p, qt, kt: (0, 0)
    grid_spec = pltpu.PrefetchScalarGridSpec(
        num_scalar_prefetch=2,
        grid=(bsz, len(pairs)),
        in_specs=[pl.BlockSpec((w, tq), qcol),
                  pl.BlockSpec((tq, w), krow),
                  pl.BlockSpec((w, tq), kcol),
                  pl.BlockSpec((META_KEYS_PAD, w), const),
                  pl.BlockSpec((w, META_KEYS_PAD), const)],
        out_specs=pl.BlockSpec((ATT_WIDTH, tq), qcol),
        scratch_shapes=[pltpu.VMEM((ATT_HEADS, 8, tq), F32),
                        pltpu.VMEM((ATT_HEADS, LANES, tq), F32),
                        pltpu.VMEM((2, tq, tq), F32)],
    )
    return pl.pallas_call(
        _attn_kernel,
        grid_spec=grid_spec,
        out_shape=jax.ShapeDtypeStruct((ATT_WIDTH, bsz * seq), BF16),
        compiler_params=pltpu.CompilerParams(
            dimension_semantics=("arbitrary", "arbitrary"), vmem_limit_bytes=48 * 1024 * 1024),
        name="fox_attention",
    )(qi_tab, ki_tab, q_t, k, v_t, k_meta, v_meta_t)


def _ssm_kernel(u_ref, kt_ref, wst_ref, wout_ref, um_ref, wm_ref, apow_ref, d_ref, y_ref, toep_ref,
                *, chunks_per_seq, n_steps):
    u = u_ref[0]
    rows = u.shape[0]
    p2 = 2 * SSM_STATE
    kt = kt_ref[0]
    n_c, tc = kt.shape
    per_vreg = LANES // n_c
    ext = jnp.concatenate([jnp.zeros_like(kt), kt], axis=1)
    for sub in range(per_vreg):
        rot = ext if sub == 0 else pltpu.roll(ext, n_c * sub, axis=1)
        for whole in range(tc // LANES):
            tp = per_vreg * whole + sub
            toep_ref[tp * n_c:(tp + 1) * n_c, :] = rot[:, tc - LANES * whole:2 * tc - LANES * whole].astype(BF16)
    y = jnp.dot(u, toep_ref[...], preferred_element_type=F32)
    s = jnp.dot(u, wst_ref[0], preferred_element_type=F32)
    x0 = jnp.dot(um_ref[0], wm_ref[0], preferred_element_type=F32)[0:1, :]

    lane = lax.broadcasted_iota(jnp.int32, (1, p2), 1)
    re_half = lane < SSM_STATE
    j = lax.broadcasted_iota(jnp.int32, (rows, 1), 0) % chunks_per_seq

    def cmul(step, z):
        ar = apow_ref[0, step:step + 1, :]
        ai = apow_ref[0, n_steps + step:n_steps + step + 1, :]
        return ar * z + ai * pltpu.roll(z, SSM_STATE, axis=1)

    s = s + jnp.where(j == 0, cmul(0, jnp.broadcast_to(x0, s.shape)), 0.0)
    for step in range(n_steps):
        sh = 1 << step
        prev = pltpu.roll(s, sh, axis=0)
        s = s + jnp.where(j >= sh, cmul(step, prev), 0.0)
    x_in = jnp.where(j == 0, x0, pltpu.roll(s, 1, axis=0))
    del re_half
    y = y + jnp.dot(x_in.astype(BF16), wout_ref[0], preferred_element_type=F32)
    y = y + d_ref[0] * u.astype(F32)
    y_ref[0] = jax.nn.gelu(y).astype(y_ref.dtype)


def _ssm(u_g, kt, wst, wout, u_meta, wmeta, apow, d_tiled, *, chunks_per_seq):
    g, rows, tc = u_g.shape
    n_steps = apow.shape[1] // 2
    grp = lambda i: (i, 0, 0)
    return pl.pallas_call(
        functools.partial(_ssm_kernel, chunks_per_seq=chunks_per_seq, n_steps=n_steps),
        grid=(g,),
        in_specs=[pl.BlockSpec((1, rows, tc), grp),
                  pl.BlockSpec((1,) + kt.shape[1:], grp),
                  pl.BlockSpec((1, tc, 2 * SSM_STATE), grp),
                  pl.BlockSpec((1, 2 * SSM_STATE, tc), grp),
                  pl.BlockSpec((1,) + u_meta.shape[1:], grp),
                  pl.BlockSpec((1,) + wmeta.shape[1:], grp),
                  pl.BlockSpec((1,) + apow.shape[1:], grp),
                  pl.BlockSpec((1, 1, tc), grp)],
        out_specs=pl.BlockSpec((1, rows, tc), grp),
        out_shape=jax.ShapeDtypeStruct((g, rows, tc), BF16),
        scratch_shapes=[pltpu.VMEM((tc, tc), BF16)],
        compiler_params=pltpu.CompilerParams(
            dimension_semantics=("arbitrary",), vmem_limit_bytes=48 * 1024 * 1024),
        name="s5_ssm",
    )(u_g, kt, wst, wout, u_meta, wmeta, apow, d_tiled)


def _ssm_tables(a_re, a_im, log_dt, b_re, b_im, c_re, c_im, d_skip, *, chunk, n_steps):
    g, p = a_re.shape
    c = b_re.shape[-1]
    dt = jnp.exp(log_dt)[:, None]
    mag = jnp.exp(a_re * dt)
    ang = a_im * dt
    lb_re, lb_im = mag * jnp.cos(ang), mag * jnp.sin(ang)
    den = a_re * a_re + a_im * a_im
    z_re, z_im = lb_re - 1.0, lb_im
    coef_re = (z_re * a_re + z_im * a_im) / den
    coef_im = (z_im * a_re - z_re * a_im) / den
    bb_re = coef_re[..., None] * b_re - coef_im[..., None] * b_im
    bb_im = coef_re[..., None] * b_im + coef_im[..., None] * b_re

    def step(carry, _):
        cr, ci = carry
        return (cr * lb_re - ci * lb_im, cr * lb_im + ci * lb_re), (cr, ci)
    _, (pw_re, pw_im) = lax.scan(step, (jnp.ones_like(lb_re), jnp.zeros_like(lb_re)), None,
                                 length=chunk + 1)

    e_re = c_re[None] * pw_re[:, :, None, :] - c_im[None] * pw_im[:, :, None, :]
    e_im = c_re[None] * pw_im[:, :, None, :] + c_im[None] * pw_re[:, :, None, :]
    kern = (jnp.einsum('tgcp,gpd->tgcd', e_re[:chunk], bb_re)
            - jnp.einsum('tgcp,gpd->tgcd', e_im[:chunk], bb_im))
    kt = jnp.transpose(kern, (1, 3, 0, 2)).reshape(g, c, chunk * c)

    def in_to_state(n):
        wr = pw_re[n - 1::-1][:n, :, :, None] * bb_re[None] - pw_im[n - 1::-1][:n, :, :, None] * bb_im[None]
        wi = pw_re[n - 1::-1][:n, :, :, None] * bb_im[None] + pw_im[n - 1::-1][:n, :, :, None] * bb_re[None]
        w = jnp.concatenate([wr, wi], axis=2)
        return jnp.transpose(w, (1, 0, 3, 2)).reshape(g, n * c, 2 * p)
    wst = in_to_state(chunk)
    wmeta = in_to_state(N_META)

    wout = jnp.concatenate([e_re[1:chunk + 1], -e_im[1:chunk + 1]], axis=3)
    wout = jnp.transpose(wout, (1, 3, 0, 2)).reshape(g, 2 * p, chunk * c)

    ar, ai = pw_re[chunk], pw_im[chunk]
    rows_r, rows_i = [], []
    for _ in range(n_steps):
        rows_r.append(jnp.concatenate([ar, ar], axis=-1))
        rows_i.append(jnp.concatenate([-ai, ai], axis=-1))
        ar, ai = ar * ar - ai * ai, 2.0 * ar * ai
    apow = jnp.stack(rows_r + rows_i, axis=1)
    d_tiled = jnp.tile(d_skip, (1, chunk))[:, None, :]
    return kt, wst.astype(BF16), wout.astype(BF16), wmeta.astype(BF16), apow, d_tiled


def _postmix_kernel(x_ref, ya_ref, yb_ref, ga_ref, gb_ref, lng_ref, lnb_ref,
                    wglu_ref, bglu_ref, wa_ref, wb_ref, wo_ref, l1g_ref, l1b_ref,
                    wrh_ref, wrl_ref, br_ref, h1_ref, ridx_ref, rw_ref, cnt_ref, carry_ref,
                    *, alpha, n_experts):
    i = pl.program_id(0)
    tm = x_ref.shape[0]
    halves = [slice(0, tm // 2), slice(tm // 2, tm)]
    dot = functools.partial(jnp.dot, preferred_element_type=F32)
    glu = [yb_ref[r, :].astype(F32) * _sigmoid(dot(yb_ref[r, :], wglu_ref[...]) + bglu_ref[...]) for r in halves]
    att = [ga_ref[r, :].astype(F32) * dot(ya_ref[r, :], wa_ref[...]) for r in halves]
    merged = [att[j] + gb_ref[r, :].astype(F32) * dot(glu[j].astype(BF16), wb_ref[...])
              for j, r in enumerate(halves)]
    mix = [dot(m.astype(BF16), wo_ref[...]) for m in merged]
    h1s = [_layer_norm(alpha * _layer_norm(x_ref[r, :], lng_ref[...], lnb_ref[...]) + mix[j],
                       l1g_ref[...], l1b_ref[...]) for j, r in enumerate(halves)]
    parts = []
    for j, r in enumerate(halves):
        h1_ref[r] = h1s[j].reshape((tm // 2,) + h1_ref.shape[1:])
        hb = h1s[j].astype(BF16)
        hl = (h1s[j] - hb.astype(F32)).astype(BF16)
        parts.append(dot(hb, wrh_ref[...]) + dot(hb, wrl_ref[...]) + dot(hl, wrh_ref[...]))
    logits = jnp.concatenate(parts, axis=0) + br_ref[...]
    lane = lax.broadcasted_iota(jnp.int32, logits.shape, 1)
    logits = jnp.where(lane < n_experts, logits, -jnp.inf)
    denom = jnp.zeros((tm, 1), F32)
    top = None
    hits, idxs, ws = [], [], []
    for _ in range(TOP_K):
        mx = jnp.max(logits, axis=-1, keepdims=True)
        idx = jnp.min(jnp.where(logits == mx, lane, LANES), axis=-1, keepdims=True)
        hit = lane == idx
        if top is None:
            top = mx
        w = jnp.exp(mx - top)
        denom = denom + w
        hits.append(hit)
        idxs.append(idx)
        ws.append(w)
        logits = jnp.where(hit, -jnp.inf, logits)

    @pl.when(i == 0)
    def _():
        carry_ref[...] = jnp.zeros_like(carry_ref)

    sel = jnp.zeros(logits.shape, F32)
    for hit in hits:
        sel = sel + jnp.where(hit, 1.0, 0.0)
    row = lax.broadcasted_iota(jnp.int32, (tm, tm), 0)
    col = lax.broadcasted_iota(jnp.int32, (tm, tm), 1)
    before = (col < row).astype(BF16)
    seen = jnp.dot(before, sel.astype(BF16), preferred_element_type=F32) + carry_ref[...]
    ridx = jnp.zeros(logits.shape, jnp.int32)
    rw = jnp.zeros(logits.shape, F32)
    for kk in range(TOP_K):
        rank = jnp.sum(jnp.where(hits[kk], seen, 0.0), axis=-1, keepdims=True).astype(jnp.int32)
        ridx = jnp.where(lane == kk, idxs[kk], ridx)
        ridx = jnp.where(lane == TOP_K + kk, rank, ridx)
        rw = jnp.where(lane == kk, ws[kk] / denom, rw)
    ridx_ref[...] = ridx
    rw_ref[...] = rw
    total = carry_ref[...] + jnp.sum(sel, axis=0, keepdims=True)
    carry_ref[...] = total
    cnt_ref[...] = jnp.broadcast_to(total, cnt_ref.shape)


def _postmix(x2, ya, yb, ga, gb, ln_g, ln_b, w_glu, b_glu, w_a, w_b, w_o, l1g, l1b,
             wr_hi, wr_lo, b_r, *, tm, alpha, n_experts):
    t, dm = x2.shape
    sw = ya.shape[1]
    rows = lambda i: (i, 0)
    const = lambda i: (0, 0)
    full = lambda a: pl.BlockSpec(a.shape, const)
    return pl.pallas_call(
        functools.partial(_postmix_kernel, alpha=alpha, n_experts=n_experts),
        grid=(t // tm,),
        in_specs=[pl.BlockSpec((tm, dm), rows), pl.BlockSpec((tm, sw), rows), pl.BlockSpec((tm, sw), rows),
                  pl.BlockSpec((tm, dm), rows), pl.BlockSpec((tm, dm), rows),
                  full(ln_g), full(ln_b), full(w_glu), full(b_glu), full(w_a), full(w_b), full(w_o),
                  full(l1g), full(l1b), full(wr_hi), full(wr_lo), full(b_r)],
        out_specs=[pl.BlockSpec((tm, 1, dm), lambda i: (i, 0, 0)), pl.BlockSpec((tm, LANES), rows),
                   pl.BlockSpec((tm, LANES), rows), pl.BlockSpec((8, LANES), const)],
        out_shape=[jax.ShapeDtypeStruct((t, 1, dm), F32), jax.ShapeDtypeStruct((t, LANES), jnp.int32),
                   jax.ShapeDtypeStruct((t, LANES), F32), jax.ShapeDtypeStruct((8, LANES), F32)],
        scratch_shapes=[pltpu.VMEM((1, LANES), F32)],
        compiler_params=pltpu.CompilerParams(
            dimension_semantics=("arbitrary",), vmem_limit_bytes=48 * 1024 * 1024),
        name="postmix_router",
    )(x2, ya, yb, ga, gb, ln_g, ln_b, w_glu, b_glu, w_a, w_b, w_o, l1g, l1b, wr_hi, wr_lo, b_r)


MOE_ROW_TILE = 512
MOE_COL_CHUNK = 256


def _moe_kernel(te_ref, nv_ref, inv_ref, h_hbm, wg_ref, bg_ref, wu_ref, bu_ref, wd_ref, bd_ref, o_hbm,
                xbuf_ref, ybuf_ref, x2_ref, xb_ref, act_ref, wgb_ref, wub_ref, wdb_ref, sem_g, sem_s,
                *, n_tokens):
    i = pl.program_id(0)
    n = pl.num_programs(0)
    _, tm, _, dm = xbuf_ref.shape
    cur = i % 2
    oth = (i + 1) % 2

    def start_gather(tile, slot):
        base = tile * tm

        def body(r, c):
            s = inv_ref[base + r]
            tok = s & (n_tokens - 1) if n_tokens & (n_tokens - 1) == 0 else s % n_tokens
            pltpu.make_async_copy(h_hbm.at[tok], xbuf_ref.at[slot, r], sem_g.at[slot]).start()
            return c
        lax.fori_loop(0, tm, body, 0, unroll=8)

    def wait_gather(slot):
        pltpu.make_async_copy(h_hbm.at[pl.ds(0, tm)], xbuf_ref.at[slot], sem_g.at[slot]).wait()

    def start_scatter(tile, slot):
        base = tile * tm

        def body(r, c):
            pltpu.make_async_copy(ybuf_ref.at[slot, r], o_hbm.at[inv_ref[base + r]], sem_s.at[slot]).start()
            return c
        lax.fori_loop(0, tm, body, 0, unroll=8)

    def wait_scatter(slot):
        pltpu.make_async_copy(ybuf_ref.at[slot], o_hbm.at[pl.ds(0, tm)], sem_s.at[slot]).wait()

    @pl.when(i == 0)
    def _():
        ybuf_ref[...] = jnp.zeros_like(ybuf_ref)
        start_gather(0, 0)

    valid = i < nv_ref[0]
    new_expert = jnp.logical_or(i == 0, te_ref[i] != te_ref[jnp.maximum(i - 1, 0)])

    @pl.when(jnp.logical_and(valid, new_expert))
    def _():
        wgb_ref[...] = wg_ref[0].astype(BF16)
        wub_ref[...] = wu_ref[0].astype(BF16)
        wdb_ref[...] = wd_ref[0].astype(BF16)

    def row_copies():
        wait_gather(cur)
        start_gather(jnp.minimum(i + 1, n - 1), oth)
        start_scatter(jnp.maximum(i - 1, 0), oth)

    def start_gather_group(g, n_groups):
        base = jnp.minimum(i + 1, n - 1) * tm
        per = tm // n_groups
        for r in range(g * per, (g + 1) * per):
            s = inv_ref[base + r]
            tok = s & (n_tokens - 1) if n_tokens & (n_tokens - 1) == 0 else s % n_tokens
            pltpu.make_async_copy(h_hbm.at[tok], xbuf_ref.at[oth, r], sem_g.at[oth]).start()

    def start_scatter_group(g, n_groups):
        base = jnp.maximum(i - 1, 0) * tm
        per = tm // n_groups
        for r in range(g * per, (g + 1) * per):
            pltpu.make_async_copy(ybuf_ref.at[oth, r], o_hbm.at[inv_ref[base + r]], sem_s.at[oth]).start()

    @pl.when(valid)
    def _():
        de = wgb_ref.shape[1]
        nc_up, nc_down = de // MOE_COL_CHUNK, dm // MOE_COL_CHUNK
        wait_gather(cur)
        x2_ref[...] = xbuf_ref[cur].reshape(x2_ref.shape)
        xb_ref[...] = x2_ref[...].astype(BF16)
        for c in range(nc_up):
            start_gather_group(c, nc_up)
            cols = slice(c * MOE_COL_CHUNK, (c + 1) * MOE_COL_CHUNK)
            gate = jnp.minimum(jnp.dot(xb_ref[...], wgb_ref[:, cols], preferred_element_type=F32)
                               + bg_ref[0][:, cols], SWIGLU_LIMIT)
            up = jnp.clip(jnp.dot(xb_ref[...], wub_ref[:, cols], preferred_element_type=F32)
                          + bu_ref[0][:, cols], -SWIGLU_LIMIT, SWIGLU_LIMIT)
            act_ref[:, cols] = ((up + 1.0) * gate * _sigmoid(SWIGLU_ALPHA * gate)).astype(BF16)
        for c in range(nc_down):
            start_scatter_group(c, nc_down)
            cols = slice(c * MOE_COL_CHUNK, (c + 1) * MOE_COL_CHUNK)
            x2_ref[:, cols] = jnp.dot(act_ref[...], wdb_ref[:, cols], preferred_element_type=F32) \
                + bd_ref[0][:, cols]

        @pl.when(i == 0)
        def _():
            wait_scatter(oth)

        @pl.when(i >= 2)
        def _():
            wait_scatter(cur)
        ybuf_ref[cur] = x2_ref[...].reshape(tm, 1, dm)

    @pl.when(jnp.logical_not(valid))
    def _():
        @pl.when(i >= 2)
        def _():
            wait_scatter(cur)
        row_copies()

    @pl.when(i == n - 1)
    def _():
        wait_gather(oth)
        wait_scatter(oth)
        start_scatter(i, cur)
        wait_scatter(cur)


def _moe(tile_expert, n_valid, inv, h1_rows, w_gate, b_gate, w_up, b_up, w_down, b_down, *, tm):
    t, _, dm = h1_rows.shape
    _, _, de = w_gate.shape
    n_rows = inv.shape[0]
    per_e = lambda i, te, nv, iv: (te[i], 0, 0)
    return pl.pallas_call(
        functools.partial(_moe_kernel, n_tokens=t),
        grid_spec=pltpu.PrefetchScalarGridSpec(
            num_scalar_prefetch=3, grid=(n_rows // tm,),
            in_specs=[pl.BlockSpec(memory_space=pl.ANY),
                      pl.BlockSpec((1, dm, de), per_e), pl.BlockSpec((1, 1, de), per_e),
                      pl.BlockSpec((1, dm, de), per_e), pl.BlockSpec((1, 1, de), per_e),
                      pl.BlockSpec((1, de, dm), per_e), pl.BlockSpec((1, 1, dm), per_e)],
            out_specs=pl.BlockSpec(memory_space=pl.ANY),
            scratch_shapes=[pltpu.VMEM((2, tm, 1, dm), F32), pltpu.VMEM((2, tm, 1, dm), F32),
                            pltpu.VMEM((tm, dm), F32), pltpu.VMEM((tm, dm), BF16), pltpu.VMEM((tm, de), BF16),
                            pltpu.VMEM((dm, de), BF16),
                            pltpu.VMEM((dm, de), BF16), pltpu.VMEM((de, dm), BF16),
                            pltpu.SemaphoreType.DMA((2,)), pltpu.SemaphoreType.DMA((2,))]),
        out_shape=jax.ShapeDtypeStruct((n_rows, 1, dm), F32),
        compiler_params=pltpu.CompilerParams(
            dimension_semantics=("arbitrary",), vmem_limit_bytes=56 * 1024 * 1024),
        name="moe_experts",
    )(tile_expert, n_valid, inv, h1_rows, w_gate, b_gate, w_up, b_up, w_down, b_down)


def _combine_kernel(*refs, alpha):
    y_refs = refs[:TOP_K]
    h1_ref, rw_ref, l2g_ref, l2b_ref, o_ref, rows2_ref = refs[TOP_K:]
    tm, dm = o_ref.shape
    rw = rw_ref[...]
    ffn = jnp.zeros((tm, dm), F32)
    for kk in range(TOP_K):
        rows2_ref[...] = y_refs[kk][...].reshape(tm, dm)
        ffn = ffn + rw[:, kk:kk + 1] * rows2_ref[...]
    rows2_ref[...] = h1_ref[...].reshape(tm, dm)
    o_ref[...] = _layer_norm(alpha * rows2_ref[...] + ffn, l2g_ref[...], l2b_ref[...])


def _combine(y_slots, h1_rows, rw, l2g, l2b, *, tm, alpha):
    t, _, dm = h1_rows.shape
    rows = lambda i: (i, 0)
    const = lambda i: (0, 0)
    slot_specs = [pl.BlockSpec((tm, 1, dm), functools.partial(lambda i, kk: (kk * (t // tm) + i, 0, 0), kk=kk))
                  for kk in range(TOP_K)]
    return pl.pallas_call(
        functools.partial(_combine_kernel, alpha=alpha),
        grid=(t // tm,),
        in_specs=slot_specs + [pl.BlockSpec((tm, 1, dm), lambda i: (i, 0, 0)),
                               pl.BlockSpec((tm, LANES), rows),
                               pl.BlockSpec((1, dm), const), pl.BlockSpec((1, dm), const)],
        out_specs=pl.BlockSpec((tm, dm), rows),
        out_shape=jax.ShapeDtypeStruct((t, dm), F32),
        scratch_shapes=[pltpu.VMEM((tm, dm), F32)],
        compiler_params=pltpu.CompilerParams(
            dimension_semantics=("arbitrary",), vmem_limit_bytes=48 * 1024 * 1024),
        name="moe_combine_ln2",
    )(*([y_slots] * TOP_K), h1_rows, rw, l2g, l2b)


INVERT_CHUNK = 8192


def _invert_kernel(rows_ref, slots_ref, inv_ref):
    def body(p, c):
        inv_ref[rows_ref[p]] = slots_ref[p]
        return c
    lax.fori_loop(0, rows_ref.shape[0], body, 0, unroll=8)


def _invert(rows, slots):
    n = rows.shape[0]
    ch = math.gcd(n, INVERT_CHUNK)
    chunk = pl.BlockSpec((ch,), lambda i: (i,), memory_space=pltpu.SMEM)
    return pl.pallas_call(
        _invert_kernel,
        grid=(n // ch,),
        in_specs=[chunk, chunk],
        out_specs=pl.BlockSpec(memory_space=pltpu.SMEM),
        out_shape=jax.ShapeDtypeStruct((n,), jnp.int32),
        compiler_params=pltpu.CompilerParams(dimension_semantics=("arbitrary",)),
        name="moe_row_slots",
    )(rows, slots)


def _routing_tables(ridx, counts, *, n_experts, tm):
    t = ridx.shape[0]
    n_real = TOP_K * t
    n_fill = n_experts * tm
    e_idx = ridx[:, :TOP_K]
    rank = ridx[:, TOP_K:2 * TOP_K]
    cnt = counts[0, :n_experts].astype(jnp.int32)
    padded = (cnt + tm - 1) // tm * tm
    ends = jnp.cumsum(padded)
    offs = ends - padded
    experts = jnp.arange(n_experts, dtype=jnp.int32)
    base = jnp.sum(jnp.where(e_idx[..., None] == experts, offs, 0), axis=-1)
    token_rows = (base + rank).reshape(-1)
    token_slots = (jnp.arange(TOP_K, dtype=jnp.int32)[None, :] * t
                   + jnp.arange(t, dtype=jnp.int32)[:, None]).reshape(-1)
    fill_cnt = padded - cnt
    fill_end = jnp.cumsum(fill_cnt)
    shift = jnp.concatenate([offs + cnt - (fill_end - fill_cnt), jnp.full((1,), n_real, jnp.int32)])
    j = jnp.arange(n_fill, dtype=jnp.int32)
    fill_rows = j + shift[0] + jnp.sum(
        jnp.where(j[:, None] >= fill_end[None, :], (shift[1:] - shift[:-1])[None, :], 0), axis=1)
    rows = jnp.concatenate([token_rows, fill_rows])
    slots = jnp.concatenate([token_slots, n_real + j])
    inv = _invert(rows, slots)
    n_tiles = (n_real + n_fill) // tm
    tile_start = jnp.arange(n_tiles, dtype=jnp.int32) * tm
    tile_expert = jnp.minimum(jnp.sum(ends[None, :] <= tile_start[:, None], axis=1), n_experts - 1)
    n_valid = (ends[-1] // tm).reshape(1)
    return inv.astype(jnp.int32), tile_expert.astype(jnp.int32), n_valid.astype(jnp.int32)


def _pick_tile(n, want):
    t = min(n, want)
    assert n % t == 0, (n, t)
    return t


def kernel(x, meta, ln_in_g, ln_in_b, w_in, b_f, w_up_a, w_up_b, w_o, ssm_a_re, ssm_a_im, ssm_log_dt,
           ssm_b_re, ssm_b_im, ssm_c_re, ssm_c_im, ssm_d, w_glu, b_glu, ln1_g, ln1_b, w_router, b_router,
           w_gate, b_gate, w_up, b_up, w_down, b_down, ln2_g, ln2_b):
    bsz, seq, dm = x.shape
    depth = w_in.shape[0]
    assert depth == 1 and meta.shape[0] == N_META
    alpha = (2.0 * depth) ** 0.25
    n_groups = ssm_a_re.shape[1]
    n_experts = w_router.shape[-1]
    aw = ATT_WIDTH
    t = bsz * seq
    x2 = x.reshape(t, dm)
    row = lambda a: a.reshape(1, -1)

    f_off = 3 * aw
    u_off = f_off + ATT_HEADS
    w0 = w_in[0]
    w_main = jnp.concatenate([w0[:, :f_off], w0[:, u_off:]], axis=1).astype(BF16)
    w_f = jnp.pad(w0[:, f_off:u_off], ((0, 0), (0, LANES - ATT_HEADS))).astype(BF16)
    b_f_pad = jnp.pad(b_f[0], (0, LANES - ATT_HEADS)).reshape(1, LANES)
    ln_g, ln_b = row(ln_in_g), row(ln_in_b)

    tm = _pick_tile(seq, 512)
    q, k, v, u, ga, gb, fcum = _inproj(x2, ln_g, ln_b, w_main, w_f, b_f_pad, tm=tm, tiles_per_seq=seq // tm)
    _, k_m, v_m, u_m, _, _, fcum_m = _inproj(meta, ln_g, ln_b, w_main, w_f, b_f_pad, tm=N_META, tiles_per_seq=1)

    pad_m = META_KEYS_PAD - N_META
    fm = fcum_m[:, :ATT_HEADS]
    bias_m = jnp.pad(-LOG2E * (fm - fm[N_META - 1:N_META, :]), ((0, pad_m), (0, 0)), constant_values=-BIG)
    pieces_m = jnp.stack(_split3(bias_m), axis=-1)
    k_meta = jnp.pad(k_m, ((0, pad_m), (0, 0))).reshape(META_KEYS_PAD, ATT_HEADS, LANES)
    k_meta = k_meta.at[:, :, HEAD_DIM:HEAD_DIM + 3].set(pieces_m).reshape(META_KEYS_PAD, ATT_HEADS * LANES)
    v_meta_t = jnp.pad(v_m, ((0, pad_m), (0, 0))).T
    tq = _pick_tile(seq, 512)
    y_a = _attn(q.T, k, v.T, k_meta, v_meta_t, bsz=bsz, seq=seq, tq=tq).T

    chunk = SSM_CHUNK
    n_chunks = seq // chunk
    n_steps = max(1, (n_chunks - 1).bit_length())
    kt, wst, wout, wmeta, apow, d_tiled = _ssm_tables(
        ssm_a_re[0], ssm_a_im[0], ssm_log_dt[0], ssm_b_re[0], ssm_b_im[0], ssm_c_re[0], ssm_c_im[0], ssm_d[0],
        chunk=chunk, n_steps=n_steps)
    u_g = jnp.transpose(u.reshape(bsz * n_chunks, chunk, n_groups, SSM_GROUP), (2, 0, 1, 3))
    u_g = u_g.reshape(n_groups, bsz * n_chunks, chunk * SSM_GROUP)
    um_g = jnp.transpose(u_m.reshape(N_META, n_groups, SSM_GROUP), (1, 0, 2)).reshape(n_groups, 1, N_META * SSM_GROUP)
    um_g = jnp.pad(um_g, ((0, 0), (0, 15), (0, 0)))
    y_g = _ssm(u_g, kt, wst, wout, um_g, wmeta, apow, d_tiled, chunks_per_seq=n_chunks)
    y_b = jnp.transpose(y_g.reshape(n_groups, bsz * n_chunks, chunk, SSM_GROUP), (1, 2, 0, 3)).reshape(t, -1)

    wr = jnp.pad(w_router[0], ((0, 0), (0, LANES - n_experts)))
    wr_hi = wr.astype(BF16)
    wr_lo = (wr - wr_hi.astype(F32)).astype(BF16)
    b_r = jnp.pad(b_router[0], (0, LANES - n_experts)).reshape(1, LANES)
    tm2 = _pick_tile(t, 512)
    h1_rows, ridx, rw, counts = _postmix(
        x2, y_a, y_b, ga, gb, ln_g, ln_b, w_glu[0].astype(BF16), row(b_glu[0]),
        w_up_a[0].astype(BF16), w_up_b[0].astype(BF16), w_o[0].astype(BF16), row(ln1_g[0]), row(ln1_b[0]),
        wr_hi, wr_lo, b_r, tm=tm2, alpha=alpha, n_experts=n_experts)

    tm3 = MOE_ROW_TILE
    inv, tile_expert, n_valid = _routing_tables(ridx, counts, n_experts=n_experts, tm=tm3)
    y_slots = _moe(tile_expert, n_valid, inv, h1_rows, w_gate[0], b_gate[0][:, None, :], w_up[0],
                   b_up[0][:, None, :], w_down[0], b_down[0][:, None, :], tm=tm3)
    tm4 = _pick_tile(t, 256)
    out = _combine(y_slots, h1_rows, rw, row(ln2_g[0]), row(ln2_b[0]), tm=tm4, alpha=alpha)
    return out.reshape(bsz, seq, dm)
```

```python
import functools
import math

import jax
import jax.numpy as jnp
from jax import lax
from jax.experimental import pallas as pl
from jax.experimental.pallas import tpu as pltpu

F32 = jnp.float32
BF16 = jnp.bfloat16

N_META = 16
ATT_HEADS = 8
HEAD_DIM = 64
ATT_WIDTH = ATT_HEADS * HEAD_DIM
SSM_GROUP = 16
SSM_STATE = 64
TOP_K = 4
SWIGLU_LIMIT = 7.0
SWIGLU_ALPHA = 1.702
LN_EPS = 1e-5

LANES = 128
SSM_CHUNK = 64
META_KEYS_PAD = LANES
BIG = 1e30
LOG2E = 1.4426950408889634


def _layer_norm(x, g, b):
    mu = jnp.mean(x, axis=-1, keepdims=True)
    xc = x - mu
    var = jnp.mean(xc * xc, axis=-1, keepdims=True)
    return xc * lax.rsqrt(var + LN_EPS) * g + b


def _log_sigmoid(z):
    return jnp.minimum(z, 0.0) - jnp.log1p(jnp.exp(-jnp.abs(z)))


def _sigmoid(z):
    return 1.0 / (1.0 + jnp.exp(-z))


def _split3(x):
    hi = x.astype(BF16)
    r1 = x - hi.astype(F32)
    mid = r1.astype(BF16)
    lo = (r1 - mid.astype(F32)).astype(BF16)
    return hi, mid, lo


def _inproj_kernel(x_ref, g_ref, b_ref, w_ref, wf_ref, bf_ref,
                   q_ref, k_ref, v_ref, u_ref, ga_ref, gb_ref, fc_ref, carry_ref,
                   *, tiles_per_seq, transpose_qv):
    i = pl.program_id(0)
    tm = x_ref.shape[0]
    h = _layer_norm(x_ref[...], g_ref[...], b_ref[...]).astype(BF16)

    def proj(lo, hi):
        return jnp.dot(h, w_ref[:, lo:hi], preferred_element_type=F32)

    aw = ATT_WIDTH
    u_ref[...] = proj(3 * aw, 4 * aw).astype(BF16)
    dm = ga_ref.shape[1]
    ga_ref[...] = _sigmoid(proj(4 * aw, 4 * aw + dm)).astype(BF16)
    gb_ref[...] = _sigmoid(proj(4 * aw + dm, 4 * aw + 2 * dm)).astype(BF16)

    zf = jnp.dot(h, wf_ref[...], preferred_element_type=F32) + bf_ref[...]
    lf = _log_sigmoid(zf)

    @pl.when(i % tiles_per_seq == 0)
    def _():
        carry_ref[...] = jnp.zeros_like(carry_ref)

    row = lax.broadcasted_iota(jnp.int32, (tm, tm), 0)
    col = lax.broadcasted_iota(jnp.int32, (tm, tm), 1)
    tri = (col <= row).astype(BF16)
    hi, mid, lo = _split3(lf)
    cs = (jnp.dot(tri, hi, preferred_element_type=F32)
          + jnp.dot(tri, mid, preferred_element_type=F32)
          + jnp.dot(tri, lo, preferred_element_type=F32)) + carry_ref[...]
    fc_ref[...] = cs
    carry_ref[...] = cs[tm - 1:tm, :]

    lane = lax.broadcasted_iota(jnp.int32, (1, LANES), 1)
    own = lane < HEAD_DIM
    bias_lanes = jnp.logical_and(lane >= HEAD_DIM, lane < HEAD_DIM + 3)
    pieces = [p.astype(F32) for p in _split3(-LOG2E * cs)]

    def slabs(x, extra, transposed=False):
        out = []
        for hp in range(ATT_HEADS // 2):
            x2 = x[:, LANES * hp:LANES * (hp + 1)]
            out.append(jnp.where(own, x2, extra(2 * hp)))
            out.append(jnp.where(own, pltpu.roll(x2, HEAD_DIM, axis=1), extra(2 * hp + 1)))
        if transposed:
            out = [o.T for o in out]
            return jnp.concatenate(out, axis=0).astype(BF16)
        return jnp.concatenate(out, axis=1).astype(BF16)

    def bias_extra(h):
        e = jnp.zeros((tm, LANES), F32)
        for j, piece in enumerate(pieces):
            e = jnp.where(lane == HEAD_DIM + j, pltpu.roll(piece, (HEAD_DIM + j - h) % LANES, axis=1), e)
        return e

    q_ref[...] = slabs(proj(0, aw) * (LOG2E / math.sqrt(HEAD_DIM)),
                       lambda h: jnp.broadcast_to(jnp.where(bias_lanes, 1.0, 0.0), (tm, LANES)), transpose_qv)
    k_ref[...] = slabs(proj(aw, 2 * aw), bias_extra)
    v_ref[...] = slabs(proj(2 * aw, 3 * aw), lambda h: jnp.ones((tm, LANES), F32), transpose_qv)


def _inproj(x2, ln_g, ln_b, w_main, w_f, b_f, *, tm, tiles_per_seq, transpose_qv):
    t, dm = x2.shape
    aw = ATT_WIDTH
    n_main = w_main.shape[1]
    const = lambda i: (0, 0)
    rows = lambda i: (i, 0)
    slab = ATT_HEADS * LANES
    row_slab = (jax.ShapeDtypeStruct((t, slab), BF16), pl.BlockSpec((tm, slab), rows))
    col_slab = (jax.ShapeDtypeStruct((slab, t), BF16), pl.BlockSpec((slab, tm), lambda i: (0, i)))
    qv = col_slab if transpose_qv else row_slab
    out_shape = [qv[0], row_slab[0], qv[0], jax.ShapeDtypeStruct((t, aw), BF16)] + \
                [jax.ShapeDtypeStruct((t, dm), BF16)] * 2 + [jax.ShapeDtypeStruct((t, LANES), F32)]
    out_specs = [qv[1], row_slab[1], qv[1], pl.BlockSpec((tm, aw), rows)] + \
                [pl.BlockSpec((tm, dm), rows)] * 2 + [pl.BlockSpec((tm, LANES), rows)]
    return pl.pallas_call(
        functools.partial(_inproj_kernel, tiles_per_seq=tiles_per_seq, transpose_qv=transpose_qv),
        grid=(t // tm,),
        in_specs=[pl.BlockSpec((tm, dm), rows),
                  pl.BlockSpec((1, dm), const), pl.BlockSpec((1, dm), const),
                  pl.BlockSpec((dm, n_main), const),
                  pl.BlockSpec((dm, LANES), const), pl.BlockSpec((1, LANES), const)],
        out_specs=out_specs,
        out_shape=out_shape,
        scratch_shapes=[pltpu.VMEM((1, LANES), F32)],
        compiler_params=pltpu.CompilerParams(
            dimension_semantics=("arbitrary",), vmem_limit_bytes=56 * 1024 * 1024),
        name="inproj",
    )(x2, ln_g, ln_b, w_main, w_f, b_f)


def _attn_kernel(qi_tab, ki_tab, qt_ref, k_ref, vt_ref, km_ref, vmt_ref, o_ref, m_ref, acc_ref, st_ref):
    p = pl.program_id(1)
    qi = qi_tab[p]
    ki = ki_tab[p]
    tq = qt_ref.shape[1]

    def process(k_blk, vt_blk, mask):
        n = k_blk.shape[0]

        def logits(h):
            sl = slice(LANES * h, LANES * (h + 1))
            st = jnp.dot(k_blk[:, sl], qt_ref[sl, :], preferred_element_type=F32)
            if mask is not None:
                st = jnp.where(mask, st, -jnp.inf)
            st_ref[h % 2, 0:n] = st
            m_old = m_ref[h]
            m_new = jnp.maximum(m_old, jnp.max(st_ref[h % 2, 0:n], axis=0, keepdims=True))
            m_ref[h] = m_new
            return m_old, m_new

        def weigh(h, m_old, m_new):
            sl = slice(LANES * h, LANES * (h + 1))
            pt = jnp.exp2(st_ref[h % 2, 0:n] - m_new[0:1, :])
            alpha = jnp.exp2(m_old - m_new)
            acc_ref[h] = alpha[0:1, :] * acc_ref[h] + jnp.dot(
                vt_blk[sl, :], pt.astype(BF16), preferred_element_type=F32)

        ms = logits(0)
        for h in range(ATT_HEADS):
            nxt = logits(h + 1) if h + 1 < ATT_HEADS else None
            weigh(h, *ms)
            ms = nxt

    @pl.when(ki == 0)
    def _():
        m_ref[...] = jnp.full_like(m_ref, -jnp.inf)
        acc_ref[...] = jnp.zeros_like(acc_ref)
        process(km_ref[...], vmt_ref[...], None)

    @pl.when(ki < qi)
    def _():
        process(k_ref[...], vt_ref[...], None)

    @pl.when(ki == qi)
    def _():
        tk = k_ref.shape[0]
        key = lax.broadcasted_iota(jnp.int32, (tk, tq), 0)
        qry = lax.broadcasted_iota(jnp.int32, (tk, tq), 1)
        process(k_ref[...], vt_ref[...], key <= qry)
        for hp in range(ATT_HEADS // 2):
            a0, a1 = acc_ref[2 * hp], acc_ref[2 * hp + 1]
            pair = jnp.concatenate([a0[:HEAD_DIM] / a0[HEAD_DIM:], a1[:HEAD_DIM] / a1[HEAD_DIM:]], axis=0)
            o_ref[:, LANES * hp:LANES * (hp + 1)] = pair.T.astype(o_ref.dtype)


def _attn(q_t, k, v_t, k_meta, v_meta_t, *, bsz, seq, tq):
    nq = seq // tq
    pairs = [(a, b) for a in range(nq) for b in range(a + 1)]
    qi_tab = jnp.asarray([a for a, _ in pairs], jnp.int32)
    ki_tab = jnp.asarray([b for _, b in pairs], jnp.int32)
    w = k.shape[1]
    qcol = lambda b, p, qt, kt: (0, b * nq + qt[p])
    kcol = lambda b, p, qt, kt: (0, b * nq + kt[p])
    krow = lambda b, p, qt, kt: (b * nq + kt[p], 0)
    const = lambda b, p, qt, kt: (0, 0)
    grid_spec = pltpu.PrefetchScalarGridSpec(
        num_scalar_prefetch=2,
        grid=(bsz, len(pairs)),
        in_specs=[pl.BlockSpec((w, tq), qcol),
                  pl.BlockSpec((tq, w), krow),
                  pl.BlockSpec((w, tq), kcol),
                  pl.BlockSpec((META_KEYS_PAD, w), const),
                  pl.BlockSpec((w, META_KEYS_PAD), const)],
        out_specs=pl.BlockSpec((tq, ATT_WIDTH), lambda b, p, qt, kt: (b * nq + qt[p], 0)),
        scratch_shapes=[pltpu.VMEM((ATT_HEADS, 8, tq), F32),
                        pltpu.VMEM((ATT_HEADS, LANES, tq), F32),
                        pltpu.VMEM((2, tq, tq), F32)],
    )
    return pl.pallas_call(
        _attn_kernel,
        grid_spec=grid_spec,
        out_shape=jax.ShapeDtypeStruct((bsz * seq, ATT_WIDTH), BF16),
        compiler_params=pltpu.CompilerParams(
            dimension_semantics=("arbitrary", "arbitrary"), vmem_limit_bytes=48 * 1024 * 1024),
        name="fox_attention",
    )(qi_tab, ki_tab, q_t, k, v_t, k_meta, v_meta_t)


def _ssm_kernel(u_ref, kt_ref, wst_ref, wout_ref, um_ref, wm_ref, apow_ref, d_ref, y_ref, toep_ref,
                *, chunks_per_seq, n_steps):
    u = u_ref[0]
    rows = u.shape[0]
    p2 = 2 * SSM_STATE
    kt = kt_ref[0]
    n_c, tc = kt.shape
    per_vreg = LANES // n_c
    ext = jnp.concatenate([jnp.zeros_like(kt), kt], axis=1)
    for sub in range(per_vreg):
        rot = ext if sub == 0 else pltpu.roll(ext, n_c * sub, axis=1)
        for whole in range(tc // LANES):
            tp = per_vreg * whole + sub
            toep_ref[tp * n_c:(tp + 1) * n_c, :] = rot[:, tc - LANES * whole:2 * tc - LANES * whole].astype(BF16)
    y = jnp.dot(u, toep_ref[...], preferred_element_type=F32)
    s = jnp.dot(u, wst_ref[0], preferred_element_type=F32)
    x0 = jnp.dot(um_ref[0], wm_ref[0], preferred_element_type=F32)[0:1, :]

    lane = lax.broadcasted_iota(jnp.int32, (1, p2), 1)
    re_half = lane < SSM_STATE
    j = lax.broadcasted_iota(jnp.int32, (rows, 1), 0) % chunks_per_seq

    def cmul(step, z):
        ar = apow_ref[0, step:step + 1, :]
        ai = apow_ref[0, n_steps + step:n_steps + step + 1, :]
        return ar * z + ai * pltpu.roll(z, SSM_STATE, axis=1)

    s = s + jnp.where(j == 0, cmul(0, jnp.broadcast_to(x0, s.shape)), 0.0)
    for step in range(n_steps):
        sh = 1 << step
        prev = pltpu.roll(s, sh, axis=0)
        s = s + jnp.where(j >= sh, cmul(step, prev), 0.0)
    x_in = jnp.where(j == 0, x0, pltpu.roll(s, 1, axis=0))
    del re_half
    y = y + jnp.dot(x_in.astype(BF16), wout_ref[0], preferred_element_type=F32)
    y = y + d_ref[0] * u.astype(F32)
    y_ref[0] = jax.nn.gelu(y).astype(y_ref.dtype)


def _ssm(u_g, kt, wst, wout, u_meta, wmeta, apow, d_tiled, *, chunks_per_seq):
    g, rows, tc = u_g.shape
    n_steps = apow.shape[1] // 2
    grp = lambda i: (i, 0, 0)
    return pl.pallas_call(
        functools.partial(_ssm_kernel, chunks_per_seq=chunks_per_seq, n_steps=n_steps),
        grid=(g,),
        in_specs=[pl.BlockSpec((1, rows, tc), grp),
                  pl.BlockSpec((1,) + kt.shape[1:], grp),
                  pl.BlockSpec((1, tc, 2 * SSM_STATE), grp),
                  pl.BlockSpec((1, 2 * SSM_STATE, tc), grp),
                  pl.BlockSpec((1,) + u_meta.shape[1:], grp),
                  pl.BlockSpec((1,) + wmeta.shape[1:], grp),
                  pl.BlockSpec((1,) + apow.shape[1:], grp),
                  pl.BlockSpec((1, 1, tc), grp)],
        out_specs=pl.BlockSpec((1, rows, tc), grp),
        out_shape=jax.ShapeDtypeStruct((g, rows, tc), BF16),
        scratch_shapes=[pltpu.VMEM((tc, tc), BF16)],
        compiler_params=pltpu.CompilerParams(
            dimension_semantics=("arbitrary",), vmem_limit_bytes=48 * 1024 * 1024),
        name="s5_ssm",
    )(u_g, kt, wst, wout, u_meta, wmeta, apow, d_tiled)


def _ssm_tables(a_re, a_im, log_dt, b_re, b_im, c_re, c_im, d_skip, *, chunk, n_steps):
    g, p = a_re.shape
    c = b_re.shape[-1]
    dt = jnp.exp(log_dt)[:, None]
    mag = jnp.exp(a_re * dt)
    ang = a_im * dt
    lb_re, lb_im = mag * jnp.cos(ang), mag * jnp.sin(ang)
    den = a_re * a_re + a_im * a_im
    z_re, z_im = lb_re - 1.0, lb_im
    coef_re = (z_re * a_re + z_im * a_im) / den
    coef_im = (z_im * a_re - z_re * a_im) / den
    bb_re = coef_re[..., None] * b_re - coef_im[..., None] * b_im
    bb_im = coef_re[..., None] * b_im + coef_im[..., None] * b_re

    def step(carry, _):
        cr, ci = carry
        return (cr * lb_re - ci * lb_im, cr * lb_im + ci * lb_re), (cr, ci)
    _, (pw_re, pw_im) = lax.scan(step, (jnp.ones_like(lb_re), jnp.zeros_like(lb_re)), None,
                                 length=chunk + 1)

    e_re = c_re[None] * pw_re[:, :, None, :] - c_im[None] * pw_im[:, :, None, :]
    e_im = c_re[None] * pw_im[:, :, None, :] + c_im[None] * pw_re[:, :, None, :]
    kern = (jnp.einsum('tgcp,gpd->tgcd', e_re[:chunk], bb_re)
            - jnp.einsum('tgcp,gpd->tgcd', e_im[:chunk], bb_im))
    kt = jnp.transpose(kern, (1, 3, 0, 2)).reshape(g, c, chunk * c)

    def in_to_state(n):
        wr = pw_re[n - 1::-1][:n, :, :, None] * bb_re[None] - pw_im[n - 1::-1][:n, :, :, None] * bb_im[None]
        wi = pw_re[n - 1::-1][:n, :, :, None] * bb_im[None] + pw_im[n - 1::-1][:n, :, :, None] * bb_re[None]
        w = jnp.concatenate([wr, wi], axis=2)
        return jnp.transpose(w, (1, 0, 3, 2)).reshape(g, n * c, 2 * p)
    wst = in_to_state(chunk)
    wmeta = in_to_state(N_META)

    wout = jnp.concatenate([e_re[1:chunk + 1], -e_im[1:chunk + 1]], axis=3)
    wout = jnp.transpose(wout, (1, 3, 0, 2)).reshape(g, 2 * p, chunk * c)

    ar, ai = pw_re[chunk], pw_im[chunk]
    rows_r, rows_i = [], []
    for _ in range(n_steps):
        rows_r.append(jnp.concatenate([ar, ar], axis=-1))
        rows_i.append(jnp.concatenate([-ai, ai], axis=-1))
        ar, ai = ar * ar - ai * ai, 2.0 * ar * ai
    apow = jnp.stack(rows_r + rows_i, axis=1)
    d_tiled = jnp.tile(d_skip, (1, chunk))[:, None, :]
    return kt, wst.astype(BF16), wout.astype(BF16), wmeta.astype(BF16), apow, d_tiled


def _postmix_kernel(x_ref, ya_ref, yb_ref, ga_ref, gb_ref, lng_ref, lnb_ref,
                    wglu_ref, bglu_ref, wa_ref, wb_ref, wo_ref, l1g_ref, l1b_ref,
                    wrh_ref, wrl_ref, br_ref, h1_ref, ridx_ref, rw_ref, cnt_ref, carry_ref,
                    *, alpha, n_experts):
    i = pl.program_id(0)
    tm = x_ref.shape[0]
    halves = [slice(0, tm // 2), slice(tm // 2, tm)]
    dot = functools.partial(jnp.dot, preferred_element_type=F32)
    glu = [yb_ref[r, :].astype(F32) * _sigmoid(dot(yb_ref[r, :], wglu_ref[...]) + bglu_ref[...]) for r in halves]
    att = [ga_ref[r, :].astype(F32) * dot(ya_ref[r, :], wa_ref[...]) for r in halves]
    merged = [att[j] + gb_ref[r, :].astype(F32) * dot(glu[j].astype(BF16), wb_ref[...])
              for j, r in enumerate(halves)]
    mix = [dot(m.astype(BF16), wo_ref[...]) for m in merged]
    h1s = [_layer_norm(alpha * _layer_norm(x_ref[r, :], lng_ref[...], lnb_ref[...]) + mix[j],
                       l1g_ref[...], l1b_ref[...]) for j, r in enumerate(halves)]
    parts = []
    for j, r in enumerate(halves):
        h1_ref[r] = h1s[j].reshape((tm // 2,) + h1_ref.shape[1:])
        hb = h1s[j].astype(BF16)
        hl = (h1s[j] - hb.astype(F32)).astype(BF16)
        parts.append(dot(hb, wrh_ref[...]) + dot(hb, wrl_ref[...]) + dot(hl, wrh_ref[...]))
    logits = jnp.concatenate(parts, axis=0) + br_ref[...]
    lane = lax.broadcasted_iota(jnp.int32, logits.shape, 1)
    logits = jnp.where(lane < n_experts, logits, -jnp.inf)
    denom = jnp.zeros((tm, 1), F32)
    top = None
    hits, idxs, ws = [], [], []
    for _ in range(TOP_K):
        mx = jnp.max(logits, axis=-1, keepdims=True)
        idx = jnp.min(jnp.where(logits == mx, lane, LANES), axis=-1, keepdims=True)
        hit = lane == idx
        if top is None:
            top = mx
        w = jnp.exp(mx - top)
        denom = denom + w
        hits.append(hit)
        idxs.append(idx)
        ws.append(w)
        logits = jnp.where(hit, -jnp.inf, logits)

    @pl.when(i == 0)
    def _():
        carry_ref[...] = jnp.zeros_like(carry_ref)

    sel = jnp.zeros(logits.shape, F32)
    for hit in hits:
        sel = sel + jnp.where(hit, 1.0, 0.0)
    row = lax.broadcasted_iota(jnp.int32, (tm, tm), 0)
    col = lax.broadcasted_iota(jnp.int32, (tm, tm), 1)
    before = (col < row).astype(BF16)
    seen = jnp.dot(before, sel.astype(BF16), preferred_element_type=F32) + carry_ref[...]
    ridx = jnp.zeros(logits.shape, jnp.int32)
    rw = jnp.zeros(logits.shape, F32)
    for kk in range(TOP_K):
        rank = jnp.sum(jnp.where(hits[kk], seen, 0.0), axis=-1, keepdims=True).astype(jnp.int32)
        ridx = jnp.where(lane == kk, idxs[kk], ridx)
        ridx = jnp.where(lane == TOP_K + kk, rank, ridx)
        rw = jnp.where(lane == kk, ws[kk] / denom, rw)
    ridx_ref[...] = ridx
    rw_ref[...] = rw
    total = carry_ref[...] + jnp.sum(sel, axis=0, keepdims=True)
    carry_ref[...] = total
    cnt_ref[...] = jnp.broadcast_to(total, cnt_ref.shape)


def _postmix(x2, ya, yb, ga, gb, ln_g, ln_b, w_glu, b_glu, w_a, w_b, w_o, l1g, l1b,
             wr_hi, wr_lo, b_r, *, tm, alpha, n_experts):
    t, dm = x2.shape
    sw = ya.shape[1]
    rows = lambda i: (i, 0)
    const = lambda i: (0, 0)
    full = lambda a: pl.BlockSpec(a.shape, const)
    return pl.pallas_call(
        functools.partial(_postmix_kernel, alpha=alpha, n_experts=n_experts),
        grid=(t // tm,),
        in_specs=[pl.BlockSpec((tm, dm), rows), pl.BlockSpec((tm, sw), rows), pl.BlockSpec((tm, sw), rows),
                  pl.BlockSpec((tm, dm), rows), pl.BlockSpec((tm, dm), rows),
                  full(ln_g), full(ln_b), full(w_glu), full(b_glu), full(w_a), full(w_b), full(w_o),
                  full(l1g), full(l1b), full(wr_hi), full(wr_lo), full(b_r)],
        out_specs=[pl.BlockSpec((tm, 1, dm), lambda i: (i, 0, 0)), pl.BlockSpec((tm, LANES), rows),
                   pl.BlockSpec((tm, LANES), rows), pl.BlockSpec((8, LANES), const)],
        out_shape=[jax.ShapeDtypeStruct((t, 1, dm), F32), jax.ShapeDtypeStruct((t, LANES), jnp.int32),
                   jax.ShapeDtypeStruct((t, LANES), F32), jax.ShapeDtypeStruct((8, LANES), F32)],
        scratch_shapes=[pltpu.VMEM((1, LANES), F32)],
        compiler_params=pltpu.CompilerParams(
            dimension_semantics=("arbitrary",), vmem_limit_bytes=48 * 1024 * 1024),
        name="postmix_router",
    )(x2, ya, yb, ga, gb, ln_g, ln_b, w_glu, b_glu, w_a, w_b, w_o, l1g, l1b, wr_hi, wr_lo, b_r)


MOE_ROW_TILE = 512
MOE_COL_CHUNK = 256


def _moe_kernel(te_ref, nv_ref, inv_ref, h_hbm, wg_ref, bg_ref, wu_ref, bu_ref, wd_ref, bd_ref, o_hbm,
                xbuf_ref, ybuf_ref, x2_ref, xb_ref, act_ref, wgb_ref, wub_ref, wdb_ref, sem_g, sem_s,
                *, n_tokens):
    i = pl.program_id(0)
    n = pl.num_programs(0)
    _, tm, _, dm = xbuf_ref.shape
    cur = i % 2
    oth = (i + 1) % 2

    def start_gather(tile, slot):
        base = tile * tm

        def body(r, c):
            s = inv_ref[base + r]
            tok = s & (n_tokens - 1) if n_tokens & (n_tokens - 1) == 0 else s % n_tokens
            pltpu.make_async_copy(h_hbm.at[tok], xbuf_ref.at[slot, r], sem_g.at[slot]).start()
            return c
        lax.fori_loop(0, tm, body, 0, unroll=8)

    def wait_gather(slot):
        pltpu.make_async_copy(h_hbm.at[pl.ds(0, tm)], xbuf_ref.at[slot], sem_g.at[slot]).wait()

    def start_scatter(tile, slot):
        base = tile * tm

        def body(r, c):
            pltpu.make_async_copy(ybuf_ref.at[slot, r], o_hbm.at[inv_ref[base + r]], sem_s.at[slot]).start()
            return c
        lax.fori_loop(0, tm, body, 0, unroll=8)

    def wait_scatter(slot):
        pltpu.make_async_copy(ybuf_ref.at[slot], o_hbm.at[pl.ds(0, tm)], sem_s.at[slot]).wait()

    @pl.when(i == 0)
    def _():
        ybuf_ref[...] = jnp.zeros_like(ybuf_ref)
        start_gather(0, 0)

    valid = i < nv_ref[0]
    new_expert = jnp.logical_or(i == 0, te_ref[i] != te_ref[jnp.maximum(i - 1, 0)])

    @pl.when(jnp.logical_and(valid, new_expert))
    def _():
        wgb_ref[...] = wg_ref[0].astype(BF16)
        wub_ref[...] = wu_ref[0].astype(BF16)
        wdb_ref[...] = wd_ref[0].astype(BF16)

    def row_copies():
        wait_gather(cur)
        start_gather(jnp.minimum(i + 1, n - 1), oth)
        start_scatter(jnp.maximum(i - 1, 0), oth)

    def start_gather_group(g, n_groups):
        base = jnp.minimum(i + 1, n - 1) * tm
        per = tm // n_groups
        for r in range(g * per, (g + 1) * per):
            s = inv_ref[base + r]
            tok = s & (n_tokens - 1) if n_tokens & (n_tokens - 1) == 0 else s % n_tokens
            pltpu.make_async_copy(h_hbm.at[tok], xbuf_ref.at[oth, r], sem_g.at[oth]).start()

    def start_scatter_group(g, n_groups):
        base = jnp.maximum(i - 1, 0) * tm
        per = tm // n_groups
        for r in range(g * per, (g + 1) * per):
            pltpu.make_async_copy(ybuf_ref.at[oth, r], o_hbm.at[inv_ref[base + r]], sem_s.at[oth]).start()

    @pl.when(valid)
    def _():
        de = wgb_ref.shape[1]
        nc_up, nc_down = de // MOE_COL_CHUNK, dm // MOE_COL_CHUNK
        wait_gather(cur)
        x2_ref[...] = xbuf_ref[cur].reshape(x2_ref.shape)
        xb_ref[...] = x2_ref[...].astype(BF16)
        for c in range(nc_up):
            start_gather_group(c, nc_up)
            cols = slice(c * MOE_COL_CHUNK, (c + 1) * MOE_COL_CHUNK)
            gate = jnp.minimum(jnp.dot(xb_ref[...], wgb_ref[:, cols], preferred_element_type=F32)
                               + bg_ref[0][:, cols], SWIGLU_LIMIT)
            up = jnp.clip(jnp.dot(xb_ref[...], wub_ref[:, cols], preferred_element_type=F32)
                          + bu_ref[0][:, cols], -SWIGLU_LIMIT, SWIGLU_LIMIT)
            act_ref[:, cols] = ((up + 1.0) * gate * _sigmoid(SWIGLU_ALPHA * gate)).astype(BF16)
        for c in range(nc_down):
            start_scatter_group(c, nc_down)
            cols = slice(c * MOE_COL_CHUNK, (c + 1) * MOE_COL_CHUNK)
            x2_ref[:, cols] = jnp.dot(act_ref[...], wdb_ref[:, cols], preferred_element_type=F32) \
                + bd_ref[0][:, cols]

        @pl.when(i == 0)
        def _():
            wait_scatter(oth)

        @pl.when(i >= 2)
        def _():
            wait_scatter(cur)
        ybuf_ref[cur] = x2_ref[...].reshape(tm, 1, dm)

    @pl.when(jnp.logical_not(valid))
    def _():
        @pl.when(i >= 2)
        def _():
            wait_scatter(cur)
        row_copies()

    @pl.when(i == n - 1)
    def _():
        wait_gather(oth)
        wait_scatter(oth)
        start_scatter(i, cur)
        wait_scatter(cur)


def _moe(tile_expert, n_valid, inv, h1_rows, w_gate, b_gate, w_up, b_up, w_down, b_down, *, tm):
    t, _, dm = h1_rows.shape
    _, _, de = w_gate.shape
    n_rows = inv.shape[0]
    per_e = lambda i, te, nv, iv: (te[i], 0, 0)
    return pl.pallas_call(
        functools.partial(_moe_kernel, n_tokens=t),
        grid_spec=pltpu.PrefetchScalarGridSpec(
            num_scalar_prefetch=3, grid=(n_rows // tm,),
            in_specs=[pl.BlockSpec(memory_space=pl.ANY),
                      pl.BlockSpec((1, dm, de), per_e), pl.BlockSpec((1, 1, de), per_e),
                      pl.BlockSpec((1, dm, de), per_e), pl.BlockSpec((1, 1, de), per_e),
                      pl.BlockSpec((1, de, dm), per_e), pl.BlockSpec((1, 1, dm), per_e)],
            out_specs=pl.BlockSpec(memory_space=pl.ANY),
            scratch_shapes=[pltpu.VMEM((2, tm, 1, dm), F32), pltpu.VMEM((2, tm, 1, dm), F32),
                            pltpu.VMEM((tm, dm), F32), pltpu.VMEM((tm, dm), BF16), pltpu.VMEM((tm, de), BF16),
                            pltpu.VMEM((dm, de), BF16),
                            pltpu.VMEM((dm, de), BF16), pltpu.VMEM((de, dm), BF16),
                            pltpu.SemaphoreType.DMA((2,)), pltpu.SemaphoreType.DMA((2,))]),
        out_shape=jax.ShapeDtypeStruct((n_rows, 1, dm), F32),
        compiler_params=pltpu.CompilerParams(
            dimension_semantics=("arbitrary",), vmem_limit_bytes=56 * 1024 * 1024),
        name="moe_experts",
    )(tile_expert, n_valid, inv, h1_rows, w_gate, b_gate, w_up, b_up, w_down, b_down)


def _combine_kernel(*refs, alpha):
    y_refs = refs[:TOP_K]
    h1_ref, rw_ref, l2g_ref, l2b_ref, o_ref, rows2_ref = refs[TOP_K:]
    tm, dm = o_ref.shape
    rw = rw_ref[...]
    ffn = jnp.zeros((tm, dm), F32)
    for kk in range(TOP_K):
        rows2_ref[...] = y_refs[kk][...].reshape(tm, dm)
        ffn = ffn + rw[:, kk:kk + 1] * rows2_ref[...]
    rows2_ref[...] = h1_ref[...].reshape(tm, dm)
    o_ref[...] = _layer_norm(alpha * rows2_ref[...] + ffn, l2g_ref[...], l2b_ref[...])


def _combine(y_slots, h1_rows, rw, l2g, l2b, *, tm, alpha):
    t, _, dm = h1_rows.shape
    rows = lambda i: (i, 0)
    const = lambda i: (0, 0)
    slot_specs = [pl.BlockSpec((tm, 1, dm), functools.partial(lambda i, kk: (kk * (t // tm) + i, 0, 0), kk=kk))
                  for kk in range(TOP_K)]
    return pl.pallas_call(
        functools.partial(_combine_kernel, alpha=alpha),
        grid=(t // tm,),
        in_specs=slot_specs + [pl.BlockSpec((tm, 1, dm), lambda i: (i, 0, 0)),
                               pl.BlockSpec((tm, LANES), rows),
                               pl.BlockSpec((1, dm), const), pl.BlockSpec((1, dm), const)],
        out_specs=pl.BlockSpec((tm, dm), rows),
        out_shape=jax.ShapeDtypeStruct((t, dm), F32),
        scratch_shapes=[pltpu.VMEM((tm, dm), F32)],
        compiler_params=pltpu.CompilerParams(
            dimension_semantics=("arbitrary",), vmem_limit_bytes=48 * 1024 * 1024),
        name="moe_combine_ln2",
    )(*([y_slots] * TOP_K), h1_rows, rw, l2g, l2b)


INVERT_CHUNK = 8192


def _invert_kernel(rows_ref, slots_ref, inv_ref):
    def body(p, c):
        inv_ref[rows_ref[p]] = slots_ref[p]
        return c
    lax.fori_loop(0, rows_ref.shape[0], body, 0, unroll=8)


def _invert(rows, slots):
    n = rows.shape[0]
    ch = math.gcd(n, INVERT_CHUNK)
    chunk = pl.BlockSpec((ch,), lambda i: (i,), memory_space=pltpu.SMEM)
    return pl.pallas_call(
        _invert_kernel,
        grid=(n // ch,),
        in_specs=[chunk, chunk],
        out_specs=pl.BlockSpec(memory_space=pltpu.SMEM),
        out_shape=jax.ShapeDtypeStruct((n,), jnp.int32),
        compiler_params=pltpu.CompilerParams(dimension_semantics=("arbitrary",)),
        name="moe_row_slots",
    )(rows, slots)


def _routing_tables(ridx, counts, *, n_experts, tm):
    t = ridx.shape[0]
    n_real = TOP_K * t
    n_fill = n_experts * tm
    e_idx = ridx[:, :TOP_K]
    rank = ridx[:, TOP_K:2 * TOP_K]
    cnt = counts[0, :n_experts].astype(jnp.int32)
    padded = (cnt + tm - 1) // tm * tm
    ends = jnp.cumsum(padded)
    offs = ends - padded
    experts = jnp.arange(n_experts, dtype=jnp.int32)
    base = jnp.sum(jnp.where(e_idx[..., None] == experts, offs, 0), axis=-1)
    token_rows = (base + rank).reshape(-1)
    token_slots = (jnp.arange(TOP_K, dtype=jnp.int32)[None, :] * t
                   + jnp.arange(t, dtype=jnp.int32)[:, None]).reshape(-1)
    fill_cnt = padded - cnt
    fill_end = jnp.cumsum(fill_cnt)
    shift = jnp.concatenate([offs + cnt - (fill_end - fill_cnt), jnp.full((1,), n_real, jnp.int32)])
    j = jnp.arange(n_fill, dtype=jnp.int32)
    fill_rows = j + shift[0] + jnp.sum(
        jnp.where(j[:, None] >= fill_end[None, :], (shift[1:] - shift[:-1])[None, :], 0), axis=1)
    rows = jnp.concatenate([token_rows, fill_rows])
    slots = jnp.concatenate([token_slots, n_real + j])
    inv = _invert(rows, slots)
    n_tiles = (n_real + n_fill) // tm
    tile_start = jnp.arange(n_tiles, dtype=jnp.int32) * tm
    tile_expert = jnp.minimum(jnp.sum(ends[None, :] <= tile_start[:, None], axis=1), n_experts - 1)
    n_valid = (ends[-1] // tm).reshape(1)
    return inv.astype(jnp.int32), tile_expert.astype(jnp.int32), n_valid.astype(jnp.int32)


def _pick_tile(n, want):
    t = min(n, want)
    assert n % t == 0, (n, t)
    return t


def kernel(x, meta, ln_in_g, ln_in_b, w_in, b_f, w_up_a, w_up_b, w_o, ssm_a_re, ssm_a_im, ssm_log_dt,
           ssm_b_re, ssm_b_im, ssm_c_re, ssm_c_im, ssm_d, w_glu, b_glu, ln1_g, ln1_b, w_router, b_router,
           w_gate, b_gate, w_up, b_up, w_down, b_down, ln2_g, ln2_b):
    bsz, seq, dm = x.shape
    depth = w_in.shape[0]
    assert depth == 1 and meta.shape[0] == N_META
    alpha = (2.0 * depth) ** 0.25
    n_groups = ssm_a_re.shape[1]
    n_experts = w_router.shape[-1]
    aw = ATT_WIDTH
    t = bsz * seq
    x2 = x.reshape(t, dm)
    row = lambda a: a.reshape(1, -1)

    f_off = 3 * aw
    u_off = f_off + ATT_HEADS
    w0 = w_in[0]
    w_main = jnp.concatenate([w0[:, :f_off], w0[:, u_off:]], axis=1).astype(BF16)
    w_f = jnp.pad(w0[:, f_off:u_off], ((0, 0), (0, LANES - ATT_HEADS))).astype(BF16)
    b_f_pad = jnp.pad(b_f[0], (0, LANES - ATT_HEADS)).reshape(1, LANES)
    ln_g, ln_b = row(ln_in_g), row(ln_in_b)

    tm = _pick_tile(seq, 512)
    q_t, k, v_t, u, ga, gb, _ = _inproj(x2, ln_g, ln_b, w_main, w_f, b_f_pad, tm=tm, tiles_per_seq=seq // tm,
                                        transpose_qv=True)
    _, k_m, v_m, u_m, _, _, fcum_m = _inproj(meta, ln_g, ln_b, w_main, w_f, b_f_pad, tm=N_META, tiles_per_seq=1,
                                             transpose_qv=False)

    pad_m = META_KEYS_PAD - N_META
    fm = fcum_m[:, :ATT_HEADS]
    bias_m = jnp.pad(-LOG2E * (fm - fm[N_META - 1:N_META, :]), ((0, pad_m), (0, 0)), constant_values=-BIG)
    pieces_m = jnp.stack(_split3(bias_m), axis=-1)
    k_meta = jnp.pad(k_m, ((0, pad_m), (0, 0))).reshape(META_KEYS_PAD, ATT_HEADS, LANES)
    k_meta = k_meta.at[:, :, HEAD_DIM:HEAD_DIM + 3].set(pieces_m).reshape(META_KEYS_PAD, ATT_HEADS * LANES)
    v_meta_t = jnp.pad(v_m, ((0, pad_m), (0, 0))).T
    tq = _pick_tile(seq, 512)
    y_a = _attn(q_t, k, v_t, k_meta, v_meta_t, bsz=bsz, seq=seq, tq=tq)

    chunk = SSM_CHUNK
    n_chunks = seq // chunk
    n_steps = max(1, (n_chunks - 1).bit_length())
    kt, wst, wout, wmeta, apow, d_tiled = _ssm_tables(
        ssm_a_re[0], ssm_a_im[0], ssm_log_dt[0], ssm_b_re[0], ssm_b_im[0], ssm_c_re[0], ssm_c_im[0], ssm_d[0],
        chunk=chunk, n_steps=n_steps)
    u_g = jnp.transpose(u.reshape(bsz * n_chunks, chunk, n_groups, SSM_GROUP), (2, 0, 1, 3))
    u_g = u_g.reshape(n_groups, bsz * n_chunks, chunk * SSM_GROUP)
    um_g = jnp.transpose(u_m.reshape(N_META, n_groups, SSM_GROUP), (1, 0, 2)).reshape(n_groups, 1, N_META * SSM_GROUP)
    um_g = jnp.pad(um_g, ((0, 0), (0, 15), (0, 0)))
    y_g = _ssm(u_g, kt, wst, wout, um_g, wmeta, apow, d_tiled, chunks_per_seq=n_chunks)
    y_b = jnp.transpose(y_g.reshape(n_groups, bsz * n_chunks, chunk, SSM_GROUP), (1, 2, 0, 3)).reshape(t, -1)

    wr = jnp.pad(w_router[0], ((0, 0), (0, LANES - n_experts)))
    wr_hi = wr.astype(BF16)
    wr_lo = (wr - wr_hi.astype(F32)).astype(BF16)
    b_r = jnp.pad(b_router[0], (0, LANES - n_experts)).reshape(1, LANES)
    tm2 = _pick_tile(t, 512)
    h1_rows, ridx, rw, counts = _postmix(
        x2, y_a, y_b, ga, gb, ln_g, ln_b, w_glu[0].astype(BF16), row(b_glu[0]),
        w_up_a[0].astype(BF16), w_up_b[0].astype(BF16), w_o[0].astype(BF16), row(ln1_g[0]), row(ln1_b[0]),
        wr_hi, wr_lo, b_r, tm=tm2, alpha=alpha, n_experts=n_experts)

    tm3 = MOE_ROW_TILE
    inv, tile_expert, n_valid = _routing_tables(ridx, counts, n_experts=n_experts, tm=tm3)
    y_slots = _moe(tile_expert, n_valid, inv, h1_rows, w_gate[0], b_gate[0][:, None, :], w_up[0],
                   b_up[0][:, None, :], w_down[0], b_down[0][:, None, :], tm=tm3)
    tm4 = _pick_tile(t, 256)
    out = _combine(y_slots, h1_rows, rw, row(ln2_g[0]), row(ln2_b[0]), tm=tm4, alpha=alpha)
    return out.reshape(bsz, seq, dm)
```

```python
import functools
import math

import jax
import jax.numpy as jnp
from jax import lax
from jax.experimental import pallas as pl
from jax.experimental.pallas import tpu as pltpu

F32 = jnp.float32
BF16 = jnp.bfloat16

N_META = 16
ATT_HEADS = 8
HEAD_DIM = 64
ATT_WIDTH = ATT_HEADS * HEAD_DIM
SSM_GROUP = 16
SSM_STATE = 64
TOP_K = 4
SWIGLU_LIMIT = 7.0
SWIGLU_ALPHA = 1.702
LN_EPS = 1e-5

LANES = 128
SSM_CHUNK = 64
META_KEYS_PAD = LANES
BIG = 1e30
LOG2E = 1.4426950408889634


def _layer_norm(x, g, b):
    mu = jnp.mean(x, axis=-1, keepdims=True)
    xc = x - mu
    var = jnp.mean(xc * xc, axis=-1, keepdims=True)
    return xc * lax.rsqrt(var + LN_EPS) * g + b


def _log_sigmoid(z):
    return jnp.minimum(z, 0.0) - jnp.log1p(jnp.exp(-jnp.abs(z)))


def _sigmoid(z):
    return 1.0 / (1.0 + jnp.exp(-z))


def _split3(x):
    hi = x.astype(BF16)
    r1 = x - hi.astype(F32)
    mid = r1.astype(BF16)
    lo = (r1 - mid.astype(F32)).astype(BF16)
    return hi, mid, lo


def _inproj_kernel(x_ref, g_ref, b_ref, w_ref, wf_ref, bf_ref,
                   q_ref, k_ref, v_ref, u_ref, ga_ref, gb_ref, fc_ref, carry_ref,
                   *, tiles_per_seq, transpose_qv):
    i = pl.program_id(0)
    tm = x_ref.shape[0]
    h = _layer_norm(x_ref[...], g_ref[...], b_ref[...]).astype(BF16)

    def proj(lo, hi):
        return jnp.dot(h, w_ref[:, lo:hi], preferred_element_type=F32)

    aw = ATT_WIDTH
    u_ref[...] = proj(3 * aw, 4 * aw).astype(BF16)
    dm = ga_ref.shape[1]
    ga_ref[...] = _sigmoid(proj(4 * aw, 4 * aw + dm)).astype(BF16)
    gb_ref[...] = _sigmoid(proj(4 * aw + dm, 4 * aw + 2 * dm)).astype(BF16)

    zf = jnp.dot(h, wf_ref[...], preferred_element_type=F32) + bf_ref[...]
    lf = _log_sigmoid(zf)

    @pl.when(i % tiles_per_seq == 0)
    def _():
        carry_ref[...] = jnp.zeros_like(carry_ref)

    row = lax.broadcasted_iota(jnp.int32, (tm, tm), 0)
    col = lax.broadcasted_iota(jnp.int32, (tm, tm), 1)
    tri = (col <= row).astype(BF16)
    hi, mid, lo = _split3(lf)
    cs = (jnp.dot(tri, hi, preferred_element_type=F32)
          + jnp.dot(tri, mid, preferred_element_type=F32)
          + jnp.dot(tri, lo, preferred_element_type=F32)) + carry_ref[...]
    fc_ref[...] = cs
    carry_ref[...] = cs[tm - 1:tm, :]

    lane = lax.broadcasted_iota(jnp.int32, (1, LANES), 1)
    own = lane < HEAD_DIM
    bias_lanes = jnp.logical_and(lane >= HEAD_DIM, lane < HEAD_DIM + 3)
    pieces = [p.astype(F32) for p in _split3(-LOG2E * cs)]

    def slabs(x, extra, transposed=False):
        out = []
        for hp in range(ATT_HEADS // 2):
            x2 = x[:, LANES * hp:LANES * (hp + 1)]
            out.append(jnp.where(own, x2, extra(2 * hp)))
            out.append(jnp.where(own, pltpu.roll(x2, HEAD_DIM, axis=1), extra(2 * hp + 1)))
        if transposed:
            out = [o.T for o in out]
            return jnp.concatenate(out, axis=0).astype(BF16)
        return jnp.concatenate(out, axis=1).astype(BF16)

    def bias_extra(h):
        e = jnp.zeros((tm, LANES), F32)
        for j, piece in enumerate(pieces):
            e = jnp.where(lane == HEAD_DIM + j, pltpu.roll(piece, (HEAD_DIM + j - h) % LANES, axis=1), e)
        return e

    q_ref[...] = slabs(proj(0, aw) * (LOG2E / math.sqrt(HEAD_DIM)),
                       lambda h: jnp.broadcast_to(jnp.where(bias_lanes, 1.0, 0.0), (tm, LANES)), transpose_qv)
    k_ref[...] = slabs(proj(aw, 2 * aw), bias_extra)
    v_ref[...] = slabs(proj(2 * aw, 3 * aw), lambda h: jnp.ones((tm, LANES), F32), transpose_qv)


def _inproj(x2, ln_g, ln_b, w_main, w_f, b_f, *, tm, tiles_per_seq, transpose_qv):
    t, dm = x2.shape
    aw = ATT_WIDTH
    n_main = w_main.shape[1]
    const = lambda i: (0, 0)
    rows = lambda i: (i, 0)
    slab = ATT_HEADS * LANES
    row_slab = (jax.ShapeDtypeStruct((t, slab), BF16), pl.BlockSpec((tm, slab), rows))
    col_slab = (jax.ShapeDtypeStruct((slab, t), BF16), pl.BlockSpec((slab, tm), lambda i: (0, i)))
    qv = col_slab if transpose_qv else row_slab
    out_shape = [qv[0], row_slab[0], qv[0], jax.ShapeDtypeStruct((t, aw), BF16)] + \
                [jax.ShapeDtypeStruct((t, dm), BF16)] * 2 + [jax.ShapeDtypeStruct((t, LANES), F32)]
    out_specs = [qv[1], row_slab[1], qv[1], pl.BlockSpec((tm, aw), rows)] + \
                [pl.BlockSpec((tm, dm), rows)] * 2 + [pl.BlockSpec((tm, LANES), rows)]
    return pl.pallas_call(
        functools.partial(_inproj_kernel, tiles_per_seq=tiles_per_seq, transpose_qv=transpose_qv),
        grid=(t // tm,),
        in_specs=[pl.BlockSpec((tm, dm), rows),
                  pl.BlockSpec((1, dm), const), pl.BlockSpec((1, dm), const),
                  pl.BlockSpec((dm, n_main), const),
                  pl.BlockSpec((dm, LANES), const), pl.BlockSpec((1, LANES), const)],
        out_specs=out_specs,
        out_shape=out_shape,
        scratch_shapes=[pltpu.VMEM((1, LANES), F32)],
        compiler_params=pltpu.CompilerParams(
            dimension_semantics=("arbitrary",), vmem_limit_bytes=56 * 1024 * 1024),
        name="inproj",
    )(x2, ln_g, ln_b, w_main, w_f, b_f)


def _attn_kernel(qi_tab, ki_tab, qt_ref, k_ref, vt_ref, km_ref, vmt_ref, o_ref, m_ref, acc_ref, st_ref):
    p = pl.program_id(1)
    qi = qi_tab[p]
    ki = ki_tab[p]
    tq = qt_ref.shape[1]

    def process(k_blk, vt_blk, mask):
        n = k_blk.shape[0]

        def logits(h):
            sl = slice(LANES * h, LANES * (h + 1))
            st = jnp.dot(k_blk[:, sl], qt_ref[sl, :], preferred_element_type=F32)
            if mask is not None:
                st = jnp.where(mask, st, -jnp.inf)
            st_ref[h % 2, 0:n] = st
            m_old = m_ref[h]
            m_new = jnp.maximum(m_old, jnp.max(st_ref[h % 2, 0:n], axis=0, keepdims=True))
            m_ref[h] = m_new
            return m_old, m_new

        def weigh(h, m_old, m_new):
            sl = slice(LANES * h, LANES * (h + 1))
            pt = jnp.exp2(st_ref[h % 2, 0:n] - m_new[0:1, :])
            alpha = jnp.exp2(m_old - m_new)
            acc_ref[h] = alpha[0:1, :] * acc_ref[h] + jnp.dot(
                vt_blk[sl, :], pt.astype(BF16), preferred_element_type=F32)

        ms = logits(0)
        for h in range(ATT_HEADS):
            nxt = logits(h + 1) if h + 1 < ATT_HEADS else None
            weigh(h, *ms)
            ms = nxt

    @pl.when(ki == 0)
    def _():
        m_ref[...] = jnp.full_like(m_ref, -jnp.inf)
        acc_ref[...] = jnp.zeros_like(acc_ref)
        process(km_ref[...], vmt_ref[...], None)

    @pl.when(ki < qi)
    def _():
        process(k_ref[...], vt_ref[...], None)

    @pl.when(ki == qi)
    def _():
        tk = k_ref.shape[0]
        key = lax.broadcasted_iota(jnp.int32, (tk, tq), 0)
        qry = lax.broadcasted_iota(jnp.int32, (tk, tq), 1)
        process(k_ref[...], vt_ref[...], key <= qry)
        for hp in range(ATT_HEADS // 2):
            a0, a1 = acc_ref[2 * hp], acc_ref[2 * hp + 1]
            pair = jnp.concatenate([a0[:HEAD_DIM] / a0[HEAD_DIM:], a1[:HEAD_DIM] / a1[HEAD_DIM:]], axis=0)
            o_ref[:, LANES * hp:LANES * (hp + 1)] = pair.T.astype(o_ref.dtype)


def _attn(q_t, k, v_t, k_meta, v_meta_t, *, bsz, seq, tq):
    nq = seq // tq
    pairs = [(a, b) for a in range(nq) for b in range(a + 1)]
    qi_tab = jnp.asarray([a for a, _ in pairs], jnp.int32)
    ki_tab = jnp.asarray([b for _, b in pairs], jnp.int32)
    w = k.shape[1]
    qcol = lambda b, p, qt, kt: (0, b * nq + qt[p])
    kcol = lambda b, p, qt, kt: (0, b * nq + kt[p])
    krow = lambda b, p, qt, kt: (b * nq + kt[p], 0)
    const = lambda b, p, qt, kt: (0, 0)
    grid_spec = pltpu.PrefetchScalarGridSpec(
        num_scalar_prefetch=2,
        grid=(bsz, len(pairs)),
        in_specs=[pl.BlockSpec((w, tq), qcol),
                  pl.BlockSpec((tq, w), krow),
                  pl.BlockSpec((w, tq), kcol),
                  pl.BlockSpec((META_KEYS_PAD, w), const),
                  pl.BlockSpec((w, META_KEYS_PAD), const)],
        out_specs=pl.BlockSpec((tq, ATT_WIDTH), lambda b, p, qt, kt: (b * nq + qt[p], 0)),
        scratch_shapes=[pltpu.VMEM((ATT_HEADS, 8, tq), F32),
                        pltpu.VMEM((ATT_HEADS, LANES, tq), F32),
                        pltpu.VMEM((2, tq, tq), F32)],
    )
    return pl.pallas_call(
        _attn_kernel,
        grid_spec=grid_spec,
        out_shape=jax.ShapeDtypeStruct((bsz * seq, ATT_WIDTH), BF16),
        compiler_params=pltpu.CompilerParams(
            dimension_semantics=("arbitrary", "arbitrary"), vmem_limit_bytes=48 * 1024 * 1024),
        name="fox_attention",
    )(qi_tab, ki_tab, q_t, k, v_t, k_meta, v_meta_t)


def _ssm_kernel(u_ref, kt_ref, wst_ref, wout_ref, um_ref, wm_ref, apow_ref, d_ref, y_ref, toep_ref,
                *, chunks_per_seq, n_steps):
    u = u_ref[0]
    rows = u.shape[0]
    p2 = 2 * SSM_STATE
    kt = kt_ref[0]
    n_c, tc = kt.shape
    per_vreg = LANES // n_c
    ext = jnp.concatenate([jnp.zeros_like(kt), kt], axis=1)
    for sub in range(per_vreg):
        rot = ext if sub == 0 else pltpu.roll(ext, n_c * sub, axis=1)
        for whole in range(tc // LANES):
            tp = per_vreg * whole + sub
            toep_ref[tp * n_c:(tp + 1) * n_c, :] = rot[:, tc - LANES * whole:2 * tc - LANES * whole].astype(BF16)
    y = jnp.dot(u, toep_ref[...], preferred_element_type=F32)
    s = jnp.dot(u, wst_ref[0], preferred_element_type=F32)
    x0 = jnp.dot(um_ref[0], wm_ref[0], preferred_element_type=F32)[0:1, :]

    lane = lax.broadcasted_iota(jnp.int32, (1, p2), 1)
    re_half = lane < SSM_STATE
    j = lax.broadcasted_iota(jnp.int32, (rows, 1), 0) % chunks_per_seq

    def cmul(step, z):
        ar = apow_ref[0, step:step + 1, :]
        ai = apow_ref[0, n_steps + step:n_steps + step + 1, :]
        return ar * z + ai * pltpu.roll(z, SSM_STATE, axis=1)

    s = s + jnp.where(j == 0, cmul(0, jnp.broadcast_to(x0, s.shape)), 0.0)
    for step in range(n_steps):
        sh = 1 << step
        prev = pltpu.roll(s, sh, axis=0)
        s = s + jnp.where(j >= sh, cmul(step, prev), 0.0)
    x_in = jnp.where(j == 0, x0, pltpu.roll(s, 1, axis=0))
    del re_half
    y = y + jnp.dot(x_in.astype(BF16), wout_ref[0], preferred_element_type=F32)
    y = y + d_ref[0] * u.astype(F32)
    y_ref[0] = jax.nn.gelu(y).astype(y_ref.dtype)


def _ssm(u_g, kt, wst, wout, u_meta, wmeta, apow, d_tiled, *, chunks_per_seq):
    g, rows, tc = u_g.shape
    n_steps = apow.shape[1] // 2
    grp = lambda i: (i, 0, 0)
    return pl.pallas_call(
        functools.partial(_ssm_kernel, chunks_per_seq=chunks_per_seq, n_steps=n_steps),
        grid=(g,),
        in_specs=[pl.BlockSpec((1, rows, tc), grp),
                  pl.BlockSpec((1,) + kt.shape[1:], grp),
                  pl.BlockSpec((1, tc, 2 * SSM_STATE), grp),
                  pl.BlockSpec((1, 2 * SSM_STATE, tc), grp),
                  pl.BlockSpec((1,) + u_meta.shape[1:], grp),
                  pl.BlockSpec((1,) + wmeta.shape[1:], grp),
                  pl.BlockSpec((1,) + apow.shape[1:], grp),
                  pl.BlockSpec((1, 1, tc), grp)],
        out_specs=pl.BlockSpec((1, rows, tc), grp),
        out_shape=jax.ShapeDtypeStruct((g, rows, tc), BF16),
        scratch_shapes=[pltpu.VMEM((tc, tc), BF16)],
        compiler_params=pltpu.CompilerParams(
            dimension_semantics=("arbitrary",), vmem_limit_bytes=48 * 1024 * 1024),
        name="s5_ssm",
    )(u_g, kt, wst, wout, u_meta, wmeta, apow, d_tiled)


def _ssm_tables(a_re, a_im, log_dt, b_re, b_im, c_re, c_im, d_skip, *, chunk, n_steps):
    g, p = a_re.shape
    c = b_re.shape[-1]
    dt = jnp.exp(log_dt)[:, None]
    mag = jnp.exp(a_re * dt)
    ang = a_im * dt
    lb_re, lb_im = mag * jnp.cos(ang), mag * jnp.sin(ang)
    den = a_re * a_re + a_im * a_im
    z_re, z_im = lb_re - 1.0, lb_im
    coef_re = (z_re * a_re + z_im * a_im) / den
    coef_im = (z_im * a_re - z_re * a_im) / den
    bb_re = coef_re[..., None] * b_re - coef_im[..., None] * b_im
    bb_im = coef_re[..., None] * b_im + coef_im[..., None] * b_re

    def step(carry, _):
        cr, ci = carry
        return (cr * lb_re - ci * lb_im, cr * lb_im + ci * lb_re), (cr, ci)
    _, (pw_re, pw_im) = lax.scan(step, (jnp.ones_like(lb_re), jnp.zeros_like(lb_re)), None,
                                 length=chunk + 1)

    e_re = c_re[None] * pw_re[:, :, None, :] - c_im[None] * pw_im[:, :, None, :]
    e_im = c_re[None] * pw_im[:, :, None, :] + c_im[None] * pw_re[:, :, None, :]
    kern = (jnp.einsum('tgcp,gpd->tgcd', e_re[:chunk], bb_re)
            - jnp.einsum('tgcp,gpd->tgcd', e_im[:chunk], bb_im))
    kt = jnp.transpose(kern, (1, 3, 0, 2)).reshape(g, c, chunk * c)

    def in_to_state(n):
        wr = pw_re[n - 1::-1][:n, :, :, None] * bb_re[None] - pw_im[n - 1::-1][:n, :, :, None] * bb_im[None]
        wi = pw_re[n - 1::-1][:n, :, :, None] * bb_im[None] + pw_im[n - 1::-1][:n, :, :, None] * bb_re[None]
        w = jnp.concatenate([wr, wi], axis=2)
        return jnp.transpose(w, (1, 0, 3, 2)).reshape(g, n * c, 2 * p)
    wst = in_to_state(chunk)
    wmeta = in_to_state(N_META)

    wout = jnp.concatenate([e_re[1:chunk + 1], -e_im[1:chunk + 1]], axis=3)
    wout = jnp.transpose(wout, (1, 3, 0, 2)).reshape(g, 2 * p, chunk * c)

    ar, ai = pw_re[chunk], pw_im[chunk]
    rows_r, rows_i = [], []
    for _ in range(n_steps):
        rows_r.append(jnp.concatenate([ar, ar], axis=-1))
        rows_i.append(jnp.concatenate([-ai, ai], axis=-1))
        ar, ai = ar * ar - ai * ai, 2.0 * ar * ai
    apow = jnp.stack(rows_r + rows_i, axis=1)
    d_tiled = jnp.tile(d_skip, (1, chunk))[:, None, :]
    return kt, wst.astype(BF16), wout.astype(BF16), wmeta.astype(BF16), apow, d_tiled


def _postmix_kernel(x_ref, ya_ref, yb_ref, ga_ref, gb_ref, lng_ref, lnb_ref,
                    wglu_ref, bglu_ref, wa_ref, wb_ref, wo_ref, l1g_ref, l1b_ref,
                    wrh_ref, wrl_ref, br_ref, h1_ref, ridx_ref, rw_ref, cnt_ref, carry_ref,
                    *, alpha, n_experts):
    i = pl.program_id(0)
    tm = x_ref.shape[0]
    halves = [slice(0, tm // 2), slice(tm // 2, tm)]
    dot = functools.partial(jnp.dot, preferred_element_type=F32)
    glu = [yb_ref[r, :].astype(F32) * _sigmoid(dot(yb_ref[r, :], wglu_ref[...]) + bglu_ref[...]) for r in halves]
    att = [ga_ref[r, :].astype(F32) * dot(ya_ref[r, :], wa_ref[...]) for r in halves]
    merged = [att[j] + gb_ref[r, :].astype(F32) * dot(glu[j].astype(BF16), wb_ref[...])
              for j, r in enumerate(halves)]
    mix = [dot(m.astype(BF16), wo_ref[...]) for m in merged]
    h1s = [_layer_norm(alpha * _layer_norm(x_ref[r, :], lng_ref[...], lnb_ref[...]) + mix[j],
                       l1g_ref[...], l1b_ref[...]) for j, r in enumerate(halves)]
    parts = []
    for j, r in enumerate(halves):
        h1_ref[r] = h1s[j].reshape((tm // 2,) + h1_ref.shape[1:])
        hb = h1s[j].astype(BF16)
        hl = (h1s[j] - hb.astype(F32)).astype(BF16)
        parts.append(dot(hb, wrh_ref[...]) + dot(hb, wrl_ref[...]) + dot(hl, wrh_ref[...]))
    logits = jnp.concatenate(parts, axis=0) + br_ref[...]
    lane = lax.broadcasted_iota(jnp.int32, logits.shape, 1)
    logits = jnp.where(lane < n_experts, logits, -jnp.inf)
    denom = jnp.zeros((tm, 1), F32)
    top = None
    hits, idxs, ws = [], [], []
    for _ in range(TOP_K):
        mx = jnp.max(logits, axis=-1, keepdims=True)
        idx = jnp.min(jnp.where(logits == mx, lane, LANES), axis=-1, keepdims=True)
        hit = lane == idx
        if top is None:
            top = mx
        w = jnp.exp(mx - top)
        denom = denom + w
        hits.append(hit)
        idxs.append(idx)
        ws.append(w)
        logits = jnp.where(hit, -jnp.inf, logits)

    @pl.when(i == 0)
    def _():
        carry_ref[...] = jnp.zeros_like(carry_ref)

    sel = jnp.zeros(logits.shape, F32)
    for hit in hits:
        sel = sel + jnp.where(hit, 1.0, 0.0)
    row = lax.broadcasted_iota(jnp.int32, (tm, tm), 0)
    col = lax.broadcasted_iota(jnp.int32, (tm, tm), 1)
    before = (col < row).astype(BF16)
    seen = jnp.dot(before, sel.astype(BF16), preferred_element_type=F32) + carry_ref[...]
    ridx = jnp.zeros(logits.shape, jnp.int32)
    rw = jnp.zeros(logits.shape, F32)
    for kk in range(TOP_K):
        rank = jnp.sum(jnp.where(hits[kk], seen, 0.0), axis=-1, keepdims=True).astype(jnp.int32)
        ridx = jnp.where(lane == kk, idxs[kk], ridx)
        ridx = jnp.where(lane == TOP_K + kk, rank, ridx)
        rw = jnp.where(lane == kk, ws[kk] / denom, rw)
    ridx_ref[...] = ridx
    rw_ref[...] = rw
    total = carry_ref[...] + jnp.sum(sel, axis=0, keepdims=True)
    carry_ref[...] = total
    cnt_ref[...] = jnp.broadcast_to(total, cnt_ref.shape)


def _postmix(x2, ya, yb, ga, gb, ln_g, ln_b, w_glu, b_glu, w_a, w_b, w_o, l1g, l1b,
             wr_hi, wr_lo, b_r, *, tm, alpha, n_experts):
    t, dm = x2.shape
    sw = ya.shape[1]
    rows = lambda i: (i, 0)
    const = lambda i: (0, 0)
    full = lambda a: pl.BlockSpec(a.shape, const)
    return pl.pallas_call(
        functools.partial(_postmix_kernel, alpha=alpha, n_experts=n_experts),
        grid=(t // tm,),
        in_specs=[pl.BlockSpec((tm, dm), rows), pl.BlockSpec((tm, sw), rows), pl.BlockSpec((tm, sw), rows),
                  pl.BlockSpec((tm, dm), rows), pl.BlockSpec((tm, dm), rows),
                  full(ln_g), full(ln_b), full(w_glu), full(b_glu), full(w_a), full(w_b), full(w_o),
                  full(l1g), full(l1b), full(wr_hi), full(wr_lo), full(b_r)],
        out_specs=[pl.BlockSpec((tm, 1, dm), lambda i: (i, 0, 0)), pl.BlockSpec((tm, LANES), rows),
                   pl.BlockSpec((tm, LANES), rows), pl.BlockSpec((8, LANES), const)],
        out_shape=[jax.ShapeDtypeStruct((t, 1, dm), F32), jax.ShapeDtypeStruct((t, LANES), jnp.int32),
                   jax.ShapeDtypeStruct((t, LANES), F32), jax.ShapeDtypeStruct((8, LANES), F32)],
        scratch_shapes=[pltpu.VMEM((1, LANES), F32)],
        compiler_params=pltpu.CompilerParams(
            dimension_semantics=("arbitrary",), vmem_limit_bytes=48 * 1024 * 1024),
        name="postmix_router",
    )(x2, ya, yb, ga, gb, ln_g, ln_b, w_glu, b_glu, w_a, w_b, w_o, l1g, l1b, wr_hi, wr_lo, b_r)


MOE_ROW_TILE = 512
MOE_COL_CHUNK = 256


def _moe_kernel(te_ref, nv_ref, inv_ref, h_hbm, wg_ref, bg_ref, wu_ref, bu_ref, wd_ref, bd_ref, o_hbm,
                xbuf_ref, ybuf_ref, x2_ref, xb_ref, act_ref, wgb_ref, wub_ref, wdb_ref, sem_g, sem_s,
                *, n_tokens):
    i = pl.program_id(0)
    n = pl.num_programs(0)
    _, tm, _, dm = xbuf_ref.shape
    cur = i % 2
    oth = (i + 1) % 2

    def start_gather(tile, slot):
        base = tile * tm

        def body(r, c):
            s = inv_ref[base + r]
            tok = s & (n_tokens - 1) if n_tokens & (n_tokens - 1) == 0 else s % n_tokens
            pltpu.make_async_copy(h_hbm.at[tok], xbuf_ref.at[slot, r], sem_g.at[slot]).start()
            return c
        lax.fori_loop(0, tm, body, 0, unroll=8)

    def wait_gather(slot):
        pltpu.make_async_copy(h_hbm.at[pl.ds(0, tm)], xbuf_ref.at[slot], sem_g.at[slot]).wait()

    def start_scatter(tile, slot):
        base = tile * tm

        def body(r, c):
            pltpu.make_async_copy(ybuf_ref.at[slot, r], o_hbm.at[inv_ref[base + r]], sem_s.at[slot]).start()
            return c
        lax.fori_loop(0, tm, body, 0, unroll=8)

    def wait_scatter(slot):
        pltpu.make_async_copy(ybuf_ref.at[slot], o_hbm.at[pl.ds(0, tm)], sem_s.at[slot]).wait()

    @pl.when(i == 0)
    def _():
        ybuf_ref[...] = jnp.zeros_like(ybuf_ref)
        start_gather(0, 0)

    valid = i < nv_ref[0]
    new_expert = jnp.logical_or(i == 0, te_ref[i] != te_ref[jnp.maximum(i - 1, 0)])

    @pl.when(jnp.logical_and(valid, new_expert))
    def _():
        wgb_ref[...] = wg_ref[0].astype(BF16)
        wub_ref[...] = wu_ref[0].astype(BF16)
        wdb_ref[...] = wd_ref[0].astype(BF16)

    def row_copies():
        wait_gather(cur)
        start_gather(jnp.minimum(i + 1, n - 1), oth)
        start_scatter(jnp.maximum(i - 1, 0), oth)

    def start_gather_group(g, n_groups):
        base = jnp.minimum(i + 1, n - 1) * tm
        per = tm // n_groups
        for r in range(g * per, (g + 1) * per):
            s = inv_ref[base + r]
            tok = s & (n_tokens - 1) if n_tokens & (n_tokens - 1) == 0 else s % n_tokens
            pltpu.make_async_copy(h_hbm.at[tok], xbuf_ref.at[oth, r], sem_g.at[oth]).start()

    def start_scatter_group(g, n_groups):
        base = jnp.maximum(i - 1, 0) * tm
        per = tm // n_groups
        for r in range(g * per, (g + 1) * per):
            pltpu.make_async_copy(ybuf_ref.at[oth, r], o_hbm.at[inv_ref[base + r]], sem_s.at[oth]).start()

    @pl.when(valid)
    def _():
        de = wgb_ref.shape[1]
        nc_up, nc_down = de // MOE_COL_CHUNK, dm // MOE_COL_CHUNK
        wait_gather(cur)
        x2_ref[...] = xbuf_ref[cur].reshape(x2_ref.shape)
        xb_ref[...] = x2_ref[...].astype(BF16)
        for c in range(nc_up):
            start_gather_group(c, nc_up)
            cols = slice(c * MOE_COL_CHUNK, (c + 1) * MOE_COL_CHUNK)
            gate = jnp.minimum(jnp.dot(xb_ref[...], wgb_ref[:, cols], preferred_element_type=F32)
                               + bg_ref[0][:, cols], SWIGLU_LIMIT)
            up = jnp.clip(jnp.dot(xb_ref[...], wub_ref[:, cols], preferred_element_type=F32)
                          + bu_ref[0][:, cols], -SWIGLU_LIMIT, SWIGLU_LIMIT)
            act_ref[:, cols] = ((up + 1.0) * gate * _sigmoid(SWIGLU_ALPHA * gate)).astype(BF16)
        for c in range(nc_down):
            start_scatter_group(c, nc_down)
            cols = slice(c * MOE_COL_CHUNK, (c + 1) * MOE_COL_CHUNK)
            x2_ref[:, cols] = jnp.dot(act_ref[...], wdb_ref[:, cols], preferred_element_type=F32) \
                + bd_ref[0][:, cols]

        @pl.when(i == 0)
        def _():
            wait_scatter(oth)

        @pl.when(i >= 2)
        def _():
            wait_scatter(cur)
        ybuf_ref[cur] = x2_ref[...].reshape(tm, 1, dm)

    @pl.when(jnp.logical_not(valid))
    def _():
        @pl.when(i >= 2)
        def _():
            wait_scatter(cur)
        row_copies()

    @pl.when(i == n - 1)
    def _():
        wait_gather(oth)
        wait_scatter(oth)
        start_scatter(i, cur)
        wait_scatter(cur)


def _moe(tile_expert, n_valid, inv, h1_rows, w_gate, b_gate, w_up, b_up, w_down, b_down, *, tm):
    t, _, dm = h1_rows.shape
    _, _, de = w_gate.shape
    n_rows = inv.shape[0]
    per_e = lambda i, te, nv, iv: (te[i], 0, 0)
    return pl.pallas_call(
        functools.partial(_moe_kernel, n_tokens=t),
        grid_spec=pltpu.PrefetchScalarGridSpec(
            num_scalar_prefetch=3, grid=(n_rows // tm,),
            in_specs=[pl.BlockSpec(memory_space=pl.ANY),
                      pl.BlockSpec((1, dm, de), per_e), pl.BlockSpec((1, 1, de), per_e),
                      pl.BlockSpec((1, dm, de), per_e), pl.BlockSpec((1, 1, de), per_e),
                      pl.BlockSpec((1, de, dm), per_e), pl.BlockSpec((1, 1, dm), per_e)],
            out_specs=pl.BlockSpec(memory_space=pl.ANY),
            scratch_shapes=[pltpu.VMEM((2, tm, 1, dm), F32), pltpu.VMEM((2, tm, 1, dm), F32),
                            pltpu.VMEM((tm, dm), F32), pltpu.VMEM((tm, dm), BF16), pltpu.VMEM((tm, de), BF16),
                            pltpu.VMEM((dm, de), BF16),
                            pltpu.VMEM((dm, de), BF16), pltpu.VMEM((de, dm), BF16),
                            pltpu.SemaphoreType.DMA((2,)), pltpu.SemaphoreType.DMA((2,))]),
        out_shape=jax.ShapeDtypeStruct((n_rows, 1, dm), F32),
        compiler_params=pltpu.CompilerParams(
            dimension_semantics=("arbitrary",), vmem_limit_bytes=56 * 1024 * 1024),
        name="moe_experts",
    )(tile_expert, n_valid, inv, h1_rows, w_gate, b_gate, w_up, b_up, w_down, b_down)


def _combine_kernel(*refs, alpha):
    y_refs = refs[:TOP_K]
    h1_ref, rw_ref, l2g_ref, l2b_ref, o_ref, rows2_ref = refs[TOP_K:]
    tm, dm = o_ref.shape
    rw = rw_ref[...]
    ffn = jnp.zeros((tm, dm), F32)
    for kk in range(TOP_K):
        rows2_ref[...] = y_refs[kk][...].reshape(tm, dm)
        ffn = ffn + rw[:, kk:kk + 1] * rows2_ref[...]
    rows2_ref[...] = h1_ref[...].reshape(tm, dm)
    o_ref[...] = _layer_norm(alpha * rows2_ref[...] + ffn, l2g_ref[...], l2b_ref[...])


def _combine(y_slots, h1_rows, rw, l2g, l2b, *, tm, alpha):
    t, _, dm = h1_rows.shape
    rows = lambda i: (i, 0)
    const = lambda i: (0, 0)
    slot_specs = [pl.BlockSpec((tm, 1, dm), functools.partial(lambda i, kk: (kk * (t // tm) + i, 0, 0), kk=kk))
                  for kk in range(TOP_K)]
    return pl.pallas_call(
        functools.partial(_combine_kernel, alpha=alpha),
        grid=(t // tm,),
        in_specs=slot_specs + [pl.BlockSpec((tm, 1, dm), lambda i: (i, 0, 0)),
                               pl.BlockSpec((tm, LANES), rows),
                               pl.BlockSpec((1, dm), const), pl.BlockSpec((1, dm), const)],
        out_specs=pl.BlockSpec((tm, dm), rows),
        out_shape=jax.ShapeDtypeStruct((t, dm), F32),
        scratch_shapes=[pltpu.VMEM((tm, dm), F32)],
        compiler_params=pltpu.CompilerParams(
            dimension_semantics=("arbitrary",), vmem_limit_bytes=48 * 1024 * 1024),
        name="moe_combine_ln2",
    )(*([y_slots] * TOP_K), h1_rows, rw, l2g, l2b)


INVERT_CHUNK = 8192


def _invert_kernel(rows_ref, inv_ref, *, n_tokens):
    i = pl.program_id(0)
    ch = rows_ref.shape[0]
    first = i * ch

    @pl.when(first < TOP_K * n_tokens)
    def _():
        t0 = first // TOP_K

        def body(m, c):
            base = t0 + 2 * m
            for q in range(2 * TOP_K):
                inv_ref[rows_ref[2 * TOP_K * m + q]] = base + ((q % TOP_K) * n_tokens + q // TOP_K)
            return c
        lax.fori_loop(0, ch // (2 * TOP_K), body, 0, unroll=4)

    @pl.when(first >= TOP_K * n_tokens)
    def _():
        def body(p, c):
            inv_ref[rows_ref[p]] = first + p
            return c
        lax.fori_loop(0, ch, body, 0, unroll=8)


def _invert(rows, n_tokens):
    n = rows.shape[0]
    ch = math.gcd(math.gcd(n, TOP_K * n_tokens), INVERT_CHUNK)
    assert ch % TOP_K == 0
    return pl.pallas_call(
        functools.partial(_invert_kernel, n_tokens=n_tokens),
        grid=(n // ch,),
        in_specs=[pl.BlockSpec((ch,), lambda i: (i,), memory_space=pltpu.SMEM)],
        out_specs=pl.BlockSpec(memory_space=pltpu.SMEM),
        out_shape=jax.ShapeDtypeStruct((n,), jnp.int32),
        compiler_params=pltpu.CompilerParams(dimension_semantics=("arbitrary",)),
        name="moe_row_slots",
    )(rows)


def _routing_tables(ridx, counts, *, n_experts, tm):
    t = ridx.shape[0]
    n_real = TOP_K * t
    n_fill = n_experts * tm
    e_idx = ridx[:, :TOP_K]
    rank = ridx[:, TOP_K:2 * TOP_K]
    cnt = counts[0, :n_experts].astype(jnp.int32)
    padded = (cnt + tm - 1) // tm * tm
    ends = jnp.cumsum(padded)
    offs = ends - padded
    experts = jnp.arange(n_experts, dtype=jnp.int32)
    base = jnp.sum(jnp.where(e_idx[..., None] == experts, offs, 0), axis=-1)
    token_rows = (base + rank).reshape(-1)
    fill_cnt = padded - cnt
    fill_end = jnp.cumsum(fill_cnt)
    shift = jnp.concatenate([offs + cnt - (fill_end - fill_cnt), jnp.full((1,), n_real, jnp.int32)])
    j = jnp.arange(n_fill, dtype=jnp.int32)
    fill_rows = j + shift[0] + jnp.sum(
        jnp.where(j[:, None] >= fill_end[None, :], (shift[1:] - shift[:-1])[None, :], 0), axis=1)
    rows = jnp.concatenate([token_rows, fill_rows])
    inv = _invert(rows, t)
    n_tiles = (n_real + n_fill) // tm
    tile_start = jnp.arange(n_tiles, dtype=jnp.int32) * tm
    tile_expert = jnp.minimum(jnp.sum(ends[None, :] <= tile_start[:, None], axis=1), n_experts - 1)
    n_valid = (ends[-1] // tm).reshape(1)
    return inv.astype(jnp.int32), tile_expert.astype(jnp.int32), n_valid.astype(jnp.int32)


def _pick_tile(n, want):
    t = min(n, want)
    assert n % t == 0, (n, t)
    return t


def kernel(x, meta, ln_in_g, ln_in_b, w_in, b_f, w_up_a, w_up_b, w_o, ssm_a_re, ssm_a_im, ssm_log_dt,
           ssm_b_re, ssm_b_im, ssm_c_re, ssm_c_im, ssm_d, w_glu, b_glu, ln1_g, ln1_b, w_router, b_router,
           w_gate, b_gate, w_up, b_up, w_down, b_down, ln2_g, ln2_b):
    bsz, seq, dm = x.shape
    depth = w_in.shape[0]
    assert depth == 1 and meta.shape[0] == N_META
    alpha = (2.0 * depth) ** 0.25
    n_groups = ssm_a_re.shape[1]
    n_experts = w_router.shape[-1]
    aw = ATT_WIDTH
    t = bsz * seq
    x2 = x.reshape(t, dm)
    row = lambda a: a.reshape(1, -1)

    f_off = 3 * aw
    u_off = f_off + ATT_HEADS
    w0 = w_in[0]
    w_main = jnp.concatenate([w0[:, :f_off], w0[:, u_off:]], axis=1).astype(BF16)
    w_f = jnp.pad(w0[:, f_off:u_off], ((0, 0), (0, LANES - ATT_HEADS))).astype(BF16)
    b_f_pad = jnp.pad(b_f[0], (0, LANES - ATT_HEADS)).reshape(1, LANES)
    ln_g, ln_b = row(ln_in_g), row(ln_in_b)

    tm = _pick_tile(seq, 512)
    q_t, k, v_t, u, ga, gb, _ = _inproj(x2, ln_g, ln_b, w_main, w_f, b_f_pad, tm=tm, tiles_per_seq=seq // tm,
                                        transpose_qv=True)
    _, k_m, v_m, u_m, _, _, fcum_m = _inproj(meta, ln_g, ln_b, w_main, w_f, b_f_pad, tm=N_META, tiles_per_seq=1,
                                             transpose_qv=False)

    pad_m = META_KEYS_PAD - N_META
    fm = fcum_m[:, :ATT_HEADS]
    bias_m = jnp.pad(-LOG2E * (fm - fm[N_META - 1:N_META, :]), ((0, pad_m), (0, 0)), constant_values=-BIG)
    k_meta = jnp.pad(k_m, ((0, pad_m), (0, 0))).reshape(META_KEYS_PAD, ATT_HEADS, LANES)
    lane_m = jnp.arange(LANES)
    for n_piece, piece in enumerate(_split3(bias_m)):
        k_meta = jnp.where(lane_m == HEAD_DIM + n_piece, piece[..., None], k_meta)
    k_meta = k_meta.reshape(META_KEYS_PAD, ATT_HEADS * LANES)
    v_meta_t = jnp.pad(v_m, ((0, pad_m), (0, 0))).T
    tq = _pick_tile(seq, 512)
    y_a = _attn(q_t, k, v_t, k_meta, v_meta_t, bsz=bsz, seq=seq, tq=tq)

    chunk = SSM_CHUNK
    n_chunks = seq // chunk
    n_steps = max(1, (n_chunks - 1).bit_length())
    kt, wst, wout, wmeta, apow, d_tiled = _ssm_tables(
        ssm_a_re[0], ssm_a_im[0], ssm_log_dt[0], ssm_b_re[0], ssm_b_im[0], ssm_c_re[0], ssm_c_im[0], ssm_d[0],
        chunk=chunk, n_steps=n_steps)
    u_g = jnp.transpose(u.reshape(bsz * n_chunks, chunk, n_groups, SSM_GROUP), (2, 0, 1, 3))
    u_g = u_g.reshape(n_groups, bsz * n_chunks, chunk * SSM_GROUP)
    um_g = jnp.transpose(u_m.reshape(N_META, n_groups, SSM_GROUP), (1, 0, 2)).reshape(n_groups, 1, N_META * SSM_GROUP)
    um_g = jnp.pad(um_g, ((0, 0), (0, 15), (0, 0)))
    y_g = _ssm(u_g, kt, wst, wout, um_g, wmeta, apow, d_tiled, chunks_per_seq=n_chunks)
    y_b = jnp.transpose(y_g.reshape(n_groups, bsz * n_chunks, chunk, SSM_GROUP), (1, 2, 0, 3)).reshape(t, -1)

    wr = jnp.pad(w_router[0], ((0, 0), (0, LANES - n_experts)))
    wr_hi = wr.astype(BF16)
    wr_lo = (wr - wr_hi.astype(F32)).astype(BF16)
    b_r = jnp.pad(b_router[0], (0, LANES - n_experts)).reshape(1, LANES)
    tm2 = _pick_tile(t, 512)
    h1_rows, ridx, rw, counts = _postmix(
        x2, y_a, y_b, ga, gb, ln_g, ln_b, w_glu[0].astype(BF16), row(b_glu[0]),
        w_up_a[0].astype(BF16), w_up_b[0].astype(BF16), w_o[0].astype(BF16), row(ln1_g[0]), row(ln1_b[0]),
        wr_hi, wr_lo, b_r, tm=tm2, alpha=alpha, n_experts=n_experts)

    tm3 = MOE_ROW_TILE
    inv, tile_expert, n_valid = _routing_tables(ridx, counts, n_experts=n_experts, tm=tm3)
    y_slots = _moe(tile_expert, n_valid, inv, h1_rows, w_gate[0], b_gate[0][:, None, :], w_up[0],
                   b_up[0][:, None, :], w_down[0], b_down[0][:, None, :], tm=tm3)
    tm4 = _pick_tile(t, 256)
    out = _combine(y_slots, h1_rows, rw, row(ln2_g[0]), row(ln2_b[0]), tm=tm4, alpha=alpha)
    return out.reshape(bsz, seq, dm)
```

```python
import functools
import math

import jax
import jax.numpy as jnp
from jax import lax
from jax.experimental import pallas as pl
from jax.experimental.pallas import tpu as pltpu

F32 = jnp.float32
BF16 = jnp.bfloat16

N_META = 16
ATT_HEADS = 8
HEAD_DIM = 64
ATT_WIDTH = ATT_HEADS * HEAD_DIM
SSM_GROUP = 16
SSM_STATE = 64
TOP_K = 4
SWIGLU_LIMIT = 7.0
SWIGLU_ALPHA = 1.702
LN_EPS = 1e-5

LANES = 128
SSM_CHUNK = 64
META_KEYS_PAD = LANES
BIG = 1e30
LOG2E = 1.4426950408889634


def _layer_norm(x, g, b):
    mu = jnp.mean(x, axis=-1, keepdims=True)
    xc = x - mu
    var = jnp.mean(xc * xc, axis=-1, keepdims=True)
    return xc * lax.rsqrt(var + LN_EPS) * g + b


def _log_sigmoid(z):
    return jnp.minimum(z, 0.0) - jnp.log1p(jnp.exp(-jnp.abs(z)))


def _sigmoid(z):
    return 1.0 / (1.0 + jnp.exp(-z))


def _split3(x):
    hi = x.astype(BF16)
    r1 = x - hi.astype(F32)
    mid = r1.astype(BF16)
    lo = (r1 - mid.astype(F32)).astype(BF16)
    return hi, mid, lo


def _inproj_kernel(x_ref, g_ref, b_ref, w_ref, wf_ref, bf_ref,
                   q_ref, k_ref, v_ref, u_ref, ga_ref, gb_ref, fc_ref, carry_ref,
                   *, tiles_per_seq, transpose_qv):
    i = pl.program_id(0)
    tm = x_ref.shape[0]
    h = _layer_norm(x_ref[...], g_ref[...], b_ref[...]).astype(BF16)

    def proj(lo, hi):
        return jnp.dot(h, w_ref[:, lo:hi], preferred_element_type=F32)

    aw = ATT_WIDTH
    u_ref[...] = proj(3 * aw, 4 * aw).astype(BF16)
    dm = ga_ref.shape[1]
    ga_ref[...] = _sigmoid(proj(4 * aw, 4 * aw + dm)).astype(BF16)
    gb_ref[...] = _sigmoid(proj(4 * aw + dm, 4 * aw + 2 * dm)).astype(BF16)

    zf = jnp.dot(h, wf_ref[...], preferred_element_type=F32) + bf_ref[...]
    lf = _log_sigmoid(zf)

    @pl.when(i % tiles_per_seq == 0)
    def _():
        carry_ref[...] = jnp.zeros_like(carry_ref)

    row = lax.broadcasted_iota(jnp.int32, (tm, tm), 0)
    col = lax.broadcasted_iota(jnp.int32, (tm, tm), 1)
    tri = (col <= row).astype(BF16)
    hi, mid, lo = _split3(lf)
    cs = (jnp.dot(tri, hi, preferred_element_type=F32)
          + jnp.dot(tri, mid, preferred_element_type=F32)
          + jnp.dot(tri, lo, preferred_element_type=F32)) + carry_ref[...]
    fc_ref[...] = cs
    carry_ref[...] = cs[tm - 1:tm, :]

    lane = lax.broadcasted_iota(jnp.int32, (1, LANES), 1)
    own = lane < HEAD_DIM
    bias_lanes = jnp.logical_and(lane >= HEAD_DIM, lane < HEAD_DIM + 3)
    pieces = [p.astype(F32) for p in _split3(-LOG2E * cs)]

    def slabs(x, extra, transposed=False):
        out = []
        for hp in range(ATT_HEADS // 2):
            x2 = x[:, LANES * hp:LANES * (hp + 1)]
            out.append(jnp.where(own, x2, extra(2 * hp)))
            out.append(jnp.where(own, pltpu.roll(x2, HEAD_DIM, axis=1), extra(2 * hp + 1)))
        if transposed:
            out = [o.T for o in out]
            return jnp.concatenate(out, axis=0).astype(BF16)
        return jnp.concatenate(out, axis=1).astype(BF16)

    def bias_extra(h):
        e = jnp.zeros((tm, LANES), F32)
        for j, piece in enumerate(pieces):
            e = jnp.where(lane == HEAD_DIM + j, pltpu.roll(piece, (HEAD_DIM + j - h) % LANES, axis=1), e)
        return e

    q_ref[...] = slabs(proj(0, aw) * (LOG2E / math.sqrt(HEAD_DIM)),
                       lambda h: jnp.broadcast_to(jnp.where(bias_lanes, 1.0, 0.0), (tm, LANES)), transpose_qv)
    k_ref[...] = slabs(proj(aw, 2 * aw), bias_extra)
    v_ref[...] = slabs(proj(2 * aw, 3 * aw), lambda h: jnp.ones((tm, LANES), F32), transpose_qv)


def _inproj(x2, ln_g, ln_b, w_main, w_f, b_f, *, tm, tiles_per_seq, transpose_qv):
    t, dm = x2.shape
    aw = ATT_WIDTH
    n_main = w_main.shape[1]
    const = lambda i: (0, 0)
    rows = lambda i: (i, 0)
    slab = ATT_HEADS * LANES
    row_slab = (jax.ShapeDtypeStruct((t, slab), BF16), pl.BlockSpec((tm, slab), rows))
    col_slab = (jax.ShapeDtypeStruct((slab, t), BF16), pl.BlockSpec((slab, tm), lambda i: (0, i)))
    qv = col_slab if transpose_qv else row_slab
    out_shape = [qv[0], row_slab[0], qv[0], jax.ShapeDtypeStruct((t, aw), BF16)] + \
                [jax.ShapeDtypeStruct((t, dm), BF16)] * 2 + [jax.ShapeDtypeStruct((t, LANES), F32)]
    out_specs = [qv[1], row_slab[1], qv[1], pl.BlockSpec((tm, aw), rows)] + \
                [pl.BlockSpec((tm, dm), rows)] * 2 + [pl.BlockSpec((tm, LANES), rows)]
    return pl.pallas_call(
        functools.partial(_inproj_kernel, tiles_per_seq=tiles_per_seq, transpose_qv=transpose_qv),
        grid=(t // tm,),
        in_specs=[pl.BlockSpec((tm, dm), rows),
                  pl.BlockSpec((1, dm), const), pl.BlockSpec((1, dm), const),
                  pl.BlockSpec((dm, n_main), const),
                  pl.BlockSpec((dm, LANES), const), pl.BlockSpec((1, LANES), const)],
        out_specs=out_specs,
        out_shape=out_shape,
        scratch_shapes=[pltpu.VMEM((1, LANES), F32)],
        compiler_params=pltpu.CompilerParams(
            dimension_semantics=("arbitrary",), vmem_limit_bytes=56 * 1024 * 1024),
        name="inproj",
    )(x2, ln_g, ln_b, w_main, w_f, b_f)


def _attn_kernel(qi_tab, ki_tab, qt_ref, k_ref, vt_ref, km_ref, vmt_ref, o_ref, m_ref, acc_ref, st_ref):
    p = pl.program_id(1)
    qi = qi_tab[p]
    ki = ki_tab[p]
    tq = qt_ref.shape[1]

    def process(k_blk, vt_blk, mask):
        n = k_blk.shape[0]

        def logits(h):
            sl = slice(LANES * h, LANES * (h + 1))
            st = jnp.dot(k_blk[:, sl], qt_ref[sl, :], preferred_element_type=F32)
            if mask is not None:
                st = jnp.where(mask, st, -jnp.inf)
            st_ref[h % 2, 0:n] = st
            m_old = m_ref[h]
            m_new = jnp.maximum(m_old, jnp.max(st_ref[h % 2, 0:n], axis=0, keepdims=True))
            m_ref[h] = m_new
            return m_old, m_new

        def weigh(h, m_old, m_new):
            sl = slice(LANES * h, LANES * (h + 1))
            pt = jnp.exp2(st_ref[h % 2, 0:n] - m_new[0:1, :])
            alpha = jnp.exp2(m_old - m_new)
            acc_ref[h] = alpha[0:1, :] * acc_ref[h] + jnp.dot(
                vt_blk[sl, :], pt.astype(BF16), preferred_element_type=F32)

        ms = logits(0)
        for h in range(ATT_HEADS):
            nxt = logits(h + 1) if h + 1 < ATT_HEADS else None
            weigh(h, *ms)
            ms = nxt

    @pl.when(ki == 0)
    def _():
        m_ref[...] = jnp.full_like(m_ref, -jnp.inf)
        acc_ref[...] = jnp.zeros_like(acc_ref)
        process(km_ref[...], vmt_ref[...], None)

    @pl.when(ki < qi)
    def _():
        process(k_ref[...], vt_ref[...], None)

    @pl.when(ki == qi)
    def _():
        tk = k_ref.shape[0]
        key = lax.broadcasted_iota(jnp.int32, (tk, tq), 0)
        qry = lax.broadcasted_iota(jnp.int32, (tk, tq), 1)
        process(k_ref[...], vt_ref[...], key <= qry)
        for hp in range(ATT_HEADS // 2):
            a0, a1 = acc_ref[2 * hp], acc_ref[2 * hp + 1]
            pair = jnp.concatenate([a0[:HEAD_DIM] / a0[HEAD_DIM:], a1[:HEAD_DIM] / a1[HEAD_DIM:]], axis=0)
            o_ref[:, LANES * hp:LANES * (hp + 1)] = pair.T.astype(o_ref.dtype)


def _attn(q_t, k, v_t, k_meta, v_meta_t, *, bsz, seq, tq):
    nq = seq // tq
    pairs = [(a, b) for a in range(nq) for b in range(a + 1)]
    qi_tab = jnp.asarray([a for a, _ in pairs], jnp.int32)
    ki_tab = jnp.asarray([b for _, b in pairs], jnp.int32)
    w = k.shape[1]
    qcol = lambda b, p, qt, kt: (0, b * nq + qt[p])
    kcol = lambda b, p, qt, kt: (0, b * nq + kt[p])
    krow = lambda b, p, qt, kt: (b * nq + kt[p], 0)
    const = lambda b, p, qt, kt: (0, 0)
    grid_spec = pltpu.PrefetchScalarGridSpec(
        num_scalar_prefetch=2,
        grid=(bsz, len(pairs)),
        in_specs=[pl.BlockSpec((w, tq), qcol),
                  pl.BlockSpec((tq, w), krow),
                  pl.BlockSpec((w, tq), kcol),
                  pl.BlockSpec((META_KEYS_PAD, w), const),
                  pl.BlockSpec((w, META_KEYS_PAD), const)],
        out_specs=pl.BlockSpec((tq, ATT_WIDTH), lambda b, p, qt, kt: (b * nq + qt[p], 0)),
        scratch_shapes=[pltpu.VMEM((ATT_HEADS, 8, tq), F32),
                        pltpu.VMEM((ATT_HEADS, LANES, tq), F32),
                        pltpu.VMEM((2, tq, tq), F32)],
    )
    return pl.pallas_call(
        _attn_kernel,
        grid_spec=grid_spec,
        out_shape=jax.ShapeDtypeStruct((bsz * seq, ATT_WIDTH), BF16),
        compiler_params=pltpu.CompilerParams(
            dimension_semantics=("arbitrary", "arbitrary"), vmem_limit_bytes=48 * 1024 * 1024),
        name="fox_attention",
    )(qi_tab, ki_tab, q_t, k, v_t, k_meta, v_meta_t)


def _ssm_kernel(u_ref, kt_ref, wst_ref, wout_ref, um_ref, wm_ref, apow_ref, d_ref, y_ref, toep_ref,
                *, chunks_per_seq, n_steps):
    u = u_ref[0]
    rows = u.shape[0]
    p2 = 2 * SSM_STATE
    kt = kt_ref[0]
    n_c, tc = kt.shape
    per_vreg = LANES // n_c
    ext = jnp.concatenate([jnp.zeros_like(kt), kt], axis=1)
    for sub in range(per_vreg):
        rot = ext if sub == 0 else pltpu.roll(ext, n_c * sub, axis=1)
        for whole in range(tc // LANES):
            tp = per_vreg * whole + sub
            toep_ref[tp * n_c:(tp + 1) * n_c, :] = rot[:, tc - LANES * whole:2 * tc - LANES * whole].astype(BF16)
    y = jnp.dot(u, toep_ref[...], preferred_element_type=F32)
    s = jnp.dot(u, wst_ref[0], preferred_element_type=F32)
    x0 = jnp.dot(um_ref[0], wm_ref[0], preferred_element_type=F32)[0:1, :]

    lane = lax.broadcasted_iota(jnp.int32, (1, p2), 1)
    re_half = lane < SSM_STATE
    j = lax.broadcasted_iota(jnp.int32, (rows, 1), 0) % chunks_per_seq

    def cmul(step, z):
        ar = apow_ref[0, step:step + 1, :]
        ai = apow_ref[0, n_steps + step:n_steps + step + 1, :]
        return ar * z + ai * pltpu.roll(z, SSM_STATE, axis=1)

    s = s + jnp.where(j == 0, cmul(0, jnp.broadcast_to(x0, s.shape)), 0.0)
    for step in range(n_steps):
        sh = 1 << step
        prev = pltpu.roll(s, sh, axis=0)
        s = s + jnp.where(j >= sh, cmul(step, prev), 0.0)
    x_in = jnp.where(j == 0, x0, pltpu.roll(s, 1, axis=0))
    del re_half
    y = y + jnp.dot(x_in.astype(BF16), wout_ref[0], preferred_element_type=F32)
    y = y + d_ref[0] * u.astype(F32)
    y_ref[0] = jax.nn.gelu(y).astype(y_ref.dtype)


def _ssm(u_g, kt, wst, wout, u_meta, wmeta, apow, d_tiled, *, chunks_per_seq):
    g, rows, tc = u_g.shape
    n_steps = apow.shape[1] // 2
    grp = lambda i: (i, 0, 0)
    return pl.pallas_call(
        functools.partial(_ssm_kernel, chunks_per_seq=chunks_per_seq, n_steps=n_steps),
        grid=(g,),
        in_specs=[pl.BlockSpec((1, rows, tc), grp),
                  pl.BlockSpec((1,) + kt.shape[1:], grp),
                  pl.BlockSpec((1, tc, 2 * SSM_STATE), grp),
                  pl.BlockSpec((1, 2 * SSM_STATE, tc), grp),
                  pl.BlockSpec((1,) + u_meta.shape[1:], grp),
                  pl.BlockSpec((1,) + wmeta.shape[1:], grp),
                  pl.BlockSpec((1,) + apow.shape[1:], grp),
                  pl.BlockSpec((1, 1, tc), grp)],
        out_specs=pl.BlockSpec((1, rows, tc), grp),
        out_shape=jax.ShapeDtypeStruct((g, rows, tc), BF16),
        scratch_shapes=[pltpu.VMEM((tc, tc), BF16)],
        compiler_params=pltpu.CompilerParams(
            dimension_semantics=("arbitrary",), vmem_limit_bytes=48 * 1024 * 1024),
        name="s5_ssm",
    )(u_g, kt, wst, wout, u_meta, wmeta, apow, d_tiled)


def _ssm_tables(a_re, a_im, log_dt, b_re, b_im, c_re, c_im, d_skip, *, chunk, n_steps):
    g, p = a_re.shape
    c = b_re.shape[-1]
    dt = jnp.exp(log_dt)[:, None]
    mag = jnp.exp(a_re * dt)
    ang = a_im * dt
    lb_re, lb_im = mag * jnp.cos(ang), mag * jnp.sin(ang)
    den = a_re * a_re + a_im * a_im
    z_re, z_im = lb_re - 1.0, lb_im
    coef_re = (z_re * a_re + z_im * a_im) / den
    coef_im = (z_im * a_re - z_re * a_im) / den
    bb_re = coef_re[..., None] * b_re - coef_im[..., None] * b_im
    bb_im = coef_re[..., None] * b_im + coef_im[..., None] * b_re

    def step(carry, _):
        cr, ci = carry
        return (cr * lb_re - ci * lb_im, cr * lb_im + ci * lb_re), (cr, ci)
    _, (pw_re, pw_im) = lax.scan(step, (jnp.ones_like(lb_re), jnp.zeros_like(lb_re)), None,
                                 length=chunk + 1)

    e_re = c_re[None] * pw_re[:, :, None, :] - c_im[None] * pw_im[:, :, None, :]
    e_im = c_re[None] * pw_im[:, :, None, :] + c_im[None] * pw_re[:, :, None, :]
    kern = (jnp.einsum('tgcp,gpd->tgcd', e_re[:chunk], bb_re)
            - jnp.einsum('tgcp,gpd->tgcd', e_im[:chunk], bb_im))
    kt = jnp.transpose(kern, (1, 3, 0, 2)).reshape(g, c, chunk * c)

    def in_to_state(n):
        wr = pw_re[n - 1::-1][:n, :, :, None] * bb_re[None] - pw_im[n - 1::-1][:n, :, :, None] * bb_im[None]
        wi = pw_re[n - 1::-1][:n, :, :, None] * bb_im[None] + pw_im[n - 1::-1][:n, :, :, None] * bb_re[None]
        w = jnp.concatenate([wr, wi], axis=2)
        return jnp.transpose(w, (1, 0, 3, 2)).reshape(g, n * c, 2 * p)
    wst = in_to_state(chunk)
    wmeta = in_to_state(N_META)

    wout = jnp.concatenate([e_re[1:chunk + 1], -e_im[1:chunk + 1]], axis=3)
    wout = jnp.transpose(wout, (1, 3, 0, 2)).reshape(g, 2 * p, chunk * c)

    ar, ai = pw_re[chunk], pw_im[chunk]
    rows_r, rows_i = [], []
    for _ in range(n_steps):
        rows_r.append(jnp.concatenate([ar, ar], axis=-1))
        rows_i.append(jnp.concatenate([-ai, ai], axis=-1))
        ar, ai = ar * ar - ai * ai, 2.0 * ar * ai
    apow = jnp.stack(rows_r + rows_i, axis=1)
    d_tiled = jnp.tile(d_skip, (1, chunk))[:, None, :]
    return kt, wst.astype(BF16), wout.astype(BF16), wmeta.astype(BF16), apow, d_tiled


def _postmix_kernel(x_ref, ya_ref, yb_ref, ga_ref, gb_ref, lng_ref, lnb_ref,
                    wglu_ref, bglu_ref, wa_ref, wb_ref, wo_ref, l1g_ref, l1b_ref,
                    wrh_ref, wrl_ref, br_ref, h1_ref, ridx_ref, rw_ref, cnt_ref, carry_ref,
                    *, alpha, n_experts):
    i = pl.program_id(0)
    tm = x_ref.shape[0]
    halves = [slice(0, tm // 2), slice(tm // 2, tm)]
    dot = functools.partial(jnp.dot, preferred_element_type=F32)
    glu = [yb_ref[r, :].astype(F32) * _sigmoid(dot(yb_ref[r, :], wglu_ref[...]) + bglu_ref[...]) for r in halves]
    att = [ga_ref[r, :].astype(F32) * dot(ya_ref[r, :], wa_ref[...]) for r in halves]
    merged = [att[j] + gb_ref[r, :].astype(F32) * dot(glu[j].astype(BF16), wb_ref[...])
              for j, r in enumerate(halves)]
    mix = [dot(m.astype(BF16), wo_ref[...]) for m in merged]
    h1s = [_layer_norm(alpha * _layer_norm(x_ref[r, :], lng_ref[...], lnb_ref[...]) + mix[j],
                       l1g_ref[...], l1b_ref[...]) for j, r in enumerate(halves)]
    parts = []
    for j, r in enumerate(halves):
        h1_ref[r] = h1s[j].reshape((tm // 2,) + h1_ref.shape[1:])
        hb = h1s[j].astype(BF16)
        hl = (h1s[j] - hb.astype(F32)).astype(BF16)
        parts.append(dot(hb, wrh_ref[...]) + dot(hb, wrl_ref[...]) + dot(hl, wrh_ref[...]))
    logits = jnp.concatenate(parts, axis=0) + br_ref[...]
    lane = lax.broadcasted_iota(jnp.int32, logits.shape, 1)
    logits = jnp.where(lane < n_experts, logits, -jnp.inf)
    denom = jnp.zeros((tm, 1), F32)
    top = None
    hits, idxs, ws = [], [], []
    for _ in range(TOP_K):
        mx = jnp.max(logits, axis=-1, keepdims=True)
        idx = jnp.min(jnp.where(logits == mx, lane, LANES), axis=-1, keepdims=True)
        hit = lane == idx
        if top is None:
            top = mx
        w = jnp.exp(mx - top)
        denom = denom + w
        hits.append(hit)
        idxs.append(idx)
        ws.append(w)
        logits = jnp.where(hit, -jnp.inf, logits)

    @pl.when(i == 0)
    def _():
        carry_ref[...] = jnp.zeros_like(carry_ref)

    sel = jnp.zeros(logits.shape, F32)
    for hit in hits:
        sel = sel + jnp.where(hit, 1.0, 0.0)
    row = lax.broadcasted_iota(jnp.int32, (tm, tm), 0)
    col = lax.broadcasted_iota(jnp.int32, (tm, tm), 1)
    before = (col < row).astype(BF16)
    seen = jnp.dot(before, sel.astype(BF16), preferred_element_type=F32) + carry_ref[...]
    ridx = jnp.zeros(logits.shape, jnp.int32)
    rw = jnp.zeros(logits.shape, F32)
    for kk in range(TOP_K):
        rank = jnp.sum(jnp.where(hits[kk], seen, 0.0), axis=-1, keepdims=True).astype(jnp.int32)
        ridx = jnp.where(lane == kk, idxs[kk], ridx)
        ridx = jnp.where(lane == TOP_K + kk, rank, ridx)
        rw = jnp.where(lane == kk, ws[kk] / denom, rw)
    ridx_ref[...] = ridx
    rw_ref[...] = rw
    total = carry_ref[...] + jnp.sum(sel, axis=0, keepdims=True)
    carry_ref[...] = total
    cnt_ref[...] = jnp.broadcast_to(total, cnt_ref.shape)


def _postmix(x2, ya, yb, ga, gb, ln_g, ln_b, w_glu, b_glu, w_a, w_b, w_o, l1g, l1b,
             wr_hi, wr_lo, b_r, *, tm, alpha, n_experts):
    t, dm = x2.shape
    sw = ya.shape[1]
    rows = lambda i: (i, 0)
    const = lambda i: (0, 0)
    full = lambda a: pl.BlockSpec(a.shape, const)
    return pl.pallas_call(
        functools.partial(_postmix_kernel, alpha=alpha, n_experts=n_experts),
        grid=(t // tm,),
        in_specs=[pl.BlockSpec((tm, dm), rows), pl.BlockSpec((tm, sw), rows), pl.BlockSpec((tm, sw), rows),
                  pl.BlockSpec((tm, dm), rows), pl.BlockSpec((tm, dm), rows),
                  full(ln_g), full(ln_b), full(w_glu), full(b_glu), full(w_a), full(w_b), full(w_o),
                  full(l1g), full(l1b), full(wr_hi), full(wr_lo), full(b_r)],
        out_specs=[pl.BlockSpec((tm, 1, dm), lambda i: (i, 0, 0)), pl.BlockSpec((tm, LANES), rows),
                   pl.BlockSpec((tm, LANES), rows), pl.BlockSpec((8, LANES), const)],
        out_shape=[jax.ShapeDtypeStruct((t, 1, dm), F32), jax.ShapeDtypeStruct((t, LANES), jnp.int32),
                   jax.ShapeDtypeStruct((t, LANES), F32), jax.ShapeDtypeStruct((8, LANES), F32)],
        scratch_shapes=[pltpu.VMEM((1, LANES), F32)],
        compiler_params=pltpu.CompilerParams(
            dimension_semantics=("arbitrary",), vmem_limit_bytes=48 * 1024 * 1024),
        name="postmix_router",
    )(x2, ya, yb, ga, gb, ln_g, ln_b, w_glu, b_glu, w_a, w_b, w_o, l1g, l1b, wr_hi, wr_lo, b_r)


MOE_ROW_TILE = 512
MOE_COL_CHUNK = 256


def _moe_kernel(te_ref, nv_ref, inv_ref, h_hbm, wg_ref, bg_ref, wu_ref, bu_ref, wd_ref, bd_ref, o_hbm,
                xbuf_ref, ybuf_ref, x2_ref, xb_ref, act_ref, wgb_ref, wub_ref, wdb_ref, sem_g, sem_s,
                *, n_tokens):
    i = pl.program_id(0)
    n = pl.num_programs(0)
    _, tm, _, dm = xbuf_ref.shape
    cur = i % 2
    oth = (i + 1) % 2

    def start_gather(tile, slot):
        base = tile * tm

        def body(r, c):
            s = inv_ref[base + r]
            tok = s & (n_tokens - 1) if n_tokens & (n_tokens - 1) == 0 else s % n_tokens
            pltpu.make_async_copy(h_hbm.at[tok], xbuf_ref.at[slot, r], sem_g.at[slot]).start()
            return c
        lax.fori_loop(0, tm, body, 0, unroll=8)

    def wait_gather(slot):
        pltpu.make_async_copy(h_hbm.at[pl.ds(0, tm)], xbuf_ref.at[slot], sem_g.at[slot]).wait()

    def start_scatter(tile, slot):
        base = tile * tm

        def body(r, c):
            pltpu.make_async_copy(ybuf_ref.at[slot, r], o_hbm.at[inv_ref[base + r]], sem_s.at[slot]).start()
            return c
        lax.fori_loop(0, tm, body, 0, unroll=8)

    def wait_scatter(slot):
        pltpu.make_async_copy(ybuf_ref.at[slot], o_hbm.at[pl.ds(0, tm)], sem_s.at[slot]).wait()

    @pl.when(i == 0)
    def _():
        ybuf_ref[...] = jnp.zeros_like(ybuf_ref)
        start_gather(0, 0)

    valid = i < nv_ref[0]
    new_expert = jnp.logical_or(i == 0, te_ref[i] != te_ref[jnp.maximum(i - 1, 0)])

    @pl.when(jnp.logical_and(valid, new_expert))
    def _():
        wgb_ref[...] = wg_ref[0].astype(BF16)
        wub_ref[...] = wu_ref[0].astype(BF16)
        wdb_ref[...] = wd_ref[0].astype(BF16)

    def start_gather_group(g, n_groups):
        base = jnp.minimum(i + 1, n - 1) * tm
        per = tm // n_groups
        for r in range(g * per, (g + 1) * per):
            s = inv_ref[base + r]
            tok = s & (n_tokens - 1) if n_tokens & (n_tokens - 1) == 0 else s % n_tokens
            pltpu.make_async_copy(h_hbm.at[tok], xbuf_ref.at[oth, r], sem_g.at[oth]).start()

    def start_scatter_group(g, n_groups):
        base = jnp.maximum(i - 1, 0) * tm
        per = tm // n_groups
        for r in range(g * per, (g + 1) * per):
            pltpu.make_async_copy(ybuf_ref.at[oth, r], o_hbm.at[inv_ref[base + r]], sem_s.at[oth]).start()

    @pl.when(valid)
    def _():
        de = wgb_ref.shape[1]
        nc_up, nc_down = de // MOE_COL_CHUNK, dm // MOE_COL_CHUNK
        wait_gather(cur)
        x2_ref[...] = xbuf_ref[cur].reshape(x2_ref.shape)
        xb_ref[...] = x2_ref[...].astype(BF16)
        for c in range(nc_up):
            start_gather_group(c, nc_up)
            cols = slice(c * MOE_COL_CHUNK, (c + 1) * MOE_COL_CHUNK)
            gate = jnp.minimum(jnp.dot(xb_ref[...], wgb_ref[:, cols], preferred_element_type=F32)
                               + bg_ref[0][:, cols], SWIGLU_LIMIT)
            up = jnp.clip(jnp.dot(xb_ref[...], wub_ref[:, cols], preferred_element_type=F32)
                          + bu_ref[0][:, cols], -SWIGLU_LIMIT, SWIGLU_LIMIT)
            act_ref[:, cols] = ((up + 1.0) * gate * _sigmoid(SWIGLU_ALPHA * gate)).astype(BF16)
        for c in range(nc_down):
            start_scatter_group(c, nc_down)
            cols = slice(c * MOE_COL_CHUNK, (c + 1) * MOE_COL_CHUNK)
            x2_ref[:, cols] = jnp.dot(act_ref[...], wdb_ref[:, cols], preferred_element_type=F32) \
                + bd_ref[0][:, cols]

        @pl.when(i == 0)
        def _():
            wait_scatter(oth)

        @pl.when(i >= 2)
        def _():
            wait_scatter(cur)
        ybuf_ref[cur] = x2_ref[...].reshape(tm, 1, dm)

    @pl.when(jnp.logical_not(valid))
    def _():
        @pl.when(i >= 2)
        def _():
            wait_scatter(cur)
        wait_gather(cur)
        pltpu.make_async_copy(h_hbm.at[pl.ds(0, tm)], xbuf_ref.at[oth], sem_g.at[oth]).start()

        @pl.when(i == nv_ref[0])
        def _():
            start_scatter(i - 1, oth)

        @pl.when(i > nv_ref[0])
        def _():
            pltpu.make_async_copy(ybuf_ref.at[oth], o_hbm.at[pl.ds((i - 1) * tm, tm)], sem_s.at[oth]).start()

    @pl.when(i == n - 1)
    def _():
        wait_gather(oth)
        wait_scatter(oth)
        start_scatter(i, cur)
        wait_scatter(cur)


def _moe(tile_expert, n_valid, inv, h1_rows, w_gate, b_gate, w_up, b_up, w_down, b_down, *, tm):
    t, _, dm = h1_rows.shape
    _, _, de = w_gate.shape
    n_rows = inv.shape[0]
    per_e = lambda i, te, nv, iv: (te[i], 0, 0)
    return pl.pallas_call(
        functools.partial(_moe_kernel, n_tokens=t),
        grid_spec=pltpu.PrefetchScalarGridSpec(
            num_scalar_prefetch=3, grid=(n_rows // tm,),
            in_specs=[pl.BlockSpec(memory_space=pl.ANY),
                      pl.BlockSpec((1, dm, de), per_e), pl.BlockSpec((1, 1, de), per_e),
                      pl.BlockSpec((1, dm, de), per_e), pl.BlockSpec((1, 1, de), per_e),
                      pl.BlockSpec((1, de, dm), per_e), pl.BlockSpec((1, 1, dm), per_e)],
            out_specs=pl.BlockSpec(memory_space=pl.ANY),
            scratch_shapes=[pltpu.VMEM((2, tm, 1, dm), F32), pltpu.VMEM((2, tm, 1, dm), F32),
                            pltpu.VMEM((tm, dm), F32), pltpu.VMEM((tm, dm), BF16), pltpu.VMEM((tm, de), BF16),
                            pltpu.VMEM((dm, de), BF16),
                            pltpu.VMEM((dm, de), BF16), pltpu.VMEM((de, dm), BF16),
                            pltpu.SemaphoreType.DMA((2,)), pltpu.SemaphoreType.DMA((2,))]),
        out_shape=jax.ShapeDtypeStruct((n_rows, 1, dm), F32),
        compiler_params=pltpu.CompilerParams(
            dimension_semantics=("arbitrary",), vmem_limit_bytes=56 * 1024 * 1024),
        name="moe_experts",
    )(tile_expert, n_valid, inv, h1_rows, w_gate, b_gate, w_up, b_up, w_down, b_down)


def _combine_kernel(*refs, alpha):
    y_refs = refs[:TOP_K]
    h1_ref, rw_ref, l2g_ref, l2b_ref, o_ref, rows2_ref = refs[TOP_K:]
    tm, dm = o_ref.shape
    rw = rw_ref[...]
    ffn = jnp.zeros((tm, dm), F32)
    for kk in range(TOP_K):
        rows2_ref[...] = y_refs[kk][...].reshape(tm, dm)
        ffn = ffn + rw[:, kk:kk + 1] * rows2_ref[...]
    rows2_ref[...] = h1_ref[...].reshape(tm, dm)
    o_ref[...] = _layer_norm(alpha * rows2_ref[...] + ffn, l2g_ref[...], l2b_ref[...])


def _combine(y_slots, h1_rows, rw, l2g, l2b, *, tm, alpha):
    t, _, dm = h1_rows.shape
    rows = lambda i: (i, 0)
    const = lambda i: (0, 0)
    slot_specs = [pl.BlockSpec((tm, 1, dm), functools.partial(lambda i, kk: (kk * (t // tm) + i, 0, 0), kk=kk))
                  for kk in range(TOP_K)]
    return pl.pallas_call(
        functools.partial(_combine_kernel, alpha=alpha),
        grid=(t // tm,),
        in_specs=slot_specs + [pl.BlockSpec((tm, 1, dm), lambda i: (i, 0, 0)),
                               pl.BlockSpec((tm, LANES), rows),
                               pl.BlockSpec((1, dm), const), pl.BlockSpec((1, dm), const)],
        out_specs=pl.BlockSpec((tm, dm), rows),
        out_shape=jax.ShapeDtypeStruct((t, dm), F32),
        scratch_shapes=[pltpu.VMEM((tm, dm), F32)],
        compiler_params=pltpu.CompilerParams(
            dimension_semantics=("arbitrary",), vmem_limit_bytes=48 * 1024 * 1024),
        name="moe_combine_ln2",
    )(*([y_slots] * TOP_K), h1_rows, rw, l2g, l2b)


INVERT_CHUNK = 8192


def _invert_kernel(rows_ref, inv_ref, *, n_tokens):
    i = pl.program_id(0)
    ch = rows_ref.shape[0]
    first = i * ch

    @pl.when(first < TOP_K * n_tokens)
    def _():
        t0 = first // TOP_K

        def body(m, c):
            base = t0 + 2 * m
            for q in range(2 * TOP_K):
                inv_ref[rows_ref[2 * TOP_K * m + q]] = base + ((q % TOP_K) * n_tokens + q // TOP_K)
            return c
        lax.fori_loop(0, ch // (2 * TOP_K), body, 0, unroll=4)

    @pl.when(first >= TOP_K * n_tokens)
    def _():
        def body(p, c):
            inv_ref[rows_ref[p]] = first + p
            return c
        lax.fori_loop(0, ch, body, 0, unroll=8)


def _invert(rows, n_tokens):
    n = rows.shape[0]
    ch = math.gcd(math.gcd(n, TOP_K * n_tokens), INVERT_CHUNK)
    assert ch % TOP_K == 0
    return pl.pallas_call(
        functools.partial(_invert_kernel, n_tokens=n_tokens),
        grid=(n // ch,),
        in_specs=[pl.BlockSpec((ch,), lambda i: (i,), memory_space=pltpu.SMEM)],
        out_specs=pl.BlockSpec(memory_space=pltpu.SMEM),
        out_shape=jax.ShapeDtypeStruct((n,), jnp.int32),
        compiler_params=pltpu.CompilerParams(dimension_semantics=("arbitrary",)),
        name="moe_row_slots",
    )(rows)


def _routing_tables(ridx, counts, *, n_experts, tm):
    t = ridx.shape[0]
    n_real = TOP_K * t
    n_fill = n_experts * tm
    e_idx = ridx[:, :TOP_K]
    rank = ridx[:, TOP_K:2 * TOP_K]
    cnt = counts[0, :n_experts].astype(jnp.int32)
    padded = (cnt + tm - 1) // tm * tm
    ends = jnp.cumsum(padded)
    offs = ends - padded
    experts = jnp.arange(n_experts, dtype=jnp.int32)
    base = jnp.sum(jnp.where(e_idx[..., None] == experts, offs, 0), axis=-1)
    token_rows = (base + rank).reshape(-1)
    fill_cnt = padded - cnt
    fill_end = jnp.cumsum(fill_cnt)
    shift = jnp.concatenate([offs + cnt - (fill_end - fill_cnt), jnp.full((1,), n_real, jnp.int32)])
    j = jnp.arange(n_fill, dtype=jnp.int32)
    fill_rows = j + shift[0] + jnp.sum(
        jnp.where(j[:, None] >= fill_end[None, :], (shift[1:] - shift[:-1])[None, :], 0), axis=1)
    rows = jnp.concatenate([token_rows, fill_rows])
    inv = _invert(rows, t)
    n_tiles = (n_real + n_fill) // tm
    tile_start = jnp.arange(n_tiles, dtype=jnp.int32) * tm
    tile_expert = jnp.minimum(jnp.sum(ends[None, :] <= tile_start[:, None], axis=1), n_experts - 1)
    n_valid = (ends[-1] // tm).reshape(1)
    return inv.astype(jnp.int32), tile_expert.astype(jnp.int32), n_valid.astype(jnp.int32)


def _pick_tile(n, want):
    t = min(n, want)
    assert n % t == 0, (n, t)
    return t


def kernel(x, meta, ln_in_g, ln_in_b, w_in, b_f, w_up_a, w_up_b, w_o, ssm_a_re, ssm_a_im, ssm_log_dt,
           ssm_b_re, ssm_b_im, ssm_c_re, ssm_c_im, ssm_d, w_glu, b_glu, ln1_g, ln1_b, w_router, b_router,
           w_gate, b_gate, w_up, b_up, w_down, b_down, ln2_g, ln2_b):
    bsz, seq, dm = x.shape
    depth = w_in.shape[0]
    assert depth == 1 and meta.shape[0] == N_META
    alpha = (2.0 * depth) ** 0.25
    n_groups = ssm_a_re.shape[1]
    n_experts = w_router.shape[-1]
    aw = ATT_WIDTH
    t = bsz * seq
    x2 = x.reshape(t, dm)
    row = lambda a: a.reshape(1, -1)

    f_off = 3 * aw
    u_off = f_off + ATT_HEADS
    w0 = w_in[0]
    w_main = jnp.concatenate([w0[:, :f_off], w0[:, u_off:]], axis=1).astype(BF16)
    w_f = jnp.pad(w0[:, f_off:u_off], ((0, 0), (0, LANES - ATT_HEADS))).astype(BF16)
    b_f_pad = jnp.pad(b_f[0], (0, LANES - ATT_HEADS)).reshape(1, LANES)
    ln_g, ln_b = row(ln_in_g), row(ln_in_b)

    tm = _pick_tile(seq, 512)
    q_t, k, v_t, u, ga, gb, _ = _inproj(x2, ln_g, ln_b, w_main, w_f, b_f_pad, tm=tm, tiles_per_seq=seq // tm,
                                        transpose_qv=True)
    _, k_m, v_m, u_m, _, _, fcum_m = _inproj(meta, ln_g, ln_b, w_main, w_f, b_f_pad, tm=N_META, tiles_per_seq=1,
                                             transpose_qv=False)

    pad_m = META_KEYS_PAD - N_META
    fm = fcum_m[:, :ATT_HEADS]
    bias_m = jnp.pad(-LOG2E * (fm - fm[N_META - 1:N_META, :]), ((0, pad_m), (0, 0)), constant_values=-BIG)
    k_meta = jnp.pad(k_m, ((0, pad_m), (0, 0))).reshape(META_KEYS_PAD, ATT_HEADS, LANES)
    lane_m = jnp.arange(LANES)
    for n_piece, piece in enumerate(_split3(bias_m)):
        k_meta = jnp.where(lane_m == HEAD_DIM + n_piece, piece[..., None], k_meta)
    k_meta = k_meta.reshape(META_KEYS_PAD, ATT_HEADS * LANES)
    v_meta_t = jnp.pad(v_m, ((0, pad_m), (0, 0))).T
    tq = _pick_tile(seq, 512)
    y_a = _attn(q_t, k, v_t, k_meta, v_meta_t, bsz=bsz, seq=seq, tq=tq)

    chunk = SSM_CHUNK
    n_chunks = seq // chunk
    n_steps = max(1, (n_chunks - 1).bit_length())
    kt, wst, wout, wmeta, apow, d_tiled = _ssm_tables(
        ssm_a_re[0], ssm_a_im[0], ssm_log_dt[0], ssm_b_re[0], ssm_b_im[0], ssm_c_re[0], ssm_c_im[0], ssm_d[0],
        chunk=chunk, n_steps=n_steps)
    u_g = jnp.transpose(u.reshape(bsz * n_chunks, chunk, n_groups, SSM_GROUP), (2, 0, 1, 3))
    u_g = u_g.reshape(n_groups, bsz * n_chunks, chunk * SSM_GROUP)
    um_g = jnp.transpose(u_m.reshape(N_META, n_groups, SSM_GROUP), (1, 0, 2)).reshape(n_groups, 1, N_META * SSM_GROUP)
    um_g = jnp.pad(um_g, ((0, 0), (0, 15), (0, 0)))
    y_g = _ssm(u_g, kt, wst, wout, um_g, wmeta, apow, d_tiled, chunks_per_seq=n_chunks)
    y_b = jnp.transpose(y_g.reshape(n_groups, bsz * n_chunks, chunk, SSM_GROUP), (1, 2, 0, 3)).reshape(t, -1)

    wr = jnp.pad(w_router[0], ((0, 0), (0, LANES - n_experts)))
    wr_hi = wr.astype(BF16)
    wr_lo = (wr - wr_hi.astype(F32)).astype(BF16)
    b_r = jnp.pad(b_router[0], (0, LANES - n_experts)).reshape(1, LANES)
    tm2 = _pick_tile(t, 512)
    h1_rows, ridx, rw, counts = _postmix(
        x2, y_a, y_b, ga, gb, ln_g, ln_b, w_glu[0].astype(BF16), row(b_glu[0]),
        w_up_a[0].astype(BF16), w_up_b[0].astype(BF16), w_o[0].astype(BF16), row(ln1_g[0]), row(ln1_b[0]),
        wr_hi, wr_lo, b_r, tm=tm2, alpha=alpha, n_experts=n_experts)

    tm3 = MOE_ROW_TILE
    inv, tile_expert, n_valid = _routing_tables(ridx, counts, n_experts=n_experts, tm=tm3)
    y_slots = _moe(tile_expert, n_valid, inv, h1_rows, w_gate[0], b_gate[0][:, None, :], w_up[0],
                   b_up[0][:, None, :], w_down[0], b_down[0][:, None, :], tm=tm3)
    tm4 = _pick_tile(t, 256)
    out = _combine(y_slots, h1_rows, rw, row(ln2_g[0]), row(ln2_b[0]), tm=tm4, alpha=alpha)
    return out.reshape(bsz, seq, dm)
```

```python
import functools
import math

import jax
import jax.numpy as jnp
from jax import lax
from jax.experimental import pallas as pl
from jax.experimental.pallas import tpu as pltpu

F32 = jnp.float32
BF16 = jnp.bfloat16

N_META = 16
ATT_HEADS = 8
HEAD_DIM = 64
ATT_WIDTH = ATT_HEADS * HEAD_DIM
SSM_GROUP = 16
SSM_STATE = 64
TOP_K = 4
SWIGLU_LIMIT = 7.0
SWIGLU_ALPHA = 1.702
LN_EPS = 1e-5

LANES = 128
SSM_CHUNK = 64
META_KEYS_PAD = LANES
BIG = 1e30
LOG2E = 1.4426950408889634


def _layer_norm(x, g, b):
    mu = jnp.mean(x, axis=-1, keepdims=True)
    xc = x - mu
    var = jnp.mean(xc * xc, axis=-1, keepdims=True)
    return xc * lax.rsqrt(var + LN_EPS) * g + b


def _log_sigmoid(z):
    return jnp.minimum(z, 0.0) - jnp.log1p(jnp.exp(-jnp.abs(z)))


def _sigmoid(z):
    return 1.0 / (1.0 + jnp.exp(-z))


def _split3(x):
    hi = x.astype(BF16)
    r1 = x - hi.astype(F32)
    mid = r1.astype(BF16)
    lo = (r1 - mid.astype(F32)).astype(BF16)
    return hi, mid, lo


def _inproj_kernel(x_ref, g_ref, b_ref, w_ref, wf_ref, bf_ref,
                   q_ref, k_ref, v_ref, u_ref, ga_ref, gb_ref, fc_ref, carry_ref,
                   *, tiles_per_seq, transpose_qv):
    i = pl.program_id(0)
    tm = x_ref.shape[0]
    h = _layer_norm(x_ref[...], g_ref[...], b_ref[...]).astype(BF16)

    def proj(lo, hi):
        return jnp.dot(h, w_ref[:, lo:hi], preferred_element_type=F32)

    aw = ATT_WIDTH
    u_ref[...] = proj(3 * aw, 4 * aw).astype(BF16)
    dm = ga_ref.shape[1]
    ga_ref[...] = _sigmoid(proj(4 * aw, 4 * aw + dm)).astype(BF16)
    gb_ref[...] = _sigmoid(proj(4 * aw + dm, 4 * aw + 2 * dm)).astype(BF16)

    zf = jnp.dot(h, wf_ref[...], preferred_element_type=F32) + bf_ref[...]
    lf = _log_sigmoid(zf)

    @pl.when(i % tiles_per_seq == 0)
    def _():
        carry_ref[...] = jnp.zeros_like(carry_ref)

    row = lax.broadcasted_iota(jnp.int32, (tm, tm), 0)
    col = lax.broadcasted_iota(jnp.int32, (tm, tm), 1)
    tri = (col <= row).astype(BF16)
    hi, mid, lo = _split3(lf)
    cs = (jnp.dot(tri, hi, preferred_element_type=F32)
          + jnp.dot(tri, mid, preferred_element_type=F32)
          + jnp.dot(tri, lo, preferred_element_type=F32)) + carry_ref[...]
    fc_ref[...] = cs
    carry_ref[...] = cs[tm - 1:tm, :]

    lane = lax.broadcasted_iota(jnp.int32, (1, LANES), 1)
    own = lane < HEAD_DIM
    bias_lanes = jnp.logical_and(lane >= HEAD_DIM, lane < HEAD_DIM + 3)
    pieces = [p.astype(F32) for p in _split3(-LOG2E * cs)]

    def slabs(x, extra, transposed=False):
        out = []
        for hp in range(ATT_HEADS // 2):
            x2 = x[:, LANES * hp:LANES * (hp + 1)]
            out.append(jnp.where(own, x2, extra(2 * hp)))
            out.append(jnp.where(own, pltpu.roll(x2, HEAD_DIM, axis=1), extra(2 * hp + 1)))
        if transposed:
            out = [o.T for o in out]
            return jnp.concatenate(out, axis=0).astype(BF16)
        return jnp.concatenate(out, axis=1).astype(BF16)

    def bias_extra(h):
        e = jnp.zeros((tm, LANES), F32)
        for j, piece in enumerate(pieces):
            e = jnp.where(lane == HEAD_DIM + j, pltpu.roll(piece, (HEAD_DIM + j - h) % LANES, axis=1), e)
        return e

    q_ref[...] = slabs(proj(0, aw) * (LOG2E / math.sqrt(HEAD_DIM)),
                       lambda h: jnp.broadcast_to(jnp.where(bias_lanes, 1.0, 0.0), (tm, LANES)), transpose_qv)
    k_ref[...] = slabs(proj(aw, 2 * aw), bias_extra)
    v_ref[...] = slabs(proj(2 * aw, 3 * aw), lambda h: jnp.ones((tm, LANES), F32), transpose_qv)


def _inproj(x2, ln_g, ln_b, w_main, w_f, b_f, *, tm, tiles_per_seq, transpose_qv):
    t, dm = x2.shape
    aw = ATT_WIDTH
    n_main = w_main.shape[1]
    const = lambda i: (0, 0)
    rows = lambda i: (i, 0)
    slab = ATT_HEADS * LANES
    row_slab = (jax.ShapeDtypeStruct((t, slab), BF16), pl.BlockSpec((tm, slab), rows))
    col_slab = (jax.ShapeDtypeStruct((slab, t), BF16), pl.BlockSpec((slab, tm), lambda i: (0, i)))
    qv = col_slab if transpose_qv else row_slab
    out_shape = [qv[0], row_slab[0], qv[0], jax.ShapeDtypeStruct((t, aw), BF16)] + \
                [jax.ShapeDtypeStruct((t, dm), BF16)] * 2 + [jax.ShapeDtypeStruct((t, LANES), F32)]
    out_specs = [qv[1], row_slab[1], qv[1], pl.BlockSpec((tm, aw), rows)] + \
                [pl.BlockSpec((tm, dm), rows)] * 2 + [pl.BlockSpec((tm, LANES), rows)]
    return pl.pallas_call(
        functools.partial(_inproj_kernel, tiles_per_seq=tiles_per_seq, transpose_qv=transpose_qv),
        grid=(t // tm,),
        in_specs=[pl.BlockSpec((tm, dm), rows),
                  pl.BlockSpec((1, dm), const), pl.BlockSpec((1, dm), const),
                  pl.BlockSpec((dm, n_main), const),
                  pl.BlockSpec((dm, LANES), const), pl.BlockSpec((1, LANES), const)],
        out_specs=out_specs,
        out_shape=out_shape,
        scratch_shapes=[pltpu.VMEM((1, LANES), F32)],
        compiler_params=pltpu.CompilerParams(
            dimension_semantics=("arbitrary",), vmem_limit_bytes=56 * 1024 * 1024),
        name="inproj",
    )(x2, ln_g, ln_b, w_main, w_f, b_f)


def _attn_kernel(qi_tab, ki_tab, qt_ref, k_ref, vt_ref, km_ref, vmt_ref, o_ref, m_ref, acc_ref, st_ref):
    p = pl.program_id(1)
    qi = qi_tab[p]
    ki = ki_tab[p]
    tq = qt_ref.shape[1]

    def process(k_blk, vt_blk, mask):
        n = k_blk.shape[0]

        def logits(h):
            sl = slice(LANES * h, LANES * (h + 1))
            st = jnp.dot(k_blk[:, sl], qt_ref[sl, :], preferred_element_type=F32)
            if mask is not None:
                st = jnp.where(mask, st, -jnp.inf)
            st_ref[h % 2, 0:n] = st
            m_old = m_ref[h]
            m_new = jnp.maximum(m_old, jnp.max(st_ref[h % 2, 0:n], axis=0, keepdims=True))
            m_ref[h] = m_new
            return m_old, m_new

        def weigh(h, m_old, m_new):
            sl = slice(LANES * h, LANES * (h + 1))
            pt = jnp.exp2(st_ref[h % 2, 0:n] - m_new[0:1, :])
            alpha = jnp.exp2(m_old - m_new)
            acc_ref[h] = alpha[0:1, :] * acc_ref[h] + jnp.dot(
                vt_blk[sl, :], pt.astype(BF16), preferred_element_type=F32)

        ms = logits(0)
        for h in range(ATT_HEADS):
            nxt = logits(h + 1) if h + 1 < ATT_HEADS else None
            weigh(h, *ms)
            ms = nxt

    @pl.when(ki == 0)
    def _():
        m_ref[...] = jnp.full_like(m_ref, -jnp.inf)
        acc_ref[...] = jnp.zeros_like(acc_ref)
        process(km_ref[...], vmt_ref[...], None)

    @pl.when(ki < qi)
    def _():
        process(k_ref[...], vt_ref[...], None)

    @pl.when(ki == qi)
    def _():
        tk = k_ref.shape[0]
        key = lax.broadcasted_iota(jnp.int32, (tk, tq), 0)
        qry = lax.broadcasted_iota(jnp.int32, (tk, tq), 1)
        process(k_ref[...], vt_ref[...], key <= qry)
        for hp in range(ATT_HEADS // 2):
            a0, a1 = acc_ref[2 * hp], acc_ref[2 * hp + 1]
            pair = jnp.concatenate([a0[:HEAD_DIM] / a0[HEAD_DIM:], a1[:HEAD_DIM] / a1[HEAD_DIM:]], axis=0)
            o_ref[:, LANES * hp:LANES * (hp + 1)] = pair.T.astype(o_ref.dtype)


def _attn(q_t, k, v_t, k_meta, v_meta_t, *, bsz, seq, tq):
    nq = seq // tq
    pairs = [(a, b) for a in range(nq) for b in range(a + 1)]
    qi_tab = jnp.asarray([a for a, _ in pairs], jnp.int32)
    ki_tab = jnp.asarray([b for _, b in pairs], jnp.int32)
    w = k.shape[1]
    qcol = lambda b, p, qt, kt: (0, b * nq + qt[p])
    kcol = lambda b, p, qt, kt: (0, b * nq + kt[p])
    krow = lambda b, p, qt, kt: (b * nq + kt[p], 0)
    const = lambda b, p, qt, kt: (0, 0)
    grid_spec = pltpu.PrefetchScalarGridSpec(
        num_scalar_prefetch=2,
        grid=(bsz, len(pairs)),
        in_specs=[pl.BlockSpec((w, tq), qcol),
                  pl.BlockSpec((tq, w), krow),
                  pl.BlockSpec((w, tq), kcol),
                  pl.BlockSpec((META_KEYS_PAD, w), const),
                  pl.BlockSpec((w, META_KEYS_PAD), const)],
        out_specs=pl.BlockSpec((tq, ATT_WIDTH), lambda b, p, qt, kt: (b * nq + qt[p], 0)),
        scratch_shapes=[pltpu.VMEM((ATT_HEADS, 8, tq), F32),
                        pltpu.VMEM((ATT_HEADS, LANES, tq), F32),
                        pltpu.VMEM((2, tq, tq), F32)],
    )
    return pl.pallas_call(
        _attn_kernel,
        grid_spec=grid_spec,
        out_shape=jax.ShapeDtypeStruct((bsz * seq, ATT_WIDTH), BF16),
        compiler_params=pltpu.CompilerParams(
            dimension_semantics=("arbitrary", "arbitrary"), vmem_limit_bytes=48 * 1024 * 1024),
        name="fox_attention",
    )(qi_tab, ki_tab, q_t, k, v_t, k_meta, v_meta_t)


def _ssm_kernel(u_ref, kt_ref, wst_ref, wout_ref, um_ref, wm_ref, apow_ref, d_ref, y_ref, toep_ref,
                *, chunks_per_seq, n_steps):
    u = u_ref[0]
    rows = u.shape[0]
    p2 = 2 * SSM_STATE
    kt = kt_ref[0]
    n_c, tc = kt.shape
    per_vreg = LANES // n_c
    ext = jnp.concatenate([jnp.zeros_like(kt), kt], axis=1)
    for sub in range(per_vreg):
        rot = ext if sub == 0 else pltpu.roll(ext, n_c * sub, axis=1)
        for whole in range(tc // LANES):
            tp = per_vreg * whole + sub
            toep_ref[tp * n_c:(tp + 1) * n_c, :] = rot[:, tc - LANES * whole:2 * tc - LANES * whole].astype(BF16)
    y = jnp.dot(u, toep_ref[...], preferred_element_type=F32)
    s = jnp.dot(u, wst_ref[0], preferred_element_type=F32)
    x0 = jnp.dot(um_ref[0], wm_ref[0], preferred_element_type=F32)[0:1, :]

    lane = lax.broadcasted_iota(jnp.int32, (1, p2), 1)
    re_half = lane < SSM_STATE
    j = lax.broadcasted_iota(jnp.int32, (rows, 1), 0) % chunks_per_seq

    def cmul(step, z):
        ar = apow_ref[0, step:step + 1, :]
        ai = apow_ref[0, n_steps + step:n_steps + step + 1, :]
        return ar * z + ai * pltpu.roll(z, SSM_STATE, axis=1)

    s = s + jnp.where(j == 0, cmul(0, jnp.broadcast_to(x0, s.shape)), 0.0)
    for step in range(n_steps):
        sh = 1 << step
        prev = pltpu.roll(s, sh, axis=0)
        s = s + jnp.where(j >= sh, cmul(step, prev), 0.0)
    x_in = jnp.where(j == 0, x0, pltpu.roll(s, 1, axis=0))
    del re_half
    y = y + jnp.dot(x_in.astype(BF16), wout_ref[0], preferred_element_type=F32)
    y = y + d_ref[0] * u.astype(F32)
    y_ref[0] = jax.nn.gelu(y).astype(y_ref.dtype)


def _ssm(u_g, kt, wst, wout, u_meta, wmeta, apow, d_tiled, *, chunks_per_seq):
    g, rows, tc = u_g.shape
    n_steps = apow.shape[1] // 2
    grp = lambda i: (i, 0, 0)
    return pl.pallas_call(
        functools.partial(_ssm_kernel, chunks_per_seq=chunks_per_seq, n_steps=n_steps),
        grid=(g,),
        in_specs=[pl.BlockSpec((1, rows, tc), grp),
                  pl.BlockSpec((1,) + kt.shape[1:], grp),
                  pl.BlockSpec((1, tc, 2 * SSM_STATE), grp),
                  pl.BlockSpec((1, 2 * SSM_STATE, tc), grp),
                  pl.BlockSpec((1,) + u_meta.shape[1:], grp),
                  pl.BlockSpec((1,) + wmeta.shape[1:], grp),
                  pl.BlockSpec((1,) + apow.shape[1:], grp),
                  pl.BlockSpec((1, 1, tc), grp)],
        out_specs=pl.BlockSpec((1, rows, tc), grp),
        out_shape=jax.ShapeDtypeStruct((g, rows, tc), BF16),
        scratch_shapes=[pltpu.VMEM((tc, tc), BF16)],
        compiler_params=pltpu.CompilerParams(
            dimension_semantics=("arbitrary",), vmem_limit_bytes=48 * 1024 * 1024),
        name="s5_ssm",
    )(u_g, kt, wst, wout, u_meta, wmeta, apow, d_tiled)


def _ssm_tables(a_re, a_im, log_dt, b_re, b_im, c_re, c_im, d_skip, *, chunk, n_steps):
    g, p = a_re.shape
    c = b_re.shape[-1]
    dt = jnp.exp(log_dt)[:, None]
    mag = jnp.exp(a_re * dt)
    ang = a_im * dt
    lb_re, lb_im = mag * jnp.cos(ang), mag * jnp.sin(ang)
    den = a_re * a_re + a_im * a_im
    z_re, z_im = lb_re - 1.0, lb_im
    coef_re = (z_re * a_re + z_im * a_im) / den
    coef_im = (z_im * a_re - z_re * a_im) / den
    bb_re = coef_re[..., None] * b_re - coef_im[..., None] * b_im
    bb_im = coef_re[..., None] * b_im + coef_im[..., None] * b_re

    def step(carry, _):
        cr, ci = carry
        return (cr * lb_re - ci * lb_im, cr * lb_im + ci * lb_re), (cr, ci)
    _, (pw_re, pw_im) = lax.scan(step, (jnp.ones_like(lb_re), jnp.zeros_like(lb_re)), None,
                                 length=chunk + 1)

    e_re = c_re[None] * pw_re[:, :, None, :] - c_im[None] * pw_im[:, :, None, :]
    e_im = c_re[None] * pw_im[:, :, None, :] + c_im[None] * pw_re[:, :, None, :]
    kern = (jnp.einsum('tgcp,gpd->tgcd', e_re[:chunk], bb_re)
            - jnp.einsum('tgcp,gpd->tgcd', e_im[:chunk], bb_im))
    kt = jnp.transpose(kern, (1, 3, 0, 2)).reshape(g, c, chunk * c)

    def in_to_state(n):
        wr = pw_re[n - 1::-1][:n, :, :, None] * bb_re[None] - pw_im[n - 1::-1][:n, :, :, None] * bb_im[None]
        wi = pw_re[n - 1::-1][:n, :, :, None] * bb_im[None] + pw_im[n - 1::-1][:n, :, :, None] * bb_re[None]
        w = jnp.concatenate([wr, wi], axis=2)
        return jnp.transpose(w, (1, 0, 3, 2)).reshape(g, n * c, 2 * p)
    wst = in_to_state(chunk)
    wmeta = in_to_state(N_META)

    wout = jnp.concatenate([e_re[1:chunk + 1], -e_im[1:chunk + 1]], axis=3)
    wout = jnp.transpose(wout, (1, 3, 0, 2)).reshape(g, 2 * p, chunk * c)

    ar, ai = pw_re[chunk], pw_im[chunk]
    rows_r, rows_i = [], []
    for _ in range(n_steps):
        rows_r.append(jnp.concatenate([ar, ar], axis=-1))
        rows_i.append(jnp.concatenate([-ai, ai], axis=-1))
        ar, ai = ar * ar - ai * ai, 2.0 * ar * ai
    apow = jnp.stack(rows_r + rows_i, axis=1)
    d_tiled = jnp.tile(d_skip, (1, chunk))[:, None, :]
    return kt, wst.astype(BF16), wout.astype(BF16), wmeta.astype(BF16), apow, d_tiled


def _postmix_kernel(x_ref, ya_ref, yb_ref, ga_ref, gb_ref, lng_ref, lnb_ref,
                    wglu_ref, bglu_ref, wa_ref, wb_ref, wo_ref, l1g_ref, l1b_ref,
                    wrh_ref, wrl_ref, br_ref, h1_ref, ridx_ref, rw_ref, cnt_ref, carry_ref,
                    *, alpha, n_experts):
    i = pl.program_id(0)
    tm = x_ref.shape[0]
    halves = [slice(0, tm // 2), slice(tm // 2, tm)]
    dot = functools.partial(jnp.dot, preferred_element_type=F32)
    glu = [yb_ref[r, :].astype(F32) * _sigmoid(dot(yb_ref[r, :], wglu_ref[...]) + bglu_ref[...]) for r in halves]
    att = [ga_ref[r, :].astype(F32) * dot(ya_ref[r, :], wa_ref[...]) for r in halves]
    merged = [att[j] + gb_ref[r, :].astype(F32) * dot(glu[j].astype(BF16), wb_ref[...])
              for j, r in enumerate(halves)]
    mix = [dot(m.astype(BF16), wo_ref[...]) for m in merged]
    h1s = [_layer_norm(alpha * _layer_norm(x_ref[r, :], lng_ref[...], lnb_ref[...]) + mix[j],
                       l1g_ref[...], l1b_ref[...]) for j, r in enumerate(halves)]
    parts = []
    for j, r in enumerate(halves):
        h1_ref[r] = h1s[j].reshape((tm // 2,) + h1_ref.shape[1:])
        hb = h1s[j].astype(BF16)
        hl = (h1s[j] - hb.astype(F32)).astype(BF16)
        parts.append(dot(hb, wrh_ref[...]) + dot(hb, wrl_ref[...]) + dot(hl, wrh_ref[...]))
    logits = jnp.concatenate(parts, axis=0) + br_ref[...]
    lane = lax.broadcasted_iota(jnp.int32, logits.shape, 1)
    logits = jnp.where(lane < n_experts, logits, -jnp.inf)
    denom = jnp.zeros((tm, 1), F32)
    top = None
    hits, idxs, ws = [], [], []
    for _ in range(TOP_K):
        mx = jnp.max(logits, axis=-1, keepdims=True)
        idx = jnp.min(jnp.where(logits == mx, lane, LANES), axis=-1, keepdims=True)
        hit = lane == idx
        if top is None:
            top = mx
        w = jnp.exp(mx - top)
        denom = denom + w
        hits.append(hit)
        idxs.append(idx)
        ws.append(w)
        logits = jnp.where(hit, -jnp.inf, logits)

    @pl.when(i == 0)
    def _():
        carry_ref[...] = jnp.zeros_like(carry_ref)

    sel = jnp.zeros(logits.shape, F32)
    for hit in hits:
        sel = sel + jnp.where(hit, 1.0, 0.0)
    row = lax.broadcasted_iota(jnp.int32, (tm, tm), 0)
    col = lax.broadcasted_iota(jnp.int32, (tm, tm), 1)
    before = (col < row).astype(BF16)
    seen = jnp.dot(before, sel.astype(BF16), preferred_element_type=F32) + carry_ref[...]
    ridx = jnp.zeros(logits.shape, jnp.int32)
    rw = jnp.zeros(logits.shape, F32)
    for kk in range(TOP_K):
        rank = jnp.sum(jnp.where(hits[kk], seen, 0.0), axis=-1, keepdims=True).astype(jnp.int32)
        ridx = jnp.where(lane == kk, idxs[kk], ridx)
        ridx = jnp.where(lane == TOP_K + kk, rank, ridx)
        rw = jnp.where(lane == kk, ws[kk] / denom, rw)
    ridx_ref[...] = ridx
    rw_ref[...] = rw
    total = carry_ref[...] + jnp.sum(sel, axis=0, keepdims=True)
    carry_ref[...] = total
    cnt_ref[...] = jnp.broadcast_to(total, cnt_ref.shape)


def _postmix(x2, ya, yb, ga, gb, ln_g, ln_b, w_glu, b_glu, w_a, w_b, w_o, l1g, l1b,
             wr_hi, wr_lo, b_r, *, tm, alpha, n_experts):
    t, dm = x2.shape
    sw = ya.shape[1]
    rows = lambda i: (i, 0)
    const = lambda i: (0, 0)
    full = lambda a: pl.BlockSpec(a.shape, const)
    return pl.pallas_call(
        functools.partial(_postmix_kernel, alpha=alpha, n_experts=n_experts),
        grid=(t // tm,),
        in_specs=[pl.BlockSpec((tm, dm), rows), pl.BlockSpec((tm, sw), rows), pl.BlockSpec((tm, sw), rows),
                  pl.BlockSpec((tm, dm), rows), pl.BlockSpec((tm, dm), rows),
                  full(ln_g), full(ln_b), full(w_glu), full(b_glu), full(w_a), full(w_b), full(w_o),
                  full(l1g), full(l1b), full(wr_hi), full(wr_lo), full(b_r)],
        out_specs=[pl.BlockSpec((tm, 1, dm), lambda i: (i, 0, 0)), pl.BlockSpec((tm, LANES), rows),
                   pl.BlockSpec((tm, LANES), rows), pl.BlockSpec((8, LANES), const)],
        out_shape=[jax.ShapeDtypeStruct((t, 1, dm), F32), jax.ShapeDtypeStruct((t, LANES), jnp.int32),
                   jax.ShapeDtypeStruct((t, LANES), F32), jax.ShapeDtypeStruct((8, LANES), F32)],
        scratch_shapes=[pltpu.VMEM((1, LANES), F32)],
        compiler_params=pltpu.CompilerParams(
            dimension_semantics=("arbitrary",), vmem_limit_bytes=48 * 1024 * 1024),
        name="postmix_router",
    )(x2, ya, yb, ga, gb, ln_g, ln_b, w_glu, b_glu, w_a, w_b, w_o, l1g, l1b, wr_hi, wr_lo, b_r)


MOE_ROW_TILE = 512
MOE_COL_CHUNK = 256


def _moe_kernel(te_ref, nv_ref, inv_ref, h_hbm, wg_ref, bg_ref, wu_ref, bu_ref, wd_ref, bd_ref, o_hbm,
                xbuf_ref, ybuf_ref, x2_ref, xb_ref, act_ref, wgb_ref, wub_ref, wdb_ref, sem_g, sem_s,
                *, n_tokens):
    i = pl.program_id(0)
    n = pl.num_programs(0)
    _, tm, _, dm = xbuf_ref.shape
    cur = i % 2
    oth = (i + 1) % 2

    def start_gather(tile, slot):
        base = tile * tm

        def body(r, c):
            s = inv_ref[base + r]
            tok = s & (n_tokens - 1) if n_tokens & (n_tokens - 1) == 0 else s % n_tokens
            pltpu.make_async_copy(h_hbm.at[tok], xbuf_ref.at[slot, r], sem_g.at[slot]).start()
            return c
        lax.fori_loop(0, tm, body, 0, unroll=8)

    def wait_gather(slot):
        pltpu.make_async_copy(h_hbm.at[pl.ds(0, tm)], xbuf_ref.at[slot], sem_g.at[slot]).wait()

    def start_scatter(tile, slot):
        base = tile * tm

        def body(r, c):
            pltpu.make_async_copy(ybuf_ref.at[slot, r], o_hbm.at[inv_ref[base + r]], sem_s.at[slot]).start()
            return c
        lax.fori_loop(0, tm, body, 0, unroll=8)

    def wait_scatter(slot):
        pltpu.make_async_copy(ybuf_ref.at[slot], o_hbm.at[pl.ds(0, tm)], sem_s.at[slot]).wait()

    @pl.when(i == 0)
    def _():
        ybuf_ref[...] = jnp.zeros_like(ybuf_ref)
        start_gather(0, 0)

    valid = i < nv_ref[0]
    new_expert = jnp.logical_or(i == 0, te_ref[i] != te_ref[jnp.maximum(i - 1, 0)])

    @pl.when(jnp.logical_and(valid, new_expert))
    def _():
        wgb_ref[...] = wg_ref[0].astype(BF16)
        wub_ref[...] = wu_ref[0].astype(BF16)
        wdb_ref[...] = wd_ref[0].astype(BF16)

    def start_gather_group(g, n_groups):
        base = jnp.minimum(i + 1, n - 1) * tm
        per = tm // n_groups
        for r in range(g * per, (g + 1) * per):
            s = inv_ref[base + r]
            tok = s & (n_tokens - 1) if n_tokens & (n_tokens - 1) == 0 else s % n_tokens
            pltpu.make_async_copy(h_hbm.at[tok], xbuf_ref.at[oth, r], sem_g.at[oth]).start()

    def start_scatter_group(g, n_groups):
        base = jnp.maximum(i - 1, 0) * tm
        per = tm // n_groups
        for r in range(g * per, (g + 1) * per):
            pltpu.make_async_copy(ybuf_ref.at[oth, r], o_hbm.at[inv_ref[base + r]], sem_s.at[oth]).start()

    @pl.when(valid)
    def _():
        de = wgb_ref.shape[1]
        nc_up, nc_down = de // MOE_COL_CHUNK, dm // MOE_COL_CHUNK
        wait_gather(cur)
        x2_ref[...] = xbuf_ref[cur].reshape(x2_ref.shape)
        xb_ref[...] = x2_ref[...].astype(BF16)
        for c in range(nc_up):
            start_gather_group(c, nc_up)
            cols = slice(c * MOE_COL_CHUNK, (c + 1) * MOE_COL_CHUNK)
            gate = jnp.minimum(jnp.dot(xb_ref[...], wgb_ref[:, cols], preferred_element_type=F32)
                               + bg_ref[0][:, cols], SWIGLU_LIMIT)
            up = jnp.clip(jnp.dot(xb_ref[...], wub_ref[:, cols], preferred_element_type=F32)
                          + bu_ref[0][:, cols], -SWIGLU_LIMIT, SWIGLU_LIMIT)
            act_ref[:, cols] = ((up + 1.0) * gate * _sigmoid(SWIGLU_ALPHA * gate)).astype(BF16)
        for c in range(nc_down):
            start_scatter_group(c, nc_down)
            cols = slice(c * MOE_COL_CHUNK, (c + 1) * MOE_COL_CHUNK)
            x2_ref[:, cols] = jnp.dot(act_ref[...], wdb_ref[:, cols], preferred_element_type=F32) \
                + bd_ref[0][:, cols]

        @pl.when(i == 0)
        def _():
            wait_scatter(oth)

        @pl.when(i >= 2)
        def _():
            wait_scatter(cur)
        ybuf_ref[cur] = x2_ref[...].reshape(tm, 1, dm)

    @pl.when(jnp.logical_not(valid))
    def _():
        @pl.when(i >= 2)
        def _():
            wait_scatter(cur)
        wait_gather(cur)
        pltpu.make_async_copy(h_hbm.at[pl.ds(0, tm)], xbuf_ref.at[oth], sem_g.at[oth]).start()

        @pl.when(i == nv_ref[0])
        def _():
            start_scatter(i - 1, oth)

        @pl.when(i > nv_ref[0])
        def _():
            pltpu.make_async_copy(ybuf_ref.at[oth], o_hbm.at[pl.ds((i - 1) * tm, tm)], sem_s.at[oth]).start()

    @pl.when(i == n - 1)
    def _():
        wait_gather(oth)
        wait_scatter(oth)
        start_scatter(i, cur)
        wait_scatter(cur)


def _moe(tile_expert, n_valid, inv, h1_rows, w_gate, b_gate, w_up, b_up, w_down, b_down, *, tm):
    t, _, dm = h1_rows.shape
    _, _, de = w_gate.shape
    n_rows = inv.shape[0]
    per_e = lambda i, te, nv, iv: (te[i], 0, 0)
    return pl.pallas_call(
        functools.partial(_moe_kernel, n_tokens=t),
        grid_spec=pltpu.PrefetchScalarGridSpec(
            num_scalar_prefetch=3, grid=(n_rows // tm,),
            in_specs=[pl.BlockSpec(memory_space=pl.ANY),
                      pl.BlockSpec((1, dm, de), per_e), pl.BlockSpec((1, 1, de), per_e),
                      pl.BlockSpec((1, dm, de), per_e), pl.BlockSpec((1, 1, de), per_e),
                      pl.BlockSpec((1, de, dm), per_e), pl.BlockSpec((1, 1, dm), per_e)],
            out_specs=pl.BlockSpec(memory_space=pl.ANY),
            scratch_shapes=[pltpu.VMEM((2, tm, 1, dm), F32), pltpu.VMEM((2, tm, 1, dm), F32),
                            pltpu.VMEM((tm, dm), F32), pltpu.VMEM((tm, dm), BF16), pltpu.VMEM((tm, de), BF16),
                            pltpu.VMEM((dm, de), BF16),
                            pltpu.VMEM((dm, de), BF16), pltpu.VMEM((de, dm), BF16),
                            pltpu.SemaphoreType.DMA((2,)), pltpu.SemaphoreType.DMA((2,))]),
        out_shape=jax.ShapeDtypeStruct((n_rows, 1, dm), F32),
        compiler_params=pltpu.CompilerParams(
            dimension_semantics=("arbitrary",), vmem_limit_bytes=56 * 1024 * 1024),
        name="moe_experts",
    )(tile_expert, n_valid, inv, h1_rows, w_gate, b_gate, w_up, b_up, w_down, b_down)


def _combine_kernel(*refs, alpha):
    y_refs = refs[:TOP_K]
    h1_ref, rw_ref, l2g_ref, l2b_ref, o_ref, rows2_ref = refs[TOP_K:]
    tm, dm = o_ref.shape
    rw = rw_ref[...]
    ffn = jnp.zeros((tm, dm), F32)
    for kk in range(TOP_K):
        rows2_ref[...] = y_refs[kk][...].reshape(tm, dm)
        ffn = ffn + rw[:, kk:kk + 1] * rows2_ref[...]
    rows2_ref[...] = h1_ref[...].reshape(tm, dm)
    o_ref[...] = _layer_norm(alpha * rows2_ref[...] + ffn, l2g_ref[...], l2b_ref[...])


def _combine(y_slots, h1_rows, rw, l2g, l2b, *, tm, alpha):
    t, _, dm = h1_rows.shape
    rows = lambda i: (i, 0)
    const = lambda i: (0, 0)
    slot_specs = [pl.BlockSpec((tm, 1, dm), functools.partial(lambda i, kk: (kk * (t // tm) + i, 0, 0), kk=kk))
                  for kk in range(TOP_K)]
    return pl.pallas_call(
        functools.partial(_combine_kernel, alpha=alpha),
        grid=(t // tm,),
        in_specs=slot_specs + [pl.BlockSpec((tm, 1, dm), lambda i: (i, 0, 0)),
                               pl.BlockSpec((tm, LANES), rows),
                               pl.BlockSpec((1, dm), const), pl.BlockSpec((1, dm), const)],
        out_specs=pl.BlockSpec((tm, dm), rows),
        out_shape=jax.ShapeDtypeStruct((t, dm), F32),
        scratch_shapes=[pltpu.VMEM((tm, dm), F32)],
        compiler_params=pltpu.CompilerParams(
            dimension_semantics=("arbitrary",), vmem_limit_bytes=48 * 1024 * 1024),
        name="moe_combine_ln2",
    )(*([y_slots] * TOP_K), h1_rows, rw, l2g, l2b)


INVERT_CHUNK = 8192


def _invert_kernel(rows_ref, inv_ref, *, n_tokens):
    i = pl.program_id(0)
    ch = rows_ref.shape[0]
    first = i * ch

    @pl.when(first < TOP_K * n_tokens)
    def _():
        t0 = first // TOP_K

        def body(m, c):
            base = t0 + 2 * m
            for q in range(2 * TOP_K):
                inv_ref[rows_ref[2 * TOP_K * m + q]] = base + ((q % TOP_K) * n_tokens + q // TOP_K)
            return c
        lax.fori_loop(0, ch // (2 * TOP_K), body, 0, unroll=4)

    @pl.when(first >= TOP_K * n_tokens)
    def _():
        def body(p, c):
            inv_ref[rows_ref[p]] = first + p
            return c
        lax.fori_loop(0, ch, body, 0, unroll=8)


def _invert(rows, n_tokens):
    n = rows.shape[0]
    ch = math.gcd(math.gcd(n, TOP_K * n_tokens), INVERT_CHUNK)
    assert ch % TOP_K == 0
    return pl.pallas_call(
        functools.partial(_invert_kernel, n_tokens=n_tokens),
        grid=(n // ch,),
        in_specs=[pl.BlockSpec((ch,), lambda i: (i,), memory_space=pltpu.SMEM)],
        out_specs=pl.BlockSpec(memory_space=pltpu.SMEM),
        out_shape=jax.ShapeDtypeStruct((n,), jnp.int32),
        compiler_params=pltpu.CompilerParams(dimension_semantics=("arbitrary",)),
        name="moe_row_slots",
    )(rows)


def _routing_tables(ridx, counts, *, n_experts, tm):
    t = ridx.shape[0]
    n_real = TOP_K * t
    n_fill = n_experts * tm
    e_idx = ridx[:, :TOP_K]
    rank = ridx[:, TOP_K:2 * TOP_K]
    cnt = counts[0, :n_experts].astype(jnp.int32)
    padded = (cnt + tm - 1) // tm * tm
    ends = jnp.cumsum(padded)
    offs = ends - padded
    experts = jnp.arange(n_experts, dtype=jnp.int32)
    base = jnp.sum(jnp.where(e_idx[..., None] == experts, offs, 0), axis=-1)
    token_rows = (base + rank).reshape(-1)
    fill_cnt = padded - cnt
    fill_end = jnp.cumsum(fill_cnt)
    shift = jnp.concatenate([offs + cnt - (fill_end - fill_cnt), jnp.full((1,), n_real, jnp.int32)])
    j = jnp.arange(n_fill, dtype=jnp.int32)
    fill_rows = j + shift[0] + jnp.sum(
        jnp.where(j[:, None] >= fill_end[None, :], (shift[1:] - shift[:-1])[None, :], 0), axis=1)
    rows = jnp.concatenate([token_rows, fill_rows])
    inv = _invert(rows, t)
    n_tiles = (n_real + n_fill) // tm
    tile_start = jnp.arange(n_tiles, dtype=jnp.int32) * tm
    tile_expert = jnp.minimum(jnp.sum(ends[None, :] <= tile_start[:, None], axis=1), n_experts - 1)
    n_valid = (ends[-1] // tm).reshape(1)
    return inv.astype(jnp.int32), tile_expert.astype(jnp.int32), n_valid.astype(jnp.int32)


def _pick_tile(n, want):
    t = min(n, want)
    assert n % t == 0, (n, t)
    return t


def kernel(x, meta, ln_in_g, ln_in_b, w_in, b_f, w_up_a, w_up_b, w_o, ssm_a_re, ssm_a_im, ssm_log_dt,
           ssm_b_re, ssm_b_im, ssm_c_re, ssm_c_im, ssm_d, w_glu, b_glu, ln1_g, ln1_b, w_router, b_router,
           w_gate, b_gate, w_up, b_up, w_down, b_down, ln2_g, ln2_b):
    bsz, seq, dm = x.shape
    depth = w_in.shape[0]
    assert depth == 1 and meta.shape[0] == N_META
    alpha = (2.0 * depth) ** 0.25
    n_groups = ssm_a_re.shape[1]
    n_experts = w_router.shape[-1]
    aw = ATT_WIDTH
    t = bsz * seq
    x2 = x.reshape(t, dm)
    row = lambda a: a.reshape(1, -1)

    f_off = 3 * aw
    u_off = f_off + ATT_HEADS
    w0 = w_in[0]
    w_main = jnp.concatenate([w0[:, :f_off], w0[:, u_off:]], axis=1).astype(BF16)
    w_f = jnp.pad(w0[:, f_off:u_off], ((0, 0), (0, LANES - ATT_HEADS))).astype(BF16)
    b_f_pad = jnp.pad(b_f[0], (0, LANES - ATT_HEADS)).reshape(1, LANES)
    ln_g, ln_b = row(ln_in_g), row(ln_in_b)

    tm = _pick_tile(seq, 512)
    q_t, k, v_t, u, ga, gb, _ = _inproj(x2, ln_g, ln_b, w_main, w_f, b_f_pad, tm=tm, tiles_per_seq=seq // tm,
                                        transpose_qv=True)
    _, k_m, v_m, u_m, _, _, fcum_m = _inproj(meta, ln_g, ln_b, w_main, w_f, b_f_pad, tm=N_META, tiles_per_seq=1,
                                             transpose_qv=False)

    pad_m = META_KEYS_PAD - N_META
    fm = fcum_m[:, :ATT_HEADS]
    bias_m = jnp.pad(-LOG2E * (fm - fm[N_META - 1:N_META, :]), ((0, pad_m), (0, 0)), constant_values=-BIG)
    k_meta = jnp.pad(k_m, ((0, pad_m), (0, 0))).reshape(META_KEYS_PAD, ATT_HEADS, LANES)
    lane_m = jnp.arange(LANES)
    for n_piece, piece in enumerate(_split3(bias_m)):
        k_meta = jnp.where(lane_m == HEAD_DIM + n_piece, piece[..., None], k_meta)
    k_meta = k_meta.reshape(META_KEYS_PAD, ATT_HEADS * LANES)
    v_meta_t = jnp.pad(v_m, ((0, pad_m), (0, 0))).T
    tq = _pick_tile(seq, 512)
    y_a = _attn(q_t, k, v_t, k_meta, v_meta_t, bsz=bsz, seq=seq, tq=tq)

    chunk = SSM_CHUNK
    n_chunks = seq // chunk
    n_steps = max(1, (n_chunks - 1).bit_length())
    kt, wst, wout, wmeta, apow, d_tiled = _ssm_tables(
        ssm_a_re[0], ssm_a_im[0], ssm_log_dt[0], ssm_b_re[0], ssm_b_im[0], ssm_c_re[0], ssm_c_im[0], ssm_d[0],
        chunk=chunk, n_steps=n_steps)
    u_g = jnp.transpose(u.reshape(bsz * n_chunks, chunk, n_groups, SSM_GROUP), (2, 0, 1, 3))
    u_g = u_g.reshape(n_groups, bsz * n_chunks, chunk * SSM_GROUP)
    um_g = jnp.transpose(u_m.reshape(N_META, n_groups, SSM_GROUP), (1, 0, 2)).reshape(n_groups, 1, N_META * SSM_GROUP)
    um_g = jnp.pad(um_g, ((0, 0), (0, 15), (0, 0)))
    y_g = _ssm(u_g, kt, wst, wout, um_g, wmeta, apow, d_tiled, chunks_per_seq=n_chunks)
    y_b = jnp.transpose(y_g.reshape(n_groups, bsz * n_chunks, chunk, SSM_GROUP), (1, 2, 0, 3)).reshape(t, -1)

    wr = jnp.pad(w_router[0], ((0, 0), (0, LANES - n_experts)))
    wr_hi = wr.astype(BF16)
    wr_lo = (wr - wr_hi.astype(F32)).astype(BF16)
    b_r = jnp.pad(b_router[0], (0, LANES - n_experts)).reshape(1, LANES)
    tm2 = _pick_tile(t, 512)
    h1_rows, ridx, rw, counts = _postmix(
        x2, y_a, y_b, ga, gb, ln_g, ln_b, w_glu[0].astype(BF16), row(b_glu[0]),
        w_up_a[0].astype(BF16), w_up_b[0].astype(BF16), w_o[0].astype(BF16), row(ln1_g[0]), row(ln1_b[0]),
        wr_hi, wr_lo, b_r, tm=tm2, alpha=alpha, n_experts=n_experts)

    tm3 = MOE_ROW_TILE
    inv, tile_expert, n_valid = _routing_tables(ridx, counts, n_experts=n_experts, tm=tm3)
    y_slots = _moe(tile_expert, n_valid, inv, h1_rows, w_gate[0], b_gate[0][:, None, :], w_up[0],
                   b_up[0][:, None, :], w_down[0], b_down[0][:, None, :], tm=tm3)
    tm4 = _pick_tile(t, 512)
    out = _combine(y_slots, h1_rows, rw, row(ln2_g[0]), row(ln2_b[0]), tm=tm4, alpha=alpha)
    return out.reshape(bsz, seq, dm)
```
